```python
import math
import jax, jax.numpy as jnp
from jax import lax
import numpy as np

D_MODEL = 1024
BATCH = 16
SEQ = 4096
DEPTH = 1
DEC_BATCH = 32
DEC_SEQ = 32
PAST_LEN = 1024

CHUNK = 64
Q_BLOCK = 128
H_A = 8
DH_A = 64
WA = H_A * DH_A
H_B = 8
DK_B = 64
DV_B = 128
QK_B = H_B * DK_B
V_B = H_B * DV_B
CONV_W = 4
CONV_DIM = 2 * QK_B + V_B
N_GROUPS = 4
EXPERTS_PER_GROUP = 8
N_EXPERTS = N_GROUPS * EXPERTS_PER_GROUP
TOP_K = 2
D_EXPERT = 512
MOE_BLOCK = 128
EPS = 1e-6
IN_SPLITS = (WA, WA, WA, H_A, QK_B, QK_B, V_B, H_B, H_B, V_B, D_MODEL, D_MODEL)
D_IN = 3 * WA + H_A + 2 * QK_B + V_B + 2 * H_B + V_B + 2 * D_MODEL

kernel_name = "fox_gdn_hier_moe_streaming_step"


def rmsnorm(x, g):
    xf = x.astype(jnp.float32)
    y = xf * lax.rsqrt(jnp.mean(xf * xf, axis=-1, keepdims=True) + EPS) * g.astype(jnp.float32)
    return y.astype(x.dtype)


def l2norm(x):
    xf = x.astype(jnp.float32)
    return xf * lax.rsqrt(jnp.sum(xf * xf, axis=-1, keepdims=True) + EPS)


def causal_conv(u, buf, w):
    T = u.shape[1]
    full = jnp.concatenate([buf.astype(u.dtype), u], axis=1)
    out = full[:, 0:T] * w[0]
    for i in range(1, CONV_W):
        out = out + full[:, i:i + T] * w[i]
    new_buf = full[:, full.shape[1] - (CONV_W - 1):]
    return jax.nn.silu(out), new_buf


def fox_attend(q, k, v, c_q, c_k, pos_q, pos_k):
    B, Tq, H, dh = q.shape
    blk = Q_BLOCK if Tq % Q_BLOCK == 0 else Tq
    nb = Tq // blk
    scale = dh ** -0.5
    c_kt = c_k.transpose(0, 2, 1)
    qb = q.reshape(B, nb, blk, H, dh).swapaxes(0, 1)
    cqb = c_q.reshape(B, nb, blk, H).transpose(1, 0, 3, 2)
    pqb = pos_q.reshape(nb, blk)

    def one_block(args):
        q_n, cq_n, pq_n = args
        s = jnp.einsum('bqhd,bkhd->bhqk', q_n, k, preferred_element_type=jnp.float32) * scale
        s = s + cq_n[..., :, None] - c_kt[..., None, :]
        s = jnp.where(pos_k[None, :] <= pq_n[:, None], s, -jnp.inf)
        p = jax.nn.softmax(s, axis=-1)
        return jnp.einsum('bhqk,bkhd->bqhd', p.astype(v.dtype), v)

    o = lax.map(one_block, (qb, cqb, pqb))
    return o.swapaxes(0, 1).reshape(B, Tq, H, dh)


def gdn_chunked(q, k, v, g, beta, s0):
    f32 = jnp.float32
    B, T, H, dk = q.shape
    dv = v.shape[-1]
    L = CHUNK if T % CHUNK == 0 else T
    N = T // L

    def blocks(a):
        a = a.astype(f32).reshape((B, N, L) + a.shape[2:])
        return jnp.moveaxis(a, (1, 3), (0, 2))

    qc, kc, vc, gc, bc = blocks(q), blocks(k), blocks(v), blocks(g), blocks(beta)
    G = jnp.cumsum(gc, axis=-1)
    eG = jnp.exp(G)
    idx = jnp.arange(L)
    incl = idx[:, None] >= idx[None, :]
    strict = idx[:, None] > idx[None, :]
    decay = jnp.exp(jnp.where(incl, G[..., :, None] - G[..., None, :], -jnp.inf))
    kb = kc * bc[..., None]
    m = jnp.where(strict, jnp.einsum('nbhid,nbhjd->nbhij', kb, kc) * decay, 0.0)
    a = m + jnp.eye(L, dtype=f32)
    rhs = jnp.concatenate([vc * bc[..., None], kb * eG[..., None]], axis=-1)
    sol = lax.linalg.triangular_solve(a, rhs, left_side=True, lower=True, unit_diagonal=True)
    u, w = sol[..., :dv], sol[..., dv:]
    attn = jnp.where(incl, jnp.einsum('nbhid,nbhjd->nbhij', qc, kc) * decay, 0.0)
    q_dec = qc * eG[..., None]
    k_dec = kc * jnp.exp(G[..., -1:] - G)[..., None]
    g_last = jnp.exp(G[..., -1])

    def step(s, xs):
        u_n, w_n, attn_n, qd_n, kd_n, gl_n = xs
        v_new = u_n - jnp.einsum('bhld,bhdv->bhlv', w_n, s)
        o_n = jnp.einsum('bhld,bhdv->bhlv', qd_n, s) + jnp.einsum('bhij,bhjv->bhiv', attn_n, v_new)
        s = s * gl_n[..., None, None] + jnp.einsum('bhld,bhlv->bhdv', kd_n, v_new)
        return s, o_n

    s_final, o = lax.scan(step, s0.astype(f32), (u, w, attn, q_dec, k_dec, g_last))
    o = jnp.moveaxis(o, (0, 2), (1, 3)).reshape(B, T, H, dv)
    return o, s_final


def hier_moe(x, w_grp, b_grp, w_exp, b_exp, w_e1, w_e3, w_e2):
    f32 = jnp.float32
    B, T, D = x.shape
    NT = B * T
    xt = x.reshape(NT, D)
    rows = jnp.arange(NT)
    grp_logits = jnp.einsum('td,dg->tg', xt, w_grp).astype(f32) + b_grp.astype(f32)
    grp_prob = jax.nn.softmax(grp_logits, axis=-1)
    g_sel = jnp.argmax(grp_logits, axis=-1)
    p_g = grp_prob[rows, g_sel]
    e_logits = (jnp.einsum('td,de->te', xt, w_exp).astype(f32) + b_exp.astype(f32)).reshape(NT, N_GROUPS, EXPERTS_PER_GROUP)
    e_in_group = e_logits[rows, g_sel]
    top_val, top_idx = lax.top_k(e_in_group, TOP_K)
    p_e = jax.nn.softmax(top_val, axis=-1)
    expert_id = g_sel[:, None] * EXPERTS_PER_GROUP + top_idx
    weights = p_g[:, None] * p_e

    A = NT * TOP_K
    flat_e = expert_id.reshape(A).astype(jnp.int32)
    flat_tok = (jnp.arange(A) // TOP_K).astype(jnp.int32)
    flat_w = weights.reshape(A)
    order = jnp.argsort(flat_e)
    e_sorted = flat_e[order]
    counts = jax.ops.segment_sum(jnp.ones((A,), jnp.int32), flat_e, num_segments=N_EXPERTS)
    padded = ((counts + MOE_BLOCK - 1) // MOE_BLOCK) * MOE_BLOCK
    pad_end = jnp.cumsum(padded)
    pad_start = pad_end - padded
    start = jnp.cumsum(counts) - counts
    dest = pad_start[e_sorted] + jnp.arange(A, dtype=jnp.int32) - start[e_sorted]
    NB = -(-A // MOE_BLOCK) + N_EXPERTS
    P = NB * MOE_BLOCK
    slot_tok = jnp.full((P,), NT, jnp.int32).at[dest].set(flat_tok[order])
    slot_w = jnp.zeros((P,), f32).at[dest].set(flat_w[order])
    blk_expert = jnp.minimum(jnp.searchsorted(pad_end, jnp.arange(NB, dtype=jnp.int32) * MOE_BLOCK, side='right'), N_EXPERTS - 1)
    x_pad = jnp.concatenate([xt, jnp.zeros((1, D), xt.dtype)], axis=0)

    def run_block(args):
        tok, e, wt = args
        xb = x_pad[tok]
        h = jax.nn.silu(xb @ w_e1[e]) * (xb @ w_e3[e])
        return (h @ w_e2[e]) * wt[:, None].astype(xb.dtype)

    yb = lax.map(run_block, (slot_tok.reshape(NB, MOE_BLOCK), blk_expert, slot_w.reshape(NB, MOE_BLOCK)))
    y = jnp.zeros((NT + 1, D), x.dtype).at[slot_tok].add(yb.reshape(P, D).astype(x.dtype))
    return y[:NT].reshape(B, T, D)


def hybrid_layer(x, past_k, past_v, past_logf, s0, conv_buf, p):
    f32 = jnp.float32
    B, T, _ = x.shape
    P = past_k.shape[1]
    h = rmsnorm(x, p["g_mix"])
    proj = jnp.einsum('btd,de->bte', h, p["w_in"])
    offsets = np.cumsum(np.array(IN_SPLITS))[:-1].tolist()
    fq, fk, fv, flog, bq, bk, bv, bbeta, ba, bz, ga, gb = jnp.split(proj, offsets, axis=-1)

    fq = rmsnorm(fq.reshape(B, T, H_A, DH_A), p["g_qnorm"])
    fk = rmsnorm(fk.reshape(B, T, H_A, DH_A), p["g_knorm"])
    fv = fv.reshape(B, T, H_A, DH_A)
    logf = jax.nn.log_sigmoid(flog.astype(f32) + p["b_f"].astype(f32))
    k_all = jnp.concatenate([past_k.astype(fk.dtype), fk], axis=1)
    v_all = jnp.concatenate([past_v.astype(fv.dtype), fv], axis=1)
    c_all = jnp.cumsum(jnp.concatenate([past_logf.astype(f32), logf], axis=1), axis=1)
    pos_k = jnp.arange(P + T)
    y_a = fox_attend(fq, k_all, v_all, c_all[:, P:], c_all, pos_k[P:], pos_k)

    conv_out, new_buf = causal_conv(jnp.concatenate([bq, bk, bv], axis=-1), conv_buf, p["w_conv"])
    cq, ck, cv = jnp.split(conv_out, [QK_B, 2 * QK_B], axis=-1)
    cq = l2norm(cq.reshape(B, T, H_B, DK_B)) * (DK_B ** -0.5)
    ck = l2norm(ck.reshape(B, T, H_B, DK_B))
    cv = cv.reshape(B, T, H_B, DV_B)
    beta = jax.nn.sigmoid(bbeta.astype(f32))
    g = -jnp.exp(p["a_log"].astype(f32)) * jax.nn.softplus(ba.astype(f32) + p["dt_bias"].astype(f32))
    o_b, s_new = gdn_chunked(cq, ck, cv, g, beta, s0)
    y_b = (rmsnorm(o_b, p["g_onorm"]) * jax.nn.silu(bz.reshape(B, T, H_B, DV_B).astype(f32))).astype(x.dtype)

    gates = jax.nn.sigmoid(jnp.concatenate([ga, gb], axis=-1).astype(f32) + p["b_gate"].astype(f32)).astype(x.dtype)
    gate_a, gate_b = gates[..., :D_MODEL], gates[..., D_MODEL:]
    branch_a = jnp.einsum('bte,ed->btd', y_a.reshape(B, T, WA), p["w_br_a"])
    branch_b = jnp.einsum('bte,ed->btd', y_b.reshape(B, T, V_B), p["w_br_b"])
    x1 = x + jnp.einsum('btd,de->bte', gate_a * branch_a + gate_b * branch_b, p["w_out"])

    x2 = x1 + hier_moe(rmsnorm(x1, p["g_ffn"]), p["w_grp"], p["b_grp"], p["w_exp"], p["b_exp"],
                       p["w_e1"], p["w_e3"], p["w_e2"])
    return x2, fk, fv, logf, s_new, new_buf


def setup_inputs(seed: int = 0) -> dict:
    key = jax.random.key(seed)
    ks = jax.random.split(key, 32)
    f32 = jnp.float32
    L = DEPTH

    def nrm(k, shape, scale):
        return jax.random.normal(k, shape, f32) * scale

    def gain(k, shape):
        return 1.0 + 0.05 * jax.random.normal(k, shape, f32)

    dt = jnp.exp(jax.random.uniform(ks[13], (L, H_B), f32, math.log(0.001), math.log(0.1)))
    return {
        "x_prompt": nrm(ks[0], (BATCH, SEQ, D_MODEL), 1.0),
        "x_sample": nrm(ks[1], (DEC_BATCH, DEC_SEQ, D_MODEL), 1.0),
        "cache_fox_k": nrm(ks[2], (L, DEC_BATCH, PAST_LEN, H_A, DH_A), 1.0),
        "cache_fox_v": nrm(ks[3], (L, DEC_BATCH, PAST_LEN, H_A, DH_A), 1.0),
        "cache_fox_logf": jax.nn.log_sigmoid(1.0 + jax.random.normal(ks[4], (L, DEC_BATCH, PAST_LEN, H_A), f32)),
        "state_gdn": nrm(ks[5], (L, DEC_BATCH, H_B, DK_B, DV_B), 0.1),
        "state_gdn_conv": nrm(ks[6], (L, DEC_BATCH, CONV_W - 1, CONV_DIM), 1.0),
        "g_mix": gain(ks[7], (L, D_MODEL)),
        "w_in": nrm(ks[8], (L, D_MODEL, D_IN), D_MODEL ** -0.5),
        "b_f": 1.0 + 0.1 * jax.random.normal(ks[9], (L, H_A), f32),
        "g_qnorm": gain(ks[10], (L, DH_A)),
        "g_knorm": gain(ks[11], (L, DH_A)),
        "w_conv": nrm(ks[12], (L, CONV_W, CONV_DIM), CONV_W ** -0.5),
        "a_log": jnp.log(jax.random.uniform(ks[14], (L, H_B), f32, 1.0, 16.0)),
        "dt_bias": dt + jnp.log(-jnp.expm1(-dt)),
        "g_onorm": gain(ks[15], (L, DV_B)),
        "b_gate": nrm(ks[16], (L, 2 * D_MODEL), 0.1),
        "w_br_a": nrm(ks[17], (L, WA, D_MODEL), WA ** -0.5),
        "w_br_b": nrm(ks[18], (L, V_B, D_MODEL), V_B ** -0.5),
        "w_out": nrm(ks[19], (L, D_MODEL, D_MODEL), D_MODEL ** -0.5),
        "g_ffn": gain(ks[20], (L, D_MODEL)),
        "w_grp": nrm(ks[21], (L, D_MODEL, N_GROUPS), D_MODEL ** -0.5),
        "b_grp": nrm(ks[22], (L, N_GROUPS), 0.01),
        "w_exp": nrm(ks[23], (L, D_MODEL, N_EXPERTS), D_MODEL ** -0.5),
        "b_exp": nrm(ks[24], (L, N_EXPERTS), 0.01),
        "w_e1": nrm(ks[25], (L, N_EXPERTS, D_MODEL, D_EXPERT), D_MODEL ** -0.5),
        "w_e3": nrm(ks[26], (L, N_EXPERTS, D_MODEL, D_EXPERT), D_MODEL ** -0.5),
        "w_e2": nrm(ks[27], (L, N_EXPERTS, D_EXPERT, D_MODEL), D_EXPERT ** -0.5),
    }


def reference(x_prompt, x_sample, cache_fox_k, cache_fox_v, cache_fox_logf, state_gdn, state_gdn_conv,
              g_mix, w_in, b_f, g_qnorm, g_knorm, w_conv, a_log, dt_bias, g_onorm, b_gate,
              w_br_a, w_br_b, w_out, g_ffn, w_grp, b_grp, w_exp, b_exp, w_e1, w_e3, w_e2):
    f32 = jnp.float32
    Bp = x_prompt.shape[0]
    yp, ys = x_prompt, x_sample
    kp_l, vp_l, lp_l, sp_l, cp_l = [], [], [], [], []
    ks_l, vs_l, ls_l, ss_l, cs_l = [], [], [], [], []
    for l in range(DEPTH):
        p = {"g_mix": g_mix[l], "w_in": w_in[l], "b_f": b_f[l], "g_qnorm": g_qnorm[l], "g_knorm": g_knorm[l],
             "w_conv": w_conv[l], "a_log": a_log[l], "dt_bias": dt_bias[l], "g_onorm": g_onorm[l],
             "b_gate": b_gate[l], "w_br_a": w_br_a[l], "w_br_b": w_br_b[l], "w_out": w_out[l],
             "g_ffn": g_ffn[l], "w_grp": w_grp[l], "b_grp": b_grp[l], "w_exp": w_exp[l], "b_exp": b_exp[l],
             "w_e1": w_e1[l], "w_e3": w_e3[l], "w_e2": w_e2[l]}
        yp, kp, vp, lp, sp, cp = hybrid_layer(
            yp,
            jnp.zeros((Bp, 0, H_A, DH_A), yp.dtype), jnp.zeros((Bp, 0, H_A, DH_A), yp.dtype),
            jnp.zeros((Bp, 0, H_A), f32), jnp.zeros((Bp, H_B, DK_B, DV_B), f32),
            jnp.zeros((Bp, CONV_W - 1, CONV_DIM), yp.dtype), p)
        ys, kss, vss, lss, sss, css = hybrid_layer(
            ys, cache_fox_k[l], cache_fox_v[l], cache_fox_logf[l], state_gdn[l], state_gdn_conv[l], p)
        kp_l.append(kp); vp_l.append(vp); lp_l.append(lp); sp_l.append(sp); cp_l.append(cp)
        ks_l.append(kss); vs_l.append(vss); ls_l.append(lss); ss_l.append(sss); cs_l.append(css)
    return (yp, ys,
            jnp.stack(kp_l), jnp.stack(vp_l), jnp.stack(lp_l), jnp.stack(sp_l), jnp.stack(cp_l),
            jnp.stack(ks_l), jnp.stack(vs_l), jnp.stack(ls_l), jnp.stack(ss_l), jnp.stack(cs_l))
```

```python
import functools
import math

import jax
import jax.numpy as jnp
from jax import lax
from jax.experimental import pallas as pl
from jax.experimental.pallas import tpu as pltpu

F32, BF16, I32, U32 = jnp.float32, jnp.bfloat16, jnp.int32, jnp.uint32

D_MODEL = 1024
H_A, DH_A = 8, 64
WA = H_A * DH_A
H_B, DK_B, DV_B = 8, 64, 128
QK_B, V_B = H_B * DK_B, H_B * DV_B
CONV_W = 4
CONV_DIM = 2 * QK_B + V_B
N_GROUPS, EXPERTS_PER_GROUP = 4, 8
N_EXPERTS = N_GROUPS * EXPERTS_PER_GROUP
D_EXPERT = 512
GDN_CHUNK = 64
EPS = 1e-6
NEG = -1e30
LANES = 128
V7X_VMEM_LIMIT = 56 * 1024 * 1024
MOE_BLOCK = 256


def _params(*sem):
    return pltpu.CompilerParams(dimension_semantics=sem, vmem_limit_bytes=V7X_VMEM_LIMIT)


def _pick_tile(n, cap, mult=8):
    if n <= cap:
        return n
    for t in range(cap - cap % mult, 0, -mult):
        if n % t == 0:
            return t
    return n


def _dot(a, b):
    return jnp.dot(a.astype(BF16), b.astype(BF16), preferred_element_type=F32)


def _dot_t(a, b):
    return lax.dot_general(a.astype(BF16), b.astype(BF16), (((1,), (1,)), ((), ())), preferred_element_type=F32)


def _dot_tl(a, b):
    return lax.dot_general(a.astype(BF16), b.astype(BF16), (((0,), (0,)), ((), ())), preferred_element_type=F32)


def _split(x, n):
    parts = []
    for _ in range(n):
        p = x.astype(BF16)
        parts.append(p)
        x = x - p.astype(F32)
    return parts


def _dot_exact_lhs(ones, x, n):
    acc = None
    for p in _split(x, n):
        t = jnp.dot(ones, p, preferred_element_type=F32)
        acc = t if acc is None else acc + t
    return acc


def _dot_exact_rhs(x, ones, n):
    acc = None
    for p in _split(x, n):
        t = jnp.dot(p, ones, preferred_element_type=F32)
        acc = t if acc is None else acc + t
    return acc


def _iota(shape, dim):
    return lax.broadcasted_iota(I32, shape, dim)


def _softplus(x):
    return jnp.maximum(x, 0.0) + jnp.log1p(jnp.exp(-jnp.abs(x)))


def _silu(x):
    return x * jax.nn.sigmoid(x)


def _rms_rows(x, g):
    return x * lax.rsqrt(jnp.mean(x * x, axis=-1, keepdims=True) + EPS) * g


def _head_ones(width, head):
    shift = head.bit_length() - 1
    assert head == 1 << shift
    r = jnp.right_shift(_iota((width, width), 0), shift)
    c = jnp.right_shift(_iota((width, width), 1), shift)
    return (r == c).astype(BF16)


def _group_sumsq(y, head):
    ones = _head_ones(2 * LANES, head)
    sq = y * y
    parts = [_dot_exact_rhs(sq[:, c:c + 2 * LANES], ones, 2) for c in range(0, y.shape[1], 2 * LANES)]
    return parts[0] if len(parts) == 1 else jnp.concatenate(parts, axis=1)


def _inproj_fox_kernel(x_ref, g_ref, wq_ref, wk_ref, wv_ref, gq_ref, gk_ref,
                       q_ref, k32_ref, k16_ref, v32_ref, v16_ref):
    h = _rms_rows(x_ref[...], g_ref[...]).astype(BF16)

    def headnorm(y, gain):
        ms = _group_sumsq(y, DH_A) * (1.0 / DH_A)
        return y * lax.rsqrt(ms + EPS) * gain

    q = headnorm(jnp.dot(h, wq_ref[...], preferred_element_type=F32), gq_ref[...])
    q_ref[...] = (q * (DH_A ** -0.5)).astype(BF16)
    k = headnorm(jnp.dot(h, wk_ref[...], preferred_element_type=F32), gk_ref[...])
    k32_ref[...] = k
    k16_ref[...] = k.astype(BF16)
    v = jnp.dot(h, wv_ref[...], preferred_element_type=F32)
    v32_ref[...] = v
    v16_ref[...] = v.astype(BF16)


def _small_epilogue(y, bias, a_log, idx):
    yb = y + bias
    logf = -_softplus(-yb)
    beta = jax.nn.sigmoid(y)
    g = -jnp.exp(a_log) * _softplus(yb)
    return jnp.where(idx < H_A, logf, jnp.where(idx < H_A + H_B, beta, g))


def _inproj_gdn_kernel(x_ref, g_ref, ws_ref, wst_ref, wu_ref, bias_ref, alog_ref, biast_ref, alogt_ref,
                       small_ref, smallt_ref, u_ref):
    h = _rms_rows(x_ref[...], g_ref[...]).astype(BF16)
    ys = jnp.dot(h, ws_ref[...], preferred_element_type=F32)
    small_ref[...] = _small_epilogue(ys, bias_ref[...], alog_ref[...], _iota(ys.shape, 1))
    yt = lax.dot_general(wst_ref[...], h, (((1,), (1,)), ((), ())), preferred_element_type=F32)
    smallt_ref[...] = _small_epilogue(yt, biast_ref[...], alogt_ref[...], _iota(yt.shape, 0))
    for c in range(0, CONV_DIM, 512):
        u_ref[:, c:c + 512] = jnp.dot(h, wu_ref[:, c:c + 512], preferred_element_type=F32)


def _inproj_gate_kernel(x_ref, g_ref, wz_ref, wg_ref, bg_ref, z_ref, gate_ref):
    h = _rms_rows(x_ref[...], g_ref[...]).astype(BF16)
    for c in range(0, V_B, 512):
        z_ref[:, c:c + 512] = jnp.dot(h, wz_ref[:, c:c + 512], preferred_element_type=F32)
    for c in range(0, 2 * D_MODEL, 512):
        y = jnp.dot(h, wg_ref[:, c:c + 512], preferred_element_type=F32)
        gate_ref[:, c:c + 512] = jax.nn.sigmoid(y + bg_ref[:, c:c + 512])


def _row_call(kernel, n, tm, row_inputs, full_inputs, outs, name):
    in_specs = [pl.BlockSpec((tm, a.shape[1]), lambda i: (i, 0)) for a in row_inputs]
    in_specs += [pl.BlockSpec(a.shape, lambda i, nd=a.ndim: (0,) * nd) for a in full_inputs]
    out_specs, out_shapes = [], []
    for shape, dtype, transposed in outs:
        if transposed:
            out_specs.append(pl.BlockSpec((shape[0], tm), lambda i: (0, i)))
        else:
            out_specs.append(pl.BlockSpec((tm, shape[1]), lambda i: (i, 0)))
        out_shapes.append(jax.ShapeDtypeStruct(shape, dtype))
    return pl.pallas_call(kernel, grid=(n // tm,), in_specs=in_specs, out_specs=out_specs, out_shape=out_shapes,
                          compiler_params=_params("arbitrary"), name=name)(*row_inputs, *full_inputs)


def _cumsum_kernel(x_ref, o_ref, carry):
    @pl.when(pl.program_id(1) == 0)
    def _():
        carry[...] = jnp.zeros_like(carry)

    x = x_ref[...]
    t = x.shape[0]
    tril = (_iota((t, t), 0) >= _iota((t, t), 1)).astype(BF16)
    c = _dot_exact_lhs(tril, x, 3) + carry[...]
    o_ref[...] = c
    carry[...] = c[t - 1:t, :]


def _cumsum_time(x):
    b, tt, c = x.shape
    tc = _pick_tile(tt, 512)
    return pl.pallas_call(
        _cumsum_kernel, grid=(b, tt // tc),
        in_specs=[pl.BlockSpec((None, tc, c), lambda i, j: (i, j, 0))],
        out_specs=pl.BlockSpec((None, tc, c), lambda i, j: (i, j, 0)),
        out_shape=jax.ShapeDtypeStruct(x.shape, F32),
        scratch_shapes=[pltpu.VMEM((1, c), F32)],
        compiler_params=_params("arbitrary", "arbitrary"), name="fox_cumsum")(x)


def _attn_kernel(qi_ref, kj_ref, last_ref, q_ref, k_ref, v_ref, cq_ref, ck_ref, o_ref, m_sc, l_sc, acc_sc, *, tq, tk, past):
    n = pl.program_id(1)
    i, j = qi_ref[n], kj_ref[n]

    @pl.when(j == 0)
    def _():
        m_sc[...] = jnp.full_like(m_sc, NEG)
        l_sc[...] = jnp.zeros_like(l_sc)
        acc_sc[...] = jnp.zeros_like(acc_sc)

    qpos = past + i * tq + _iota((tq, 1), 0)
    kpos = j * tk + _iota((1, tk), 1)
    visible = kpos <= qpos
    lo = _iota((1, LANES), 1) < DH_A
    cq = cq_ref[...]
    ck = ck_ref[...]
    for p in range(H_A // 2):
        sl = slice(p * LANES, (p + 1) * LANES)
        q2, k2, v2 = q_ref[:, sl], k_ref[:, sl], v_ref[:, sl]
        alphas, pvs = [], []
        for half, mask in ((0, lo), (1, jnp.logical_not(lo))):
            hd = 2 * p + half
            s = _dot_t(jnp.where(mask, q2, jnp.zeros_like(q2)), k2)
            s = s + cq[:, hd:hd + 1] - ck[hd:hd + 1, :]
            s = jnp.where(visible, s, NEG)
            m_prev = m_sc[hd]
            m_new = jnp.maximum(m_prev, jnp.max(s, axis=-1, keepdims=True))
            alpha = jnp.exp(m_prev - m_new)
            pr = jnp.exp(s - m_new)
            l_sc[hd] = alpha * l_sc[hd] + jnp.sum(pr, axis=-1, keepdims=True)
            m_sc[hd] = m_new
            alphas.append(alpha)
            pvs.append(_dot(pr, jnp.where(mask, v2, jnp.zeros_like(v2))))
        acc_sc[p] = jnp.where(lo, alphas[0], alphas[1]) * acc_sc[p] + pvs[0] + pvs[1]

    @pl.when(last_ref[n] == 1)
    def _():
        for p in range(H_A // 2):
            l2 = jnp.where(lo, l_sc[2 * p], l_sc[2 * p + 1])
            o_ref[:, p * LANES:(p + 1) * LANES] = (acc_sc[p] / l2).astype(BF16)


def _fox_attention(q, k_all, v_all, c_q, c_kt, b, t, past):
    tt = past + t
    tq = _pick_tile(t, 512, 16)
    tk = _pick_tile(tt, 512, LANES)
    nq, nk = t // tq, tt // tk
    pairs = [(i, j) for i in range(nq) for j in range(nk) if j * tk <= past + (i + 1) * tq - 1]
    qi = jnp.array([p[0] for p in pairs], I32)
    kj = jnp.array([p[1] for p in pairs], I32)
    last = jnp.array([1 if (idx + 1 == len(pairs) or pairs[idx + 1][0] != p[0]) else 0
                      for idx, p in enumerate(pairs)], I32)
    grid_spec = pltpu.PrefetchScalarGridSpec(
        num_scalar_prefetch=3, grid=(b, len(pairs)),
        in_specs=[
            pl.BlockSpec((tq, WA), lambda bi, n, qi, kj, la: (bi * nq + qi[n], 0)),
            pl.BlockSpec((tk, WA), lambda bi, n, qi, kj, la: (bi * nk + kj[n], 0)),
            pl.BlockSpec((tk, WA), lambda bi, n, qi, kj, la: (bi * nk + kj[n], 0)),
            pl.BlockSpec((tq, H_A), lambda bi, n, qi, kj, la: (bi * nq + qi[n], 0)),
            pl.BlockSpec((None, H_A, tk), lambda bi, n, qi, kj, la: (bi, 0, kj[n])),
        ],
        out_specs=pl.BlockSpec((tq, WA), lambda bi, n, qi, kj, la: (bi * nq + qi[n], 0)),
        scratch_shapes=[pltpu.VMEM((H_A, tq, 1), F32), pltpu.VMEM((H_A, tq, 1), F32),
                        pltpu.VMEM((H_A // 2, tq, LANES), F32)])
    return pl.pallas_call(
        functools.partial(_attn_kernel, tq=tq, tk=tk, past=past), grid_spec=grid_spec,
        out_shape=jax.ShapeDtypeStruct((b * t, WA), BF16),
        compiler_params=_params("arbitrary", "arbitrary"), name="fox_attention")(qi, kj, last, q, k_all, v_all, c_q, c_kt)


def _gdn_kernel(u_ref, small_ref, smallt_ref, z_ref, s0_ref, conv0_ref, wconv_ref, gon_ref,
                yb_ref, sout_ref, cout_ref, s_sc, ubuf, *, chunk, nchunks):
    c = pl.program_id(1)
    L = chunk
    keep = CONV_W - 1

    @pl.when(c == 0)
    def _():
        s_sc[...] = s0_ref[...]
        ubuf[8 - keep:8, :] = conv0_ref[...]

    ubuf[8:8 + L, :] = u_ref[...]
    w = wconv_ref[...]
    conv = ubuf[8 - keep:8 - keep + L, :] * w[0:1, :]
    for i in range(1, CONV_W):
        conv = conv + ubuf[8 - keep + i:8 - keep + i + L, :] * w[i:i + 1, :]
    act = _silu(conv)
    tail = ubuf[8 + L - keep:8 + L, :]
    ubuf[8 - keep:8, :] = tail

    @pl.when(c == nchunks - 1)
    def _():
        cout_ref[...] = tail

    sm = small_ref[...]
    smt = smallt_ref[...]
    row, col = _iota((L, L), 0), _iota((L, L), 1)
    incl, strict = row >= col, row > col
    g_col = _dot_exact_lhs(incl.astype(BF16), sm, 3)
    g_row = _dot_exact_rhs(smt, (row <= col).astype(BF16), 3)
    eye = (row == col).astype(F32)
    lo = _iota((1, LANES), 1) < DK_B
    n_sq = max(int(math.ceil(math.log2(L))) - 1, 0)

    for p in range(H_B // 2):
        sl = slice(p * LANES, (p + 1) * LANES)
        q2, k2 = act[:, sl], act[:, QK_B + p * LANES:QK_B + (p + 1) * LANES]
        qn = q2 * lax.rsqrt(_group_sumsq_128(q2) + EPS) * (DK_B ** -0.5)
        kn = k2 * lax.rsqrt(_group_sumsq_128(k2) + EPS)
        s2 = s_sc[sl, :]
        upd = None
        gls = []
        for half, mask in ((0, lo), (1, jnp.logical_not(lo))):
            hd = 2 * p + half
            gc = g_col[:, 2 * H_B + hd:2 * H_B + hd + 1]
            gr = g_row[2 * H_B + hd:2 * H_B + hd + 1, :]
            g_last = gc[L - 1:L, :]
            decay = jnp.exp(jnp.where(incl, gc - gr, NEG))
            e_g = jnp.exp(gc)
            beta = sm[:, H_B + hd:H_B + hd + 1]
            k_m = jnp.where(mask, kn, 0.0)
            q_m = jnp.where(mask, qn, 0.0)
            kb_m = k_m * beta
            m = jnp.where(strict, _dot_t(kb_m, kn) * decay, 0.0)
            t_inv = eye - m
            mp = m
            for _ in range(n_sq):
                mp = _dot(mp, mp)
                t_inv = t_inv + _dot(t_inv, mp)
            v_h = act[:, 2 * QK_B + hd * DV_B:2 * QK_B + (hd + 1) * DV_B]
            u_h = _dot(t_inv, v_h * beta)
            w_m = _dot(t_inv, kb_m * e_g)
            attn = jnp.where(incl, _dot_t(q_m, kn) * decay, 0.0)
            v_new = u_h - _dot(w_m, s2)
            o = _dot(q_m * e_g, s2) + _dot(attn, v_new)
            d_h = _dot_tl(k_m * jnp.exp(g_last - gc), v_new)
            upd = d_h if upd is None else upd + d_h
            gls.append(jnp.exp(g_last))
            on = o * lax.rsqrt(jnp.mean(o * o, axis=-1, keepdims=True) + EPS) * gon_ref[...]
            zs = z_ref[:, hd * DV_B:(hd + 1) * DV_B]
            yb_ref[:, hd * DV_B:(hd + 1) * DV_B] = (on * _silu(zs)).astype(BF16)
        gl2 = jnp.where(_iota((LANES, 1), 0) < DK_B, gls[0], gls[1])
        s_sc[sl, :] = s2 * gl2 + upd

    @pl.when(c == nchunks - 1)
    def _():
        sout_ref[...] = s_sc[...]


def _group_sumsq_128(y):
    ones = _head_ones(LANES, DK_B)
    return _dot_exact_rhs(y * y, ones, 2)


def _gdn(u, small, smallt, z, s0, conv0, w_conv, g_onorm, b, t):
    chunk = GDN_CHUNK if t % GDN_CHUNK == 0 else t
    nc = t // chunk
    smallt3 = smallt.reshape(smallt.shape[0], b * nc, chunk).transpose(1, 0, 2)
    keep = CONV_W - 1
    return pl.pallas_call(
        functools.partial(_gdn_kernel, chunk=chunk, nchunks=nc), grid=(b, nc),
        in_specs=[
            pl.BlockSpec((chunk, CONV_DIM), lambda i, j: (i * nc + j, 0)),
            pl.BlockSpec((chunk, LANES), lambda i, j: (i * nc + j, 0)),
            pl.BlockSpec((None, smallt.shape[0], chunk), lambda i, j: (i * nc + j, 0, 0)),
            pl.BlockSpec((chunk, V_B), lambda i, j: (i * nc + j, 0)),
            pl.BlockSpec((None, H_B * DK_B, DV_B), lambda i, j: (i, 0, 0)),
            pl.BlockSpec((None, keep, CONV_DIM), lambda i, j: (i, 0, 0)),
            pl.BlockSpec((CONV_W, CONV_DIM), lambda i, j: (0, 0)),
            pl.BlockSpec((1, DV_B), lambda i, j: (0, 0)),
        ],
        out_specs=[
            pl.BlockSpec((chunk, V_B), lambda i, j: (i * nc + j, 0)),
            pl.BlockSpec((None, H_B * DK_B, DV_B), lambda i, j: (i, 0, 0)),
            pl.BlockSpec((None, keep, CONV_DIM), lambda i, j: (i, 0, 0)),
        ],
        out_shape=[jax.ShapeDtypeStruct((b * t, V_B), BF16),
                   jax.ShapeDtypeStruct((b, H_B * DK_B, DV_B), F32),
                   jax.ShapeDtypeStruct((b, keep, CONV_DIM), F32)],
        scratch_shapes=[pltpu.VMEM((H_B * DK_B, DV_B), F32), pltpu.VMEM((8 + chunk, CONV_DIM), F32)],
        compiler_params=_params("arbitrary", "arbitrary"), name="gdn_chunked",
    )(u, small, smallt3, z, s0, conv0, w_conv, g_onorm)


def _merge_kernel(x_ref, ya_ref, yb_ref, gate_ref, wa_ref, wb_ref, wo_ref, gf_ref, wr_ref, br_ref,
                  x1_ref, h2_ref, logit_ref):
    br_a = jnp.dot(ya_ref[...], wa_ref[...], preferred_element_type=F32)
    br_b = jnp.dot(yb_ref[...], wb_ref[...], preferred_element_type=F32)
    merged = gate_ref[:, :D_MODEL] * br_a + gate_ref[:, D_MODEL:] * br_b
    x1 = x_ref[...] + _dot(merged, wo_ref[...])
    x1_ref[...] = x1
    h2 = _rms_rows(x1, gf_ref[...])
    h2_ref[...] = h2
    logit_ref[...] = _dot(h2, wr_ref[...]) + br_ref[...]


def _route_kernel(logit_ref, meta_ref, wgt_ref, cnt_ref, run):
    @pl.when(pl.program_id(0) == 0)
    def _():
        run[...] = jnp.zeros_like(run)

    x = logit_ref[...]
    tm = x.shape[0]
    lane = _iota((1, LANES), 1)
    lanef = lane.astype(F32)
    big = float(LANES)

    def first_argmax(vals, valid):
        mx = jnp.max(vals, axis=-1, keepdims=True)
        idx = jnp.min(jnp.where(jnp.logical_and(vals == mx, valid), lanef, big), axis=-1, keepdims=True)
        return mx, idx.astype(I32)

    gmask = lane < N_GROUPS
    gmax, gsel = first_argmax(jnp.where(gmask, x, NEG), gmask)
    p_g = 1.0 / jnp.sum(jnp.where(gmask, jnp.exp(x - gmax), 0.0), axis=-1, keepdims=True)
    first = N_GROUPS + EXPERTS_PER_GROUP * gsel
    emask = jnp.logical_and(lane >= first, lane < first + EXPERTS_PER_GROUP)
    ev = jnp.where(emask, x, NEG)
    m1, i1 = first_argmax(ev, emask)
    emask2 = jnp.logical_and(emask, lane != i1)
    m2, i2 = first_argmax(jnp.where(emask2, x, NEG), emask2)
    e21 = jnp.exp(m2 - m1)
    w1 = p_g / (1.0 + e21)
    w2 = p_g * e21 / (1.0 + e21)
    e1, e2 = i1 - N_GROUPS, i2 - N_GROUPS
    oh1, oh2 = lane == e1, lane == e2
    onehot = jnp.logical_or(oh1, oh2).astype(BF16)
    strict = (_iota((tm, tm), 0) > _iota((tm, tm), 1)).astype(BF16)
    before = jnp.dot(strict, onehot, preferred_element_type=F32) + run[...]
    r1 = jnp.sum(jnp.where(oh1, before, 0.0), axis=-1, keepdims=True).astype(I32)
    r2 = jnp.sum(jnp.where(oh2, before, 0.0), axis=-1, keepdims=True).astype(I32)
    run[...] = run[...] + jnp.sum(onehot.astype(F32), axis=0, keepdims=True)
    meta_ref[...] = jnp.where(lane == 0, e1, jnp.where(lane == 1, e2, jnp.where(lane == 2, r1, jnp.where(lane == 3, r2, 0))))
    wgt_ref[...] = jnp.where(lane == 0, w1, jnp.where(lane == 1, w2, 0.0))
    cnt_ref[...] = run[...]


def _row_copy(src, dst, sem, s, d):
    return pltpu.make_async_copy(src.at[pl.ds(s, 1)], dst.at[pl.ds(d, 1)], sem)


def _dispatch_kernel(dest_ref, h_ref, zero_ref, xs_ref, sem, *, td):
    del zero_ref

    def issue(t, carry):
        _row_copy(h_ref, xs_ref, sem, t, dest_ref[0, t]).start()
        _row_copy(h_ref, xs_ref, sem, t, dest_ref[0, td + t]).start()
        return carry

    lax.fori_loop(0, td, issue, 0)

    def drain(t, carry):
        _row_copy(h_ref, xs_ref, sem, 0, 0).wait()
        _row_copy(h_ref, xs_ref, sem, 0, 0).wait()
        return carry

    lax.fori_loop(0, td, drain, 0)


def _expert_kernel(be_ref, nu_ref, x_ref, w1_ref, w3_ref, w2_ref, y_ref):
    blk = pl.program_id(0)

    @pl.when(blk < nu_ref[0])
    def _():
        x = x_ref[...].astype(BF16)
        a = jnp.dot(x, w1_ref[...], preferred_element_type=F32)
        g = jnp.dot(x, w3_ref[...], preferred_element_type=F32)
        y_ref[...] = _dot(_silu(a) * g, w2_ref[...])

    @pl.when(blk >= nu_ref[0])
    def _():
        y_ref[...] = jnp.zeros_like(y_ref)


def _combine_kernel(dest_ref, x1_ref, wgt_ref, yb_ref, o_ref, g0, g1, sem, *, tc):
    def issue(t, carry):
        _row_copy(yb_ref, g0, sem, dest_ref[0, t], t).start()
        _row_copy(yb_ref, g1, sem, dest_ref[0, tc + t], t).start()
        return carry

    lax.fori_loop(0, tc, issue, 0)

    def drain(t, carry):
        _row_copy(yb_ref, g0, sem, 0, 0).wait()
        _row_copy(yb_ref, g1, sem, 0, 0).wait()
        return carry

    lax.fori_loop(0, tc, drain, 0)
    wg = wgt_ref[...]
    o_ref[...] = x1_ref[...] + (g0[...] * wg[:, 0:1] + g1[...] * wg[:, 1:2])


def _hier_moe(x1, h2, logits, w_e1, w_e3, w_e2):
    n = x1.shape[0]
    tr = _pick_tile(n, 512)
    meta, wgt, cnt = pl.pallas_call(
        _route_kernel, grid=(n // tr,),
        in_specs=[pl.BlockSpec((tr, LANES), lambda i: (i, 0))],
        out_specs=[pl.BlockSpec((tr, LANES), lambda i: (i, 0)), pl.BlockSpec((tr, LANES), lambda i: (i, 0)),
                   pl.BlockSpec((1, LANES), lambda i: (0, 0))],
        out_shape=[jax.ShapeDtypeStruct((n, LANES), I32), jax.ShapeDtypeStruct((n, LANES), F32),
                   jax.ShapeDtypeStruct((1, LANES), F32)],
        scratch_shapes=[pltpu.VMEM((1, LANES), F32)],
        compiler_params=_params("arbitrary"), name="moe_route")(logits)

    bm = MOE_BLOCK
    counts = cnt[0, :N_EXPERTS].astype(I32)
    padded = ((counts + bm - 1) // bm) * bm
    pad_end = jnp.cumsum(padded)
    pad_start = pad_end - padded
    nblk = -(-(2 * n) // bm) + N_EXPERTS
    n_used = (pad_end[-1] // bm).astype(I32).reshape(1)
    blk_expert = jnp.minimum(jnp.searchsorted(pad_end, jnp.arange(nblk, dtype=I32) * bm, side='right'),
                             N_EXPERTS - 1).astype(I32)
    dest0 = pad_start[meta[:, 0]] + meta[:, 2]
    dest1 = pad_start[meta[:, 1]] + meta[:, 3]

    td = _pick_tile(n, 256)
    dest = jnp.concatenate([dest0.reshape(n // td, 1, td), dest1.reshape(n // td, 1, td)], axis=2)
    xs = pl.pallas_call(
        functools.partial(_dispatch_kernel, td=td), grid=(n // td,),
        in_specs=[pl.BlockSpec((None, 1, 2 * td), lambda i: (i, 0, 0), memory_space=pltpu.SMEM),
                  pl.BlockSpec((td, D_MODEL), lambda i: (i, 0)),
                  pl.BlockSpec(memory_space=pl.ANY)],
        out_specs=pl.BlockSpec(memory_space=pl.ANY),
        out_shape=jax.ShapeDtypeStruct((nblk * bm, D_MODEL), F32),
        scratch_shapes=[pltpu.SemaphoreType.DMA(())],
        input_output_aliases={2: 0},
        compiler_params=_params("arbitrary"), name="moe_dispatch")(dest, h2, jnp.zeros((nblk * bm, D_MODEL), F32))

    grid_spec = pltpu.PrefetchScalarGridSpec(
        num_scalar_prefetch=2, grid=(nblk,),
        in_specs=[pl.BlockSpec((bm, D_MODEL), lambda i, be, nu: (i, 0)),
                  pl.BlockSpec((None, D_MODEL, D_EXPERT), lambda i, be, nu: (be[i], 0, 0)),
                  pl.BlockSpec((None, D_MODEL, D_EXPERT), lambda i, be, nu: (be[i], 0, 0)),
                  pl.BlockSpec((None, D_EXPERT, D_MODEL), lambda i, be, nu: (be[i], 0, 0))],
        out_specs=pl.BlockSpec((bm, D_MODEL), lambda i, be, nu: (i, 0)))
    yb = pl.pallas_call(
        _expert_kernel, grid_spec=grid_spec, out_shape=jax.ShapeDtypeStruct((nblk * bm, D_MODEL), F32),
        compiler_params=_params("arbitrary"), name="moe_experts")(blk_expert, n_used, xs, w_e1, w_e3, w_e2)

    tc = td
    return pl.pallas_call(
        functools.partial(_combine_kernel, tc=tc), grid=(n // tc,),
        in_specs=[pl.BlockSpec((None, 1, 2 * tc), lambda i: (i, 0, 0), memory_space=pltpu.SMEM),
                  pl.BlockSpec((tc, D_MODEL), lambda i: (i, 0)),
                  pl.BlockSpec((tc, LANES), lambda i: (i, 0)),
                  pl.BlockSpec(memory_space=pl.ANY)],
        out_specs=pl.BlockSpec((tc, D_MODEL), lambda i: (i, 0)),
        out_shape=jax.ShapeDtypeStruct((n, D_MODEL), F32),
        scratch_shapes=[pltpu.VMEM((tc, D_MODEL), F32), pltpu.VMEM((tc, D_MODEL), F32), pltpu.SemaphoreType.DMA(())],
        compiler_params=_params("arbitrary"), name="moe_combine")(dest, x1, wgt, yb)


def _prep_weights(l, g_mix, w_in, b_f, g_qnorm, g_knorm, w_conv, a_log, dt_bias, g_onorm, b_gate,
                  w_br_a, w_br_b, w_out, g_ffn, w_grp, b_grp, w_exp, b_exp, w_e1, w_e3, w_e2):
    w = w_in[l]
    offs = [0]
    for s in (WA, WA, WA, H_A, QK_B, QK_B, V_B, H_B, H_B, V_B, D_MODEL, D_MODEL):
        offs.append(offs[-1] + s)
    col = lambda i, j=None: w[:, offs[i]:offs[(i if j is None else j) + 1]]
    n_small = H_A + 2 * H_B
    w_small = jnp.concatenate([col(3), col(7), col(8)], axis=1)
    zeros8 = jnp.zeros((H_A,), F32)
    bias = jnp.concatenate([b_f[l], zeros8, dt_bias[l]])
    alog = jnp.concatenate([zeros8, zeros8, a_log[l]])
    pad = lambda v, n: jnp.pad(v, (0, n - v.shape[0]))
    rows_t = 32
    return dict(
        g_mix=g_mix[l][None, :],
        wq=col(0).astype(BF16), wk=col(1).astype(BF16), wv=col(2).astype(BF16),
        gq=jnp.tile(g_qnorm[l], H_A)[None, :], gk=jnp.tile(g_knorm[l], H_A)[None, :],
        ws=jnp.pad(w_small, ((0, 0), (0, LANES - n_small))).astype(BF16),
        wst=jnp.pad(w_small.T, ((0, rows_t - n_small), (0, 0))).astype(BF16),
        wu=col(4, 6).astype(BF16),
        bias=pad(bias, LANES)[None, :], alog=pad(alog, LANES)[None, :],
        biast=pad(bias, rows_t)[:, None], alogt=pad(alog, rows_t)[:, None],
        wz=col(9).astype(BF16), wg=col(10, 11).astype(BF16), bg=b_gate[l][None, :],
        w_conv=w_conv[l], g_onorm=g_onorm[l][None, :],
        w_br_a=w_br_a[l].astype(BF16), w_br_b=w_br_b[l].astype(BF16), w_out=w_out[l].astype(BF16),
        g_ffn=g_ffn[l][None, :],
        wr=jnp.pad(jnp.concatenate([w_grp[l], w_exp[l]], axis=1), ((0, 0), (0, LANES - N_GROUPS - N_EXPERTS))).astype(BF16),
        br=pad(jnp.concatenate([b_grp[l], b_exp[l]]), LANES)[None, :],
        w_e1=w_e1[l].astype(BF16), w_e3=w_e3[l].astype(BF16), w_e2=w_e2[l].astype(BF16),
    )


def _layer(x, past_k, past_v, past_logf, s0, conv0, p):
    b, t, _ = x.shape
    past = past_k.shape[1]
    n = b * t
    xf = x.reshape(n, D_MODEL)
    tm = _pick_tile(n, 512)

    q16, k32, k16, v32, v16 = _row_call(
        _inproj_fox_kernel, n, tm, [xf], [p["g_mix"], p["wq"], p["wk"], p["wv"], p["gq"], p["gk"]],
        [((n, WA), BF16, False), ((n, WA), F32, False), ((n, WA), BF16, False), ((n, WA), F32, False),
         ((n, WA), BF16, False)], "inproj_fox")
    small, smallt, u = _row_call(
        _inproj_gdn_kernel, n, tm, [xf],
        [p["g_mix"], p["ws"], p["wst"], p["wu"], p["bias"], p["alog"], p["biast"], p["alogt"]],
        [((n, LANES), F32, False), ((32, n), F32, True), ((n, CONV_DIM), F32, False)], "inproj_gdn")
    z, gates = _row_call(
        _inproj_gate_kernel, n, tm, [xf], [p["g_mix"], p["wz"], p["wg"], p["bg"]],
        [((n, V_B), F32, False), ((n, 2 * D_MODEL), F32, False)], "inproj_gate")

    logf = small[:, :H_A].reshape(b, t, H_A)
    c_all = _cumsum_time(jnp.concatenate([past_logf.astype(F32), logf], axis=1))
    k_all = jnp.concatenate([past_k.reshape(b, past, WA).astype(BF16), k16.reshape(b, t, WA)], axis=1)
    v_all = jnp.concatenate([past_v.reshape(b, past, WA).astype(BF16), v16.reshape(b, t, WA)], axis=1)
    y_a = _fox_attention(q16, k_all.reshape(b * (past + t), WA), v_all.reshape(b * (past + t), WA),
                         c_all[:, past:].reshape(n, H_A), c_all.transpose(0, 2, 1), b, t, past)

    y_b, s_new, conv_new = _gdn(u, small, smallt, z, s0.reshape(b, H_B * DK_B, DV_B), conv0,
                                p["w_conv"], p["g_onorm"], b, t)

    x1, h2, logits = _row_call(
        _merge_kernel, n, tm, [xf, y_a, y_b, gates],
        [p["w_br_a"], p["w_br_b"], p["w_out"], p["g_ffn"], p["wr"], p["br"]],
        [((n, D_MODEL), F32, False), ((n, D_MODEL), F32, False), ((n, LANES), F32, False)], "merge_outproj")
    x2 = _hier_moe(x1, h2, logits, p["w_e1"], p["w_e3"], p["w_e2"])
    return (x2.reshape(b, t, D_MODEL), k32.reshape(b, t, H_A, DH_A), v32.reshape(b, t, H_A, DH_A), logf,
            s_new.reshape(b, H_B, DK_B, DV_B), conv_new)


def kernel(x_prompt, x_sample, cache_fox_k, cache_fox_v, cache_fox_logf, state_gdn, state_gdn_conv, g_mix, w_in, b_f, g_qnorm, g_knorm, w_conv, a_log, dt_bias, g_onorm, b_gate, w_br_a, w_br_b, w_out, g_ffn, w_grp, b_grp, w_exp, b_exp, w_e1, w_e3, w_e2):
    depth = w_in.shape[0]
    bp = x_prompt.shape[0]
    yp, ys = x_prompt, x_sample
    outs_p, outs_s = [], []
    for l in range(depth):
        p = _prep_weights(l, g_mix, w_in, b_f, g_qnorm, g_knorm, w_conv, a_log, dt_bias, g_onorm, b_gate,
                          w_br_a, w_br_b, w_out, g_ffn, w_grp, b_grp, w_exp, b_exp, w_e1, w_e3, w_e2)
        yp, *rest_p = _layer(
            yp, jnp.zeros((bp, 0, H_A, DH_A), F32), jnp.zeros((bp, 0, H_A, DH_A), F32), jnp.zeros((bp, 0, H_A), F32),
            jnp.zeros((bp, H_B, DK_B, DV_B), F32), jnp.zeros((bp, CONV_W - 1, CONV_DIM), F32), p)
        ys, *rest_s = _layer(ys, cache_fox_k[l], cache_fox_v[l], cache_fox_logf[l], state_gdn[l], state_gdn_conv[l], p)
        outs_p.append(rest_p)
        outs_s.append(rest_s)
    stack = lambda outs, i: jnp.stack([o[i] for o in outs])
    return (yp, ys,
            stack(outs_p, 0), stack(outs_p, 1), stack(outs_p, 2), stack(outs_p, 3), stack(outs_p, 4),
            stack(outs_s, 0), stack(outs_s, 1), stack(outs_s, 2), stack(outs_s, 3), stack(outs_s, 4))
```

```python
import functools
import math

import jax
import jax.numpy as jnp
from jax import lax
from jax.experimental import pallas as pl
from jax.experimental.pallas import tpu as pltpu

F32, BF16, I32, U32 = jnp.float32, jnp.bfloat16, jnp.int32, jnp.uint32

D_MODEL = 1024
H_A, DH_A = 8, 64
WA = H_A * DH_A
H_B, DK_B, DV_B = 8, 64, 128
QK_B, V_B = H_B * DK_B, H_B * DV_B
CONV_W = 4
CONV_DIM = 2 * QK_B + V_B
N_GROUPS, EXPERTS_PER_GROUP = 4, 8
N_EXPERTS = N_GROUPS * EXPERTS_PER_GROUP
D_EXPERT = 512
GDN_CHUNK = 64
GDN_CHUNKS_PER_STEP = 4
EPS = 1e-6
NEG = -1e30
LANES = 128
V7X_VMEM_LIMIT = 56 * 1024 * 1024
MOE_BLOCK = 256


def _params(*sem):
    return pltpu.CompilerParams(dimension_semantics=sem, vmem_limit_bytes=V7X_VMEM_LIMIT)


def _pick_tile(n, cap, mult=8):
    if n <= cap:
        return n
    for t in range(cap - cap % mult, 0, -mult):
        if n % t == 0:
            return t
    return n


def _dot(a, b):
    return jnp.dot(a.astype(BF16), b.astype(BF16), preferred_element_type=F32)


def _dot_t(a, b):
    return lax.dot_general(a.astype(BF16), b.astype(BF16), (((1,), (1,)), ((), ())), preferred_element_type=F32)


def _dot_tl(a, b):
    return lax.dot_general(a.astype(BF16), b.astype(BF16), (((0,), (0,)), ((), ())), preferred_element_type=F32)


def _split(x, n):
    parts = []
    for _ in range(n):
        p = x.astype(BF16)
        parts.append(p)
        x = x - p.astype(F32)
    return parts


def _dot_exact_lhs(ones, x, n):
    acc = None
    for p in _split(x, n):
        t = jnp.dot(ones, p, preferred_element_type=F32)
        acc = t if acc is None else acc + t
    return acc


def _dot_exact_rhs(x, ones, n):
    acc = None
    for p in _split(x, n):
        t = jnp.dot(p, ones, preferred_element_type=F32)
        acc = t if acc is None else acc + t
    return acc


def _iota(shape, dim):
    return lax.broadcasted_iota(I32, shape, dim)


def _softplus(x):
    return jnp.maximum(x, 0.0) + jnp.log1p(jnp.exp(-jnp.abs(x)))


def _silu(x):
    return x * jax.nn.sigmoid(x)


def _rms_rows(x, g):
    return x * lax.rsqrt(jnp.mean(x * x, axis=-1, keepdims=True) + EPS) * g


def _head_ones(width, head):
    shift = head.bit_length() - 1
    assert head == 1 << shift
    r = jnp.right_shift(_iota((width, width), 0), shift)
    c = jnp.right_shift(_iota((width, width), 1), shift)
    return (r == c).astype(BF16)


def _group_sumsq(y, head):
    ones = _head_ones(2 * LANES, head)
    sq = y * y
    parts = [_dot_exact_rhs(sq[:, c:c + 2 * LANES], ones, 2) for c in range(0, y.shape[1], 2 * LANES)]
    return parts[0] if len(parts) == 1 else jnp.concatenate(parts, axis=1)


def _inproj_fox_kernel(x_ref, g_ref, wq_ref, wk_ref, wv_ref, gq_ref, gk_ref,
                       q_ref, k32_ref, k16_ref, v32_ref, v16_ref):
    h = _rms_rows(x_ref[...], g_ref[...]).astype(BF16)

    def headnorm(y, gain):
        ms = _group_sumsq(y, DH_A) * (1.0 / DH_A)
        return y * lax.rsqrt(ms + EPS) * gain

    q = headnorm(jnp.dot(h, wq_ref[...], preferred_element_type=F32), gq_ref[...])
    q_ref[...] = (q * (DH_A ** -0.5)).astype(BF16)
    k = headnorm(jnp.dot(h, wk_ref[...], preferred_element_type=F32), gk_ref[...])
    k32_ref[...] = k
    k16_ref[...] = k.astype(BF16)
    v = jnp.dot(h, wv_ref[...], preferred_element_type=F32)
    v32_ref[...] = v
    v16_ref[...] = v.astype(BF16)


def _small_epilogue(y, bias, a_log, idx):
    yb = y + bias
    logf = -_softplus(-yb)
    beta = jax.nn.sigmoid(y)
    g = -jnp.exp(a_log) * _softplus(yb)
    return jnp.where(idx < H_A, logf, jnp.where(idx < H_A + H_B, beta, g))


def _inproj_gdn_kernel(x_ref, g_ref, ws_ref, wst_ref, wu_ref, bias_ref, alog_ref, biast_ref, alogt_ref,
                       small_ref, smallt_ref, u_ref):
    h = _rms_rows(x_ref[...], g_ref[...]).astype(BF16)
    ys = jnp.dot(h, ws_ref[...], preferred_element_type=F32)
    small_ref[...] = _small_epilogue(ys, bias_ref[...], alog_ref[...], _iota(ys.shape, 1))
    yt = lax.dot_general(wst_ref[...], h, (((1,), (1,)), ((), ())), preferred_element_type=F32)
    smallt_ref[...] = _small_epilogue(yt, biast_ref[...], alogt_ref[...], _iota(yt.shape, 0))
    for c in range(0, CONV_DIM, 512):
        u_ref[:, c:c + 512] = jnp.dot(h, wu_ref[:, c:c + 512], preferred_element_type=F32)


def _inproj_gate_kernel(x_ref, g_ref, wz_ref, wg_ref, bg_ref, z_ref, gate_ref):
    h = _rms_rows(x_ref[...], g_ref[...]).astype(BF16)
    for c in range(0, V_B, 512):
        z_ref[:, c:c + 512] = jnp.dot(h, wz_ref[:, c:c + 512], preferred_element_type=F32)
    for c in range(0, 2 * D_MODEL, 512):
        y = jnp.dot(h, wg_ref[:, c:c + 512], preferred_element_type=F32)
        gate_ref[:, c:c + 512] = jax.nn.sigmoid(y + bg_ref[:, c:c + 512])


def _row_call(kernel, n, tm, row_inputs, full_inputs, outs, name):
    in_specs = [pl.BlockSpec((tm, a.shape[1]), lambda i: (i, 0)) for a in row_inputs]
    in_specs += [pl.BlockSpec(a.shape, lambda i, nd=a.ndim: (0,) * nd) for a in full_inputs]
    out_specs, out_shapes = [], []
    for shape, dtype, transposed in outs:
        if transposed:
            out_specs.append(pl.BlockSpec((shape[0], tm), lambda i: (0, i)))
        else:
            out_specs.append(pl.BlockSpec((tm, shape[1]), lambda i: (i, 0)))
        out_shapes.append(jax.ShapeDtypeStruct(shape, dtype))
    return pl.pallas_call(kernel, grid=(n // tm,), in_specs=in_specs, out_specs=out_specs, out_shape=out_shapes,
                          compiler_params=_params("arbitrary"), name=name)(*row_inputs, *full_inputs)


def _cumsum_kernel(x_ref, o_ref, carry):
    @pl.when(pl.program_id(1) == 0)
    def _():
        carry[...] = jnp.zeros_like(carry)

    x = x_ref[...]
    t = x.shape[0]
    tril = (_iota((t, t), 0) >= _iota((t, t), 1)).astype(BF16)
    c = _dot_exact_lhs(tril, x, 3) + carry[...]
    o_ref[...] = c
    carry[...] = c[t - 1:t, :]


def _cumsum_time(x):
    b, tt, c = x.shape
    tc = _pick_tile(tt, 512)
    return pl.pallas_call(
        _cumsum_kernel, grid=(b, tt // tc),
        in_specs=[pl.BlockSpec((None, tc, c), lambda i, j: (i, j, 0))],
        out_specs=pl.BlockSpec((None, tc, c), lambda i, j: (i, j, 0)),
        out_shape=jax.ShapeDtypeStruct(x.shape, F32),
        scratch_shapes=[pltpu.VMEM((1, c), F32)],
        compiler_params=_params("arbitrary", "arbitrary"), name="fox_cumsum")(x)


def _attn_kernel(qi_ref, kj_ref, last_ref, q_ref, k_ref, v_ref, cq_ref, ck_ref, o_ref, m_sc, l_sc, acc_sc, *, tq, tk, past):
    n = pl.program_id(1)
    i, j = qi_ref[n], kj_ref[n]

    @pl.when(j == 0)
    def _():
        m_sc[...] = jnp.full_like(m_sc, NEG)
        l_sc[...] = jnp.zeros_like(l_sc)
        acc_sc[...] = jnp.zeros_like(acc_sc)

    qpos = past + i * tq + _iota((tq, 1), 0)
    kpos = j * tk + _iota((1, tk), 1)
    visible = kpos <= qpos
    lo = _iota((1, LANES), 1) < DH_A
    cq = cq_ref[...]
    ck = ck_ref[...]
    halves = ((0, lo), (1, jnp.logical_not(lo)))
    scores = []
    for p in range(H_A // 2):
        sl = slice(p * LANES, (p + 1) * LANES)
        q2, k2 = q_ref[:, sl], k_ref[:, sl]
        for half, mask in halves:
            scores.append(_dot_t(jnp.where(mask, q2, jnp.zeros_like(q2)), k2))
    for p in range(H_A // 2):
        sl = slice(p * LANES, (p + 1) * LANES)
        v2 = v_ref[:, sl]
        alphas, pvs = [], []
        for half, mask in halves:
            hd = 2 * p + half
            s = scores[hd] + cq[:, hd:hd + 1] - ck[hd:hd + 1, :]
            s = jnp.where(visible, s, NEG)
            m_prev = m_sc[hd]
            m_new = jnp.maximum(m_prev, jnp.max(s, axis=-1, keepdims=True))
            alpha = jnp.exp(m_prev - m_new)
            pr = jnp.exp(s - m_new)
            l_sc[hd] = alpha * l_sc[hd] + jnp.sum(pr, axis=-1, keepdims=True)
            m_sc[hd] = m_new
            alphas.append(alpha)
            pvs.append(_dot(pr, jnp.where(mask, v2, jnp.zeros_like(v2))))
        acc_sc[p] = jnp.where(lo, alphas[0], alphas[1]) * acc_sc[p] + pvs[0] + pvs[1]

    @pl.when(last_ref[n] == 1)
    def _():
        for p in range(H_A // 2):
            l2 = jnp.where(lo, l_sc[2 * p], l_sc[2 * p + 1])
            o_ref[:, p * LANES:(p + 1) * LANES] = (acc_sc[p] / l2).astype(BF16)


def _fox_attention(q, k_all, v_all, c_q, c_kt, b, t, past):
    tt = past + t
    tq = _pick_tile(t, 512, 16)
    tk = _pick_tile(tt, 512, LANES)
    nq, nk = t // tq, tt // tk
    pairs = [(i, j) for i in range(nq) for j in range(nk) if j * tk <= past + (i + 1) * tq - 1]
    qi = jnp.array([p[0] for p in pairs], I32)
    kj = jnp.array([p[1] for p in pairs], I32)
    last = jnp.array([1 if (idx + 1 == len(pairs) or pairs[idx + 1][0] != p[0]) else 0
                      for idx, p in enumerate(pairs)], I32)
    grid_spec = pltpu.PrefetchScalarGridSpec(
        num_scalar_prefetch=3, grid=(b, len(pairs)),
        in_specs=[
            pl.BlockSpec((tq, WA), lambda bi, n, qi, kj, la: (bi * nq + qi[n], 0)),
            pl.BlockSpec((tk, WA), lambda bi, n, qi, kj, la: (bi * nk + kj[n], 0)),
            pl.BlockSpec((tk, WA), lambda bi, n, qi, kj, la: (bi * nk + kj[n], 0)),
            pl.BlockSpec((tq, H_A), lambda bi, n, qi, kj, la: (bi * nq + qi[n], 0)),
            pl.BlockSpec((None, H_A, tk), lambda bi, n, qi, kj, la: (bi, 0, kj[n])),
        ],
        out_specs=pl.BlockSpec((tq, WA), lambda bi, n, qi, kj, la: (bi * nq + qi[n], 0)),
        scratch_shapes=[pltpu.VMEM((H_A, tq, 1), F32), pltpu.VMEM((H_A, tq, 1), F32),
                        pltpu.VMEM((H_A // 2, tq, LANES), F32)])
    return pl.pallas_call(
        functools.partial(_attn_kernel, tq=tq, tk=tk, past=past), grid_spec=grid_spec,
        out_shape=jax.ShapeDtypeStruct((b * t, WA), BF16),
        compiler_params=_params("arbitrary", "arbitrary"), name="fox_attention")(qi, kj, last, q, k_all, v_all, c_q, c_kt)


def _gdn_kernel(u_ref, small_ref, smallt_ref, z_ref, s0_ref, conv0_ref, wconv_ref, gon_ref,
                yb_ref, sout_ref, cout_ref, s_sc, ubuf, *, chunk, cps, nsteps):
    c = pl.program_id(1)
    L, C = chunk, cps
    R = L * C
    lg = L.bit_length() - 1
    keep = CONV_W - 1

    @pl.when(c == 0)
    def _():
        s_sc[...] = s0_ref[...]
        ubuf[8 - keep:8, :] = conv0_ref[...]

    ubuf[8:8 + R, :] = u_ref[...]
    w = wconv_ref[...]
    conv = ubuf[8 - keep:8 - keep + R, :] * w[0:1, :]
    for i in range(1, CONV_W):
        conv = conv + ubuf[8 - keep + i:8 - keep + i + R, :] * w[i:i + 1, :]
    act = _silu(conv)
    tail = ubuf[8 + R - keep:8 + R, :]
    ubuf[8 - keep:8, :] = tail

    @pl.when(c == nsteps - 1)
    def _():
        cout_ref[...] = tail

    sm = small_ref[...]
    smt = smallt_ref[...]
    rr, rc = _iota((R, R), 0), _iota((R, R), 1)
    same = jnp.right_shift(rr, lg) == jnp.right_shift(rc, lg)
    g_col = _dot_exact_lhs(jnp.logical_and(same, rr >= rc).astype(BF16), sm, 3)
    qn_all = act[:, :QK_B]
    qn_all = qn_all * lax.rsqrt(_group_sumsq(qn_all, DK_B) + EPS) * (DK_B ** -0.5)
    kn_all = act[:, QK_B:2 * QK_B]
    kn_all = kn_all * lax.rsqrt(_group_sumsq(kn_all, DK_B) + EPS)

    W2 = 2 * L
    lo64 = _iota((1, LANES), 1) < DK_B
    hi64 = jnp.logical_not(lo64)
    lane2 = _iota((1, W2), 1)
    lo_l, hi_l = lane2 < L, lane2 >= L
    ri, ci = _iota((L, W2), 0), jnp.bitwise_and(_iota((L, W2), 1), L - 1)
    incl, strict = ri >= ci, ri > ci
    eye = (ri == ci).astype(F32)
    n_sq = max(int(math.ceil(math.log2(L))) - 1, 0)
    sub128 = _iota((LANES, 1), 0) < DK_B

    def stack_heads(x, lo, hi):
        return jnp.concatenate([jnp.where(lo, x, 0.0), jnp.where(hi, x, 0.0)], axis=0)

    items = [(ch, p) for ch in range(C) for p in range(H_B // 2)]
    g_rows = []
    for ch in range(C):
        tt, jj = _iota((R, W2), 0), jnp.bitwise_and(_iota((R, W2), 1), L - 1)
        sel = jnp.logical_and(jnp.right_shift(tt, lg) == ch, jnp.bitwise_and(tt, L - 1) <= jj)
        g_rows.append(_dot_exact_rhs(smt, sel.astype(BF16), 3))

    st = {}
    for ch, p in items:
        rs = slice(ch * L, (ch + 1) * L)
        ha, hb = 2 * H_B + 2 * p, 2 * H_B + 2 * p + 1
        gca, gcb = g_col[rs, ha:ha + 1], g_col[rs, hb:hb + 1]
        gla, glb = gca[L - 1:L, :], gcb[L - 1:L, :]
        q2, k2 = qn_all[rs, p * LANES:(p + 1) * LANES], kn_all[rs, p * LANES:(p + 1) * LANES]
        beta2 = jnp.where(lo64, sm[rs, H_B + 2 * p:H_B + 2 * p + 1], sm[rs, H_B + 2 * p + 1:H_B + 2 * p + 2])
        eg2 = jnp.exp(jnp.where(lo64, gca, gcb))
        egl2 = jnp.exp(jnp.where(lo64, gla - gca, glb - gcb))
        kb2 = k2 * beta2
        gr = jnp.where(lo_l, g_rows[ch][ha:ha + 1, :], g_rows[ch][hb:hb + 1, :])
        decay = jnp.exp(jnp.where(incl, jnp.where(lo_l, gca, gcb) - gr, NEG))
        ybd = stack_heads(k2, lo64, hi64)
        va = act[rs, 2 * QK_B + 2 * p * DV_B:2 * QK_B + (2 * p + 1) * DV_B] * sm[rs, H_B + 2 * p:H_B + 2 * p + 1]
        vb = act[rs, 2 * QK_B + (2 * p + 1) * DV_B:2 * QK_B + (2 * p + 2) * DV_B] * sm[rs, H_B + 2 * p + 1:H_B + 2 * p + 2]
        st[ch, p] = dict(
            decay=decay, ybd=ybd, kb2=kb2, q2=q2,
            kbg_bd=stack_heads(kb2 * eg2, lo64, hi64), qd_bd=stack_heads(q2 * eg2, lo64, hi64),
            kd_bd=stack_heads(k2 * egl2, lo64, hi64), vst=jnp.concatenate([va, vb], axis=0),
            gl2=jnp.where(sub128, jnp.exp(gla), jnp.exp(glb)))
    for it in items:
        d = st[it]
        d["m"] = jnp.where(strict, _dot_t(d["kb2"], d["ybd"]) * d["decay"], 0.0)
        d["attn"] = jnp.where(incl, _dot_t(d["q2"], d["ybd"]) * d["decay"], 0.0)
    for it in items:
        d = st[it]
        d["t"] = eye - d["m"]
        d["mp"] = d["m"]
    for lvl in range(n_sq):
        for it in items:
            d = st[it]
            d["mp"] = _dot(d["mp"], stack_heads(d["mp"], lo_l, hi_l))
        for it in items:
            d = st[it]
            d["t"] = d["t"] + _dot(d["t"], stack_heads(d["mp"], lo_l, hi_l))
    for it in items:
        d = st[it]
        d["ust"] = _dot(stack_heads(d["t"], lo_l, hi_l), d["vst"])
        d["w2"] = _dot(d["t"], d["kbg_bd"])
    for ch in range(C):
        rs = slice(ch * L, (ch + 1) * L)
        s2s, ass = [], []
        for p in range(H_B // 2):
            d = st[ch, p]
            s2 = s_sc[p * LANES:(p + 1) * LANES, :]
            s2s.append(s2)
            ass.append(_dot(jnp.concatenate([stack_heads(d["w2"], lo64, hi64), d["qd_bd"]], axis=0), s2))
        for p in range(H_B // 2):
            d = st[ch, p]
            v_new = d["ust"] - ass[p][:W2, :]
            o_st = ass[p][W2:, :] + _dot(stack_heads(d["attn"], lo_l, hi_l), v_new)
            s_sc[p * LANES:(p + 1) * LANES, :] = s2s[p] * d["gl2"] + _dot_tl(d["kd_bd"], v_new)
            for half in range(2):
                hd = 2 * p + half
                o = o_st[half * L:(half + 1) * L, :]
                on = o * lax.rsqrt(jnp.mean(o * o, axis=-1, keepdims=True) + EPS) * gon_ref[...]
                zs = z_ref[rs, hd * DV_B:(hd + 1) * DV_B]
                yb_ref[rs, hd * DV_B:(hd + 1) * DV_B] = (on * _silu(zs)).astype(BF16)

    @pl.when(c == nsteps - 1)
    def _():
        sout_ref[...] = s_sc[...]


def _gdn(u, small, smallt, z, s0, conv0, w_conv, g_onorm, b, t):
    clen = GDN_CHUNK if t % GDN_CHUNK == 0 else t
    assert clen & (clen - 1) == 0, clen
    cps = GDN_CHUNKS_PER_STEP if (t // clen) % GDN_CHUNKS_PER_STEP == 0 else 1
    chunk = clen * cps
    nc = t // chunk
    smallt3 = smallt.reshape(smallt.shape[0], b * nc, chunk).transpose(1, 0, 2)
    keep = CONV_W - 1
    return pl.pallas_call(
        functools.partial(_gdn_kernel, chunk=clen, cps=cps, nsteps=nc), grid=(b, nc),
        in_specs=[
            pl.BlockSpec((chunk, CONV_DIM), lambda i, j: (i * nc + j, 0)),
            pl.BlockSpec((chunk, LANES), lambda i, j: (i * nc + j, 0)),
            pl.BlockSpec((None, smallt.shape[0], chunk), lambda i, j: (i * nc + j, 0, 0)),
            pl.BlockSpec((chunk, V_B), lambda i, j: (i * nc + j, 0)),
            pl.BlockSpec((None, H_B * DK_B, DV_B), lambda i, j: (i, 0, 0)),
            pl.BlockSpec((None, keep, CONV_DIM), lambda i, j: (i, 0, 0)),
            pl.BlockSpec((CONV_W, CONV_DIM), lambda i, j: (0, 0)),
            pl.BlockSpec((1, DV_B), lambda i, j: (0, 0)),
        ],
        out_specs=[
            pl.BlockSpec((chunk, V_B), lambda i, j: (i * nc + j, 0)),
            pl.BlockSpec((None, H_B * DK_B, DV_B), lambda i, j: (i, 0, 0)),
            pl.BlockSpec((None, keep, CONV_DIM), lambda i, j: (i, 0, 0)),
        ],
        out_shape=[jax.ShapeDtypeStruct((b * t, V_B), BF16),
                   jax.ShapeDtypeStruct((b, H_B * DK_B, DV_B), F32),
                   jax.ShapeDtypeStruct((b, keep, CONV_DIM), F32)],
        scratch_shapes=[pltpu.VMEM((H_B * DK_B, DV_B), F32), pltpu.VMEM((8 + chunk, CONV_DIM), F32)],
        compiler_params=_params("arbitrary", "arbitrary"), name="gdn_chunked",
    )(u, small, smallt3, z, s0, conv0, w_conv, g_onorm)


def _merge_kernel(x_ref, ya_ref, yb_ref, gate_ref, wa_ref, wb_ref, wo_ref, gf_ref, wr_ref, br_ref,
                  x1_ref, h2_ref, logit_ref):
    br_a = jnp.dot(ya_ref[...], wa_ref[...], preferred_element_type=F32)
    br_b = jnp.dot(yb_ref[...], wb_ref[...], preferred_element_type=F32)
    merged = gate_ref[:, :D_MODEL] * br_a + gate_ref[:, D_MODEL:] * br_b
    x1 = x_ref[...] + _dot(merged, wo_ref[...])
    x1_ref[...] = x1
    h2 = _rms_rows(x1, gf_ref[...])
    h2_ref[...] = h2
    logit_ref[...] = _dot(h2, wr_ref[...]) + br_ref[...]


def _route_kernel(logit_ref, meta_ref, wgt_ref, cnt_ref, run, start, *, block):
    sweep, step = pl.program_id(0), pl.program_id(1)

    @pl.when(jnp.logical_and(sweep == 0, step == 0))
    def _():
        run[...] = jnp.zeros_like(run)
        start[...] = jnp.zeros_like(start)

    @pl.when(jnp.logical_and(sweep == 1, step == 0))
    def _():
        counts = run[...]
        cnt_ref[...] = counts
        padded = jnp.floor((counts + (block - 1)) * (1.0 / block)) * block
        before_e = (_iota((LANES, LANES), 0) < _iota((LANES, LANES), 1)).astype(BF16)
        start[...] = _dot_exact_rhs(jnp.broadcast_to(padded, (8, LANES)), before_e, 3)[0:1, :]
        run[...] = jnp.zeros_like(run)

    x = logit_ref[...]
    tm = x.shape[0]
    lane = _iota((1, LANES), 1)
    lanef = lane.astype(F32)
    big = float(LANES)

    def first_argmax(vals, valid):
        mx = jnp.max(vals, axis=-1, keepdims=True)
        idx = jnp.min(jnp.where(jnp.logical_and(vals == mx, valid), lanef, big), axis=-1, keepdims=True)
        return mx, idx.astype(I32)

    gmask = lane < N_GROUPS
    gmax, gsel = first_argmax(jnp.where(gmask, x, NEG), gmask)
    p_g = 1.0 / jnp.sum(jnp.where(gmask, jnp.exp(x - gmax), 0.0), axis=-1, keepdims=True)
    first = N_GROUPS + EXPERTS_PER_GROUP * gsel
    emask = jnp.logical_and(lane >= first, lane < first + EXPERTS_PER_GROUP)
    ev = jnp.where(emask, x, NEG)
    m1, i1 = first_argmax(ev, emask)
    emask2 = jnp.logical_and(emask, lane != i1)
    m2, i2 = first_argmax(jnp.where(emask2, x, NEG), emask2)
    e21 = jnp.exp(m2 - m1)
    w1 = p_g / (1.0 + e21)
    w2 = p_g * e21 / (1.0 + e21)
    e1, e2 = i1 - N_GROUPS, i2 - N_GROUPS
    oh1, oh2 = lane == e1, lane == e2
    onehot = jnp.logical_or(oh1, oh2).astype(BF16)
    strict = (_iota((tm, tm), 0) > _iota((tm, tm), 1)).astype(BF16)
    slot = jnp.dot(strict, onehot, preferred_element_type=F32) + run[...] + start[...]
    run[...] = run[...] + jnp.sum(onehot.astype(F32), axis=0, keepdims=True)

    @pl.when(sweep == 1)
    def _():
        d1 = jnp.sum(jnp.where(oh1, slot, 0.0), axis=-1, keepdims=True).astype(I32)
        d2 = jnp.sum(jnp.where(oh2, slot, 0.0), axis=-1, keepdims=True).astype(I32)
        l8 = _iota((1, 8), 1)
        meta_ref[...] = jnp.where(l8 == 0, e1, jnp.where(l8 == 1, e2, jnp.where(l8 == 2, d1, jnp.where(l8 == 3, d2, 0))))
        wgt_ref[...] = jnp.where(l8 == 0, w1, jnp.where(l8 == 1, w2, 0.0))


def _row_copy(src, dst, sem, s, d):
    return pltpu.make_async_copy(src.at[pl.ds(s, 1)], dst.at[pl.ds(d, 1)], sem)


def _dispatch_kernel(dest_ref, h_ref, zero_ref, xs_ref, sem, *, td):
    del zero_ref

    def issue(t, carry):
        _row_copy(h_ref, xs_ref, sem, t, dest_ref[0, t]).start()
        _row_copy(h_ref, xs_ref, sem, t, dest_ref[0, td + t]).start()
        return carry

    lax.fori_loop(0, td, issue, 0)

    def drain(t, carry):
        _row_copy(h_ref, xs_ref, sem, 0, 0).wait()
        _row_copy(h_ref, xs_ref, sem, 0, 0).wait()
        return carry

    lax.fori_loop(0, td, drain, 0)


def _expert_kernel(be_ref, nu_ref, x_ref, w1_ref, w3_ref, w2_ref, y_ref):
    blk = pl.program_id(0)

    @pl.when(blk < nu_ref[0])
    def _():
        x = x_ref[...].astype(BF16)
        a = jnp.dot(x, w1_ref[...], preferred_element_type=F32)
        g = jnp.dot(x, w3_ref[...], preferred_element_type=F32)
        y_ref[...] = _dot(_silu(a) * g, w2_ref[...])

    @pl.when(blk >= nu_ref[0])
    def _():
        y_ref[...] = jnp.zeros_like(y_ref)


def _combine_kernel(dest_ref, x1_ref, wgt_ref, yb_ref, o_ref, g0, g1, sem, *, tc):
    def issue(t, carry):
        _row_copy(yb_ref, g0, sem, dest_ref[0, t], t).start()
        _row_copy(yb_ref, g1, sem, dest_ref[0, tc + t], t).start()
        return carry

    lax.fori_loop(0, tc, issue, 0)

    def drain(t, carry):
        _row_copy(yb_ref, g0, sem, 0, 0).wait()
        _row_copy(yb_ref, g1, sem, 0, 0).wait()
        return carry

    lax.fori_loop(0, tc, drain, 0)
    wg = wgt_ref[...]
    o_ref[...] = x1_ref[...] + (g0[...] * wg[:, 0:1] + g1[...] * wg[:, 1:2])


def _hier_moe(x1, h2, logits, w_e1, w_e3, w_e2):
    n = x1.shape[0]
    tr = _pick_tile(n, 512)
    bm = MOE_BLOCK
    assert bm & (bm - 1) == 0
    meta, wgt, cnt = pl.pallas_call(
        functools.partial(_route_kernel, block=bm), grid=(2, n // tr),
        in_specs=[pl.BlockSpec((tr, LANES), lambda s, i: (i, 0))],
        out_specs=[pl.BlockSpec((tr, 8), lambda s, i: (i * s, 0)), pl.BlockSpec((tr, 8), lambda s, i: (i * s, 0)),
                   pl.BlockSpec((1, LANES), lambda s, i: (0, 0))],
        out_shape=[jax.ShapeDtypeStruct((n, 8), I32), jax.ShapeDtypeStruct((n, 8), F32),
                   jax.ShapeDtypeStruct((1, LANES), F32)],
        scratch_shapes=[pltpu.VMEM((1, LANES), F32), pltpu.VMEM((1, LANES), F32)],
        compiler_params=_params("arbitrary", "arbitrary"), name="moe_route")(logits)

    counts = cnt[0, :N_EXPERTS].astype(I32)
    pad_end = jnp.cumsum(((counts + bm - 1) // bm) * bm)
    nblk = -(-(2 * n) // bm) + N_EXPERTS
    n_used = (pad_end[-1] // bm).astype(I32).reshape(1)
    blk_first = jnp.arange(nblk, dtype=I32) * bm
    blk_expert = jnp.minimum(jnp.sum((pad_end[None, :] <= blk_first[:, None]).astype(I32), axis=1), N_EXPERTS - 1)

    td = _pick_tile(n, 256)
    dest = meta[:, 2:4].reshape(n // td, td, 2).transpose(0, 2, 1).reshape(n // td, 1, 2 * td)
    xs = pl.pallas_call(
        functools.partial(_dispatch_kernel, td=td), grid=(n // td,),
        in_specs=[pl.BlockSpec((None, 1, 2 * td), lambda i: (i, 0, 0), memory_space=pltpu.SMEM),
                  pl.BlockSpec((td, D_MODEL), lambda i: (i, 0)),
                  pl.BlockSpec(memory_space=pl.ANY)],
        out_specs=pl.BlockSpec(memory_space=pl.ANY),
        out_shape=jax.ShapeDtypeStruct((nblk * bm, D_MODEL), F32),
        scratch_shapes=[pltpu.SemaphoreType.DMA(())],
        input_output_aliases={2: 0},
        compiler_params=_params("arbitrary"), name="moe_dispatch")(dest, h2, jnp.zeros((nblk * bm, D_MODEL), F32))

    grid_spec = pltpu.PrefetchScalarGridSpec(
        num_scalar_prefetch=2, grid=(nblk,),
        in_specs=[pl.BlockSpec((bm, D_MODEL), lambda i, be, nu: (i, 0)),
                  pl.BlockSpec((None, D_MODEL, D_EXPERT), lambda i, be, nu: (be[i], 0, 0)),
                  pl.BlockSpec((None, D_MODEL, D_EXPERT), lambda i, be, nu: (be[i], 0, 0)),
                  pl.BlockSpec((None, D_EXPERT, D_MODEL), lambda i, be, nu: (be[i], 0, 0))],
        out_specs=pl.BlockSpec((bm, D_MODEL), lambda i, be, nu: (i, 0)))
    yb = pl.pallas_call(
        _expert_kernel, grid_spec=grid_spec, out_shape=jax.ShapeDtypeStruct((nblk * bm, D_MODEL), F32),
        compiler_params=_params("arbitrary"), name="moe_experts")(blk_expert, n_used, xs, w_e1, w_e3, w_e2)

    tc = td
    return pl.pallas_call(
        functools.partial(_combine_kernel, tc=tc), grid=(n // tc,),
        in_specs=[pl.BlockSpec((None, 1, 2 * tc), lambda i: (i, 0, 0), memory_space=pltpu.SMEM),
                  pl.BlockSpec((tc, D_MODEL), lambda i: (i, 0)),
                  pl.BlockSpec((tc, 8), lambda i: (i, 0)),
                  pl.BlockSpec(memory_space=pl.ANY)],
        out_specs=pl.BlockSpec((tc, D_MODEL), lambda i: (i, 0)),
        out_shape=jax.ShapeDtypeStruct((n, D_MODEL), F32),
        scratch_shapes=[pltpu.VMEM((tc, D_MODEL), F32), pltpu.VMEM((tc, D_MODEL), F32), pltpu.SemaphoreType.DMA(())],
        compiler_params=_params("arbitrary"), name="moe_combine")(dest, x1, wgt, yb)


def _prep_weights(l, g_mix, w_in, b_f, g_qnorm, g_knorm, w_conv, a_log, dt_bias, g_onorm, b_gate,
                  w_br_a, w_br_b, w_out, g_ffn, w_grp, b_grp, w_exp, b_exp, w_e1, w_e3, w_e2):
    w = w_in[l]
    offs = [0]
    for s in (WA, WA, WA, H_A, QK_B, QK_B, V_B, H_B, H_B, V_B, D_MODEL, D_MODEL):
        offs.append(offs[-1] + s)
    col = lambda i, j=None: w[:, offs[i]:offs[(i if j is None else j) + 1]]
    n_small = H_A + 2 * H_B
    w_small = jnp.concatenate([col(3), col(7), col(8)], axis=1)
    zeros8 = jnp.zeros((H_A,), F32)
    bias = jnp.concatenate([b_f[l], zeros8, dt_bias[l]])
    alog = jnp.concatenate([zeros8, zeros8, a_log[l]])
    pad = lambda v, n: jnp.pad(v, (0, n - v.shape[0]))
    rows_t = 32
    return dict(
        g_mix=g_mix[l][None, :],
        wq=col(0).astype(BF16), wk=col(1).astype(BF16), wv=col(2).astype(BF16),
        gq=jnp.tile(g_qnorm[l], H_A)[None, :], gk=jnp.tile(g_knorm[l], H_A)[None, :],
        ws=jnp.pad(w_small, ((0, 0), (0, LANES - n_small))).astype(BF16),
        wst=jnp.pad(w_small.T, ((0, rows_t - n_small), (0, 0))).astype(BF16),
        wu=col(4, 6).astype(BF16),
        bias=pad(bias, LANES)[None, :], alog=pad(alog, LANES)[None, :],
        biast=pad(bias, rows_t)[:, None], alogt=pad(alog, rows_t)[:, None],
        wz=col(9).astype(BF16), wg=col(10, 11).astype(BF16), bg=b_gate[l][None, :],
        w_conv=w_conv[l], g_onorm=g_onorm[l][None, :],
        w_br_a=w_br_a[l].astype(BF16), w_br_b=w_br_b[l].astype(BF16), w_out=w_out[l].astype(BF16),
        g_ffn=g_ffn[l][None, :],
        wr=jnp.pad(jnp.concatenate([w_grp[l], w_exp[l]], axis=1), ((0, 0), (0, LANES - N_GROUPS - N_EXPERTS))).astype(BF16),
        br=pad(jnp.concatenate([b_grp[l], b_exp[l]]), LANES)[None, :],
        w_e1=w_e1[l].astype(BF16), w_e3=w_e3[l].astype(BF16), w_e2=w_e2[l].astype(BF16),
    )


def _layer(x, past_k, past_v, past_logf, s0, conv0, p):
    b, t, _ = x.shape
    past = past_k.shape[1]
    n = b * t
    xf = x.reshape(n, D_MODEL)
    tm = _pick_tile(n, 512)

    q16, k32, k16, v32, v16 = _row_call(
        _inproj_fox_kernel, n, tm, [xf], [p["g_mix"], p["wq"], p["wk"], p["wv"], p["gq"], p["gk"]],
        [((n, WA), BF16, False), ((n, WA), F32, False), ((n, WA), BF16, False), ((n, WA), F32, False),
         ((n, WA), BF16, False)], "inproj_fox")
    small, smallt, u = _row_call(
        _inproj_gdn_kernel, n, tm, [xf],
        [p["g_mix"], p["ws"], p["wst"], p["wu"], p["bias"], p["alog"], p["biast"], p["alogt"]],
        [((n, LANES), F32, False), ((32, n), F32, True), ((n, CONV_DIM), F32, False)], "inproj_gdn")
    z, gates = _row_call(
        _inproj_gate_kernel, n, tm, [xf], [p["g_mix"], p["wz"], p["wg"], p["bg"]],
        [((n, V_B), F32, False), ((n, 2 * D_MODEL), F32, False)], "inproj_gate")

    logf = small[:, :H_A].reshape(b, t, H_A)
    c_all = _cumsum_time(jnp.concatenate([past_logf.astype(F32), logf], axis=1))
    k_all = jnp.concatenate([past_k.reshape(b, past, WA).astype(BF16), k16.reshape(b, t, WA)], axis=1)
    v_all = jnp.concatenate([past_v.reshape(b, past, WA).astype(BF16), v16.reshape(b, t, WA)], axis=1)
    y_a = _fox_attention(q16, k_all.reshape(b * (past + t), WA), v_all.reshape(b * (past + t), WA),
                         c_all[:, past:].reshape(n, H_A), c_all.transpose(0, 2, 1), b, t, past)

    y_b, s_new, conv_new = _gdn(u, small, smallt, z, s0.reshape(b, H_B * DK_B, DV_B), conv0,
                                p["w_conv"], p["g_onorm"], b, t)

    x1, h2, logits = _row_call(
        _merge_kernel, n, tm, [xf, y_a, y_b, gates],
        [p["w_br_a"], p["w_br_b"], p["w_out"], p["g_ffn"], p["wr"], p["br"]],
        [((n, D_MODEL), F32, False), ((n, D_MODEL), F32, False), ((n, LANES), F32, False)], "merge_outproj")
    x2 = _hier_moe(x1, h2, logits, p["w_e1"], p["w_e3"], p["w_e2"])
    return (x2.reshape(b, t, D_MODEL), k32.reshape(b, t, H_A, DH_A), v32.reshape(b, t, H_A, DH_A), logf,
            s_new.reshape(b, H_B, DK_B, DV_B), conv_new)


def kernel(x_prompt, x_sample, cache_fox_k, cache_fox_v, cache_fox_logf, state_gdn, state_gdn_conv, g_mix, w_in, b_f, g_qnorm, g_knorm, w_conv, a_log, dt_bias, g_onorm, b_gate, w_br_a, w_br_b, w_out, g_ffn, w_grp, b_grp, w_exp, b_exp, w_e1, w_e3, w_e2):
    depth = w_in.shape[0]
    bp = x_prompt.shape[0]
    yp, ys = x_prompt, x_sample
    outs_p, outs_s = [], []
    for l in range(depth):
        p = _prep_weights(l, g_mix, w_in, b_f, g_qnorm, g_knorm, w_conv, a_log, dt_bias, g_onorm, b_gate,
                          w_br_a, w_br_b, w_out, g_ffn, w_grp, b_grp, w_exp, b_exp, w_e1, w_e3, w_e2)
        ys, *rest_s = _layer(ys, cache_fox_k[l], cache_fox_v[l], cache_fox_logf[l], state_gdn[l], state_gdn_conv[l], p)
        yp, *rest_p = _layer(
            yp, jnp.zeros((bp, 0, H_A, DH_A), F32), jnp.zeros((bp, 0, H_A, DH_A), F32), jnp.zeros((bp, 0, H_A), F32),
            jnp.zeros((bp, H_B, DK_B, DV_B), F32), jnp.zeros((bp, CONV_W - 1, CONV_DIM), F32), p)
        outs_p.append(rest_p)
        outs_s.append(rest_s)
    stack = lambda outs, i: jnp.stack([o[i] for o in outs])
    return (yp, ys,
            stack(outs_p, 0), stack(outs_p, 1), stack(outs_p, 2), stack(outs_p, 3), stack(outs_p, 4),
            stack(outs_s, 0), stack(outs_s, 1), stack(outs_s, 2), stack(outs_s, 3), stack(outs_s, 4))
```

```python
import functools
import math

import jax
import jax.numpy as jnp
from jax import lax
from jax.experimental import pallas as pl
from jax.experimental.pallas import tpu as pltpu

F32, BF16, I32, U32 = jnp.float32, jnp.bfloat16, jnp.int32, jnp.uint32

D_MODEL = 1024
H_A, DH_A = 8, 64
WA = H_A * DH_A
H_B, DK_B, DV_B = 8, 64, 128
QK_B, V_B = H_B * DK_B, H_B * DV_B
CONV_W = 4
CONV_DIM = 2 * QK_B + V_B
N_GROUPS, EXPERTS_PER_GROUP = 4, 8
N_EXPERTS = N_GROUPS * EXPERTS_PER_GROUP
D_EXPERT = 512
GDN_CHUNK = 64
GDN_CHUNKS_PER_STEP = 4
EPS = 1e-6
LOG2E = math.log2(math.e)
NEG = -1e30
LANES = 128
V7X_VMEM_LIMIT = 56 * 1024 * 1024
MOE_BLOCK = 256


def _params(*sem):
    return pltpu.CompilerParams(dimension_semantics=sem, vmem_limit_bytes=V7X_VMEM_LIMIT)


def _pick_tile(n, cap, mult=8):
    if n <= cap:
        return n
    for t in range(cap - cap % mult, 0, -mult):
        if n % t == 0:
            return t
    return n


def _dot(a, b):
    return jnp.dot(a.astype(BF16), b.astype(BF16), preferred_element_type=F32)


def _dot_t(a, b):
    return lax.dot_general(a.astype(BF16), b.astype(BF16), (((1,), (1,)), ((), ())), preferred_element_type=F32)


def _dot_tl(a, b):
    return lax.dot_general(a.astype(BF16), b.astype(BF16), (((0,), (0,)), ((), ())), preferred_element_type=F32)


def _split(x, n):
    parts = []
    for _ in range(n):
        p = x.astype(BF16)
        parts.append(p)
        x = x - p.astype(F32)
    return parts


def _dot_exact_lhs(ones, x, n):
    acc = None
    for p in _split(x, n):
        t = jnp.dot(ones, p, preferred_element_type=F32)
        acc = t if acc is None else acc + t
    return acc


def _dot_exact_rhs(x, ones, n):
    acc = None
    for p in _split(x, n):
        t = jnp.dot(p, ones, preferred_element_type=F32)
        acc = t if acc is None else acc + t
    return acc


def _iota(shape, dim):
    return lax.broadcasted_iota(I32, shape, dim)


def _softplus(x):
    return jnp.maximum(x, 0.0) + jnp.log1p(jnp.exp(-jnp.abs(x)))


def _silu(x):
    return x * jax.nn.sigmoid(x)


def _rms_rows(x, g):
    return x * lax.rsqrt(jnp.mean(x * x, axis=-1, keepdims=True) + EPS) * g


def _head_ones(width, head):
    shift = head.bit_length() - 1
    assert head == 1 << shift
    r = jnp.right_shift(_iota((width, width), 0), shift)
    c = jnp.right_shift(_iota((width, width), 1), shift)
    return (r == c).astype(BF16)


def _group_sumsq(y, head):
    ones = _head_ones(2 * LANES, head)
    sq = y * y
    parts = [_dot_exact_rhs(sq[:, c:c + 2 * LANES], ones, 2) for c in range(0, y.shape[1], 2 * LANES)]
    return parts[0] if len(parts) == 1 else jnp.concatenate(parts, axis=1)


def _inproj_fox_kernel(x_ref, g_ref, wq_ref, wk_ref, wv_ref, gq_ref, gk_ref,
                       q_ref, k32_ref, k16_ref, v32_ref, v16_ref):
    h = _rms_rows(x_ref[...], g_ref[...]).astype(BF16)

    def headnorm(y, gain):
        ms = _group_sumsq(y, DH_A) * (1.0 / DH_A)
        return y * lax.rsqrt(ms + EPS) * gain

    q = headnorm(jnp.dot(h, wq_ref[...], preferred_element_type=F32), gq_ref[...])
    q_ref[...] = (q * (DH_A ** -0.5 * LOG2E)).astype(BF16)
    k = headnorm(jnp.dot(h, wk_ref[...], preferred_element_type=F32), gk_ref[...])
    k32_ref[...] = k
    k16_ref[...] = k.astype(BF16)
    v = jnp.dot(h, wv_ref[...], preferred_element_type=F32)
    v32_ref[...] = v
    v16_ref[...] = v.astype(BF16)


def _small_epilogue(y, bias, a_log, idx):
    yb = y + bias
    logf = -_softplus(-yb)
    beta = jax.nn.sigmoid(y)
    g = -jnp.exp(a_log) * _softplus(yb)
    return jnp.where(idx < H_A, logf, jnp.where(idx < H_A + H_B, beta, g))


def _inproj_gdn_kernel(x_ref, g_ref, ws_ref, wst_ref, wu_ref, bias_ref, alog_ref, biast_ref, alogt_ref,
                       small_ref, smallt_ref, u_ref):
    h = _rms_rows(x_ref[...], g_ref[...]).astype(BF16)
    ys = jnp.dot(h, ws_ref[...], preferred_element_type=F32)
    small_ref[...] = _small_epilogue(ys, bias_ref[...], alog_ref[...], _iota(ys.shape, 1))
    yt = lax.dot_general(wst_ref[...], h, (((1,), (1,)), ((), ())), preferred_element_type=F32)
    smallt_ref[...] = _small_epilogue(yt, biast_ref[...], alogt_ref[...], _iota(yt.shape, 0))
    for c in range(0, CONV_DIM, 512):
        u_ref[:, c:c + 512] = jnp.dot(h, wu_ref[:, c:c + 512], preferred_element_type=F32)


def _inproj_gate_kernel(x_ref, g_ref, wz_ref, wg_ref, bg_ref, z_ref, gate_ref):
    h = _rms_rows(x_ref[...], g_ref[...]).astype(BF16)
    for c in range(0, V_B, 512):
        z_ref[:, c:c + 512] = jnp.dot(h, wz_ref[:, c:c + 512], preferred_element_type=F32)
    for c in range(0, 2 * D_MODEL, 512):
        y = jnp.dot(h, wg_ref[:, c:c + 512], preferred_element_type=F32)
        gate_ref[:, c:c + 512] = jax.nn.sigmoid(y + bg_ref[:, c:c + 512])


def _row_call(kernel, n, tm, row_inputs, full_inputs, outs, name):
    in_specs = [pl.BlockSpec((tm, a.shape[1]), lambda i: (i, 0)) for a in row_inputs]
    in_specs += [pl.BlockSpec(a.shape, lambda i, nd=a.ndim: (0,) * nd) for a in full_inputs]
    out_specs, out_shapes = [], []
    for shape, dtype, transposed in outs:
        if transposed:
            out_specs.append(pl.BlockSpec((shape[0], tm), lambda i: (0, i)))
        else:
            out_specs.append(pl.BlockSpec((tm, shape[1]), lambda i: (i, 0)))
        out_shapes.append(jax.ShapeDtypeStruct(shape, dtype))
    return pl.pallas_call(kernel, grid=(n // tm,), in_specs=in_specs, out_specs=out_specs, out_shape=out_shapes,
                          compiler_params=_params("arbitrary"), name=name)(*row_inputs, *full_inputs)


AUG_PIECES = 3


def _lane_table(lane, columns):
    rows = next(c for c in columns if c is not None).shape[0]
    out = jnp.zeros((rows, LANES), F32)
    for idx, colv in enumerate(columns):
        if colv is not None:
            out = jnp.where(lane == idx, colv, out)
    return out


def _cumsum_kernel(x_ref, kaug_ref, qaug_ref, carry):
    @pl.when(pl.program_id(1) == 0)
    def _():
        carry[...] = jnp.zeros_like(carry)

    x = x_ref[...]
    t = x.shape[0]
    tril = (_iota((t, t), 0) >= _iota((t, t), 1)).astype(BF16)
    c = _dot_exact_lhs(tril, x, 3) + carry[...]
    carry[...] = c[t - 1:t, :]
    pieces = [p.astype(F32) for p in _split(c * LOG2E, AUG_PIECES)]
    lane = _iota((1, LANES), 1)
    ones = jnp.ones((t, 1), F32)
    n = AUG_PIECES
    for pair in range(H_A // 2):
        a, b = 2 * pair, 2 * pair + 1
        cols = [-p[:, a:a + 1] for p in pieces] + [-p[:, b:b + 1] for p in pieces] + [ones] * n
        kaug_ref[:, pair * LANES:(pair + 1) * LANES] = _lane_table(lane, cols).astype(BF16)
    for hd in range(H_A):
        sel = [ones] * n + [None] * n if hd % 2 == 0 else [None] * n + [ones] * n
        cols = sel + [p[:, hd:hd + 1] for p in pieces]
        qaug_ref[:, hd * LANES:(hd + 1) * LANES] = _lane_table(lane, cols).astype(BF16)


def _forget_companions(x):
    b, tt, c = x.shape
    tc = _pick_tile(tt, 512, 16)
    return pl.pallas_call(
        _cumsum_kernel, grid=(b, tt // tc),
        in_specs=[pl.BlockSpec((None, tc, c), lambda i, j: (i, j, 0))],
        out_specs=[pl.BlockSpec((None, tc, WA), lambda i, j: (i, j, 0)),
                   pl.BlockSpec((None, tc, H_A * LANES), lambda i, j: (i, j, 0))],
        out_shape=[jax.ShapeDtypeStruct((b, tt, WA), BF16), jax.ShapeDtypeStruct((b, tt, H_A * LANES), BF16)],
        scratch_shapes=[pltpu.VMEM((1, c), F32)],
        compiler_params=_params("arbitrary", "arbitrary"), name="fox_cumsum")(x)


def _attn_kernel(qi_ref, kj_ref, last_ref, q_ref, qa_ref, k_ref, ka_ref, v_ref, o_ref, qq_sc, m_sc, l_sc, acc_sc,
                 *, tq, tk, past):
    n = pl.program_id(1)
    i, j = qi_ref[n], kj_ref[n]
    lo = _iota((1, LANES), 1) < DH_A
    halves = (lo, jnp.logical_not(lo))

    @pl.when(j == 0)
    def _():
        m_sc[...] = jnp.full_like(m_sc, NEG)
        l_sc[...] = jnp.zeros_like(l_sc)
        acc_sc[...] = jnp.zeros_like(acc_sc)
        for hd in range(H_A):
            q2 = q_ref[:, (hd // 2) * LANES:(hd // 2 + 1) * LANES]
            qq_sc[hd, :, :LANES] = jnp.where(halves[hd % 2], q2, jnp.zeros_like(q2))
            qq_sc[hd, :, LANES:] = qa_ref[:, hd * LANES:(hd + 1) * LANES]

    def step(masked):
        if masked:
            visible = j * tk + _iota((1, tk), 1) <= past + i * tq + _iota((tq, 1), 0)
        scores = []
        for p in range(H_A // 2):
            sl = slice(p * LANES, (p + 1) * LANES)
            kk = jnp.concatenate([k_ref[:, sl], ka_ref[:, sl]], axis=1)
            scores += [_dot_t(qq_sc[2 * p + half], kk) for half in range(2)]
        probs, alphas = [], []
        for hd in range(H_A):
            s = jnp.where(visible, scores[hd], NEG) if masked else scores[hd]
            m_prev = m_sc[hd]
            m_new = jnp.maximum(m_prev, jnp.max(s, axis=-1, keepdims=True))
            alpha = jnp.exp2(m_prev - m_new)
            pr = jnp.exp2(s - m_new)
            l_sc[hd] = alpha * l_sc[hd] + jnp.sum(pr, axis=-1, keepdims=True)
            m_sc[hd] = m_new
            alphas.append(alpha)
            probs.append(pr.astype(BF16))
        for p in range(H_A // 2):
            v2 = v_ref[:, p * LANES:(p + 1) * LANES]
            pvs = [_dot(probs[2 * p + half], jnp.where(halves[half], v2, jnp.zeros_like(v2))) for half in range(2)]
            acc_sc[p] = jnp.where(lo, alphas[2 * p], alphas[2 * p + 1]) * acc_sc[p] + pvs[0] + pvs[1]

    some_hidden = (j + 1) * tk - 1 > past + i * tq
    pl.when(some_hidden)(functools.partial(step, True))
    pl.when(jnp.logical_not(some_hidden))(functools.partial(step, False))

    @pl.when(last_ref[n] == 1)
    def _():
        for p in range(H_A // 2):
            l2 = jnp.where(lo, l_sc[2 * p], l_sc[2 * p + 1])
            o_ref[:, p * LANES:(p + 1) * LANES] = (acc_sc[p] / l2).astype(BF16)


def _fox_attention(q, q_aug, k_all, k_aug, v_all, b, t, past):
    tt = past + t
    tq = _pick_tile(t, 512, 16)
    tk = _pick_tile(tt, 512, 16)
    nq, nk = t // tq, tt // tk
    pairs = [(i, j) for i in range(nq) for j in range(nk) if j * tk <= past + (i + 1) * tq - 1]
    qi = jnp.array([p[0] for p in pairs], I32)
    kj = jnp.array([p[1] for p in pairs], I32)
    last = jnp.array([1 if (idx + 1 == len(pairs) or pairs[idx + 1][0] != p[0]) else 0
                      for idx, p in enumerate(pairs)], I32)
    grid_spec = pltpu.PrefetchScalarGridSpec(
        num_scalar_prefetch=3, grid=(b, len(pairs)),
        in_specs=[
            pl.BlockSpec((tq, WA), lambda bi, n, qi, kj, la: (bi * nq + qi[n], 0)),
            pl.BlockSpec((tq, H_A * LANES), lambda bi, n, qi, kj, la: (bi * nq + qi[n], 0)),
            pl.BlockSpec((tk, WA), lambda bi, n, qi, kj, la: (bi * nk + kj[n], 0)),
            pl.BlockSpec((tk, WA), lambda bi, n, qi, kj, la: (bi * nk + kj[n], 0)),
            pl.BlockSpec((tk, WA), lambda bi, n, qi, kj, la: (bi * nk + kj[n], 0)),
        ],
        out_specs=pl.BlockSpec((tq, WA), lambda bi, n, qi, kj, la: (bi * nq + qi[n], 0)),
        scratch_shapes=[pltpu.VMEM((H_A, tq, 2 * LANES), BF16), pltpu.VMEM((H_A, tq, 1), F32),
                        pltpu.VMEM((H_A, tq, 1), F32), pltpu.VMEM((H_A // 2, tq, LANES), F32)])
    return pl.pallas_call(
        functools.partial(_attn_kernel, tq=tq, tk=tk, past=past), grid_spec=grid_spec,
        out_shape=jax.ShapeDtypeStruct((b * t, WA), BF16),
        compiler_params=_params("arbitrary", "arbitrary"), name="fox_attention")(qi, kj, last, q, q_aug, k_all, k_aug, v_all)


def _gdn_kernel(u_ref, small_ref, smallt_ref, z_ref, s0_ref, conv0_ref, wconv_ref, gon_ref,
                yb_ref, sout_ref, cout_ref, s_sc, ubuf, *, chunk, cps, nsteps):
    c = pl.program_id(1)
    L, C = chunk, cps
    R = L * C
    lg = L.bit_length() - 1
    keep = CONV_W - 1

    @pl.when(c == 0)
    def _():
        s_sc[...] = s0_ref[...]
        ubuf[8 - keep:8, :] = conv0_ref[...]

    ubuf[8:8 + R, :] = u_ref[...]
    w = wconv_ref[...]
    conv = ubuf[8 - keep:8 - keep + R, :] * w[0:1, :]
    for i in range(1, CONV_W):
        conv = conv + ubuf[8 - keep + i:8 - keep + i + R, :] * w[i:i + 1, :]
    act = _silu(conv)
    tail = ubuf[8 + R - keep:8 + R, :]
    ubuf[8 - keep:8, :] = tail

    @pl.when(c == nsteps - 1)
    def _():
        cout_ref[...] = tail

    sm = small_ref[...]
    smt = smallt_ref[...]
    rr, rc = _iota((R, R), 0), _iota((R, R), 1)
    same = jnp.right_shift(rr, lg) == jnp.right_shift(rc, lg)
    g_col = _dot_exact_lhs(jnp.logical_and(same, rr >= rc).astype(BF16), sm, 3)
    qn_all = act[:, :QK_B]
    qn_all = qn_all * lax.rsqrt(_group_sumsq(qn_all, DK_B) + EPS) * (DK_B ** -0.5)
    kn_all = act[:, QK_B:2 * QK_B]
    kn_all = kn_all * lax.rsqrt(_group_sumsq(kn_all, DK_B) + EPS)

    W2 = 2 * L
    lo64 = _iota((1, LANES), 1) < DK_B
    hi64 = jnp.logical_not(lo64)
    lane2 = _iota((1, W2), 1)
    lo_l, hi_l = lane2 < L, lane2 >= L
    ri, ci = _iota((L, W2), 0), jnp.bitwise_and(_iota((L, W2), 1), L - 1)
    incl, strict = ri >= ci, ri > ci
    eye = (ri == ci).astype(F32)
    n_sq = max(int(math.ceil(math.log2(L))) - 1, 0)
    sub128 = _iota((LANES, 1), 0) < DK_B

    def stack_heads(x, lo, hi):
        return jnp.concatenate([jnp.where(lo, x, 0.0), jnp.where(hi, x, 0.0)], axis=0)

    items = [(ch, p) for ch in range(C) for p in range(H_B // 2)]
    g_rows = []
    for ch in range(C):
        tt, jj = _iota((R, W2), 0), jnp.bitwise_and(_iota((R, W2), 1), L - 1)
        sel = jnp.logical_and(jnp.right_shift(tt, lg) == ch, jnp.bitwise_and(tt, L - 1) <= jj)
        g_rows.append(_dot_exact_rhs(smt, sel.astype(BF16), 3))

    st = {}
    for ch, p in items:
        rs = slice(ch * L, (ch + 1) * L)
        ha, hb = 2 * H_B + 2 * p, 2 * H_B + 2 * p + 1
        gca, gcb = g_col[rs, ha:ha + 1], g_col[rs, hb:hb + 1]
        gla, glb = gca[L - 1:L, :], gcb[L - 1:L, :]
        q2, k2 = qn_all[rs, p * LANES:(p + 1) * LANES], kn_all[rs, p * LANES:(p + 1) * LANES]
        beta2 = jnp.where(lo64, sm[rs, H_B + 2 * p:H_B + 2 * p + 1], sm[rs, H_B + 2 * p + 1:H_B + 2 * p + 2])
        eg2 = jnp.exp(jnp.where(lo64, gca, gcb))
        egl2 = jnp.exp(jnp.where(lo64, gla - gca, glb - gcb))
        kb2 = k2 * beta2
        gr = jnp.where(lo_l, g_rows[ch][ha:ha + 1, :], g_rows[ch][hb:hb + 1, :])
        decay = jnp.exp(jnp.where(incl, jnp.where(lo_l, gca, gcb) - gr, NEG))
        ybd = stack_heads(k2, lo64, hi64)
        va = act[rs, 2 * QK_B + 2 * p * DV_B:2 * QK_B + (2 * p + 1) * DV_B] * sm[rs, H_B + 2 * p:H_B + 2 * p + 1]
        vb = act[rs, 2 * QK_B + (2 * p + 1) * DV_B:2 * QK_B + (2 * p + 2) * DV_B] * sm[rs, H_B + 2 * p + 1:H_B + 2 * p + 2]
        st[ch, p] = dict(
            decay=decay, ybd=ybd, kb2=kb2, q2=q2,
            kbg_bd=stack_heads(kb2 * eg2, lo64, hi64), qd_bd=stack_heads(q2 * eg2, lo64, hi64),
            kd_bd=stack_heads(k2 * egl2, lo64, hi64), vst=jnp.concatenate([va, vb], axis=0),
            gl2=jnp.where(sub128, jnp.exp(gla), jnp.exp(glb)))
    for it in items:
        d = st[it]
        d["m"] = jnp.where(strict, _dot_t(d["kb2"], d["ybd"]) * d["decay"], 0.0)
        d["attn"] = jnp.where(incl, _dot_t(d["q2"], d["ybd"]) * d["decay"], 0.0)
    for it in items:
        d = st[it]
        d["t"] = eye - d["m"]
        d["mp"] = d["m"]
    for lvl in range(n_sq):
        for it in items:
            d = st[it]
            d["mp"] = _dot(d["mp"], stack_heads(d["mp"], lo_l, hi_l))
        for it in items:
            d = st[it]
            d["t"] = d["t"] + _dot(d["t"], stack_heads(d["mp"], lo_l, hi_l))
    for it in items:
        d = st[it]
        d["ust"] = _dot(stack_heads(d["t"], lo_l, hi_l), d["vst"])
        d["w2"] = _dot(d["t"], d["kbg_bd"])
    for ch in range(C):
        rs = slice(ch * L, (ch + 1) * L)
        s2s, ass = [], []
        for p in range(H_B // 2):
            d = st[ch, p]
            s2 = s_sc[p * LANES:(p + 1) * LANES, :]
            s2s.append(s2)
            ass.append(_dot(jnp.concatenate([stack_heads(d["w2"], lo64, hi64), d["qd_bd"]], axis=0), s2))
        for p in range(H_B // 2):
            d = st[ch, p]
            v_new = d["ust"] - ass[p][:W2, :]
            o_st = ass[p][W2:, :] + _dot(stack_heads(d["attn"], lo_l, hi_l), v_new)
            s_sc[p * LANES:(p + 1) * LANES, :] = s2s[p] * d["gl2"] + _dot_tl(d["kd_bd"], v_new)
            for half in range(2):
                hd = 2 * p + half
                o = o_st[half * L:(half + 1) * L, :]
                on = o * lax.rsqrt(jnp.mean(o * o, axis=-1, keepdims=True) + EPS) * gon_ref[...]
                zs = z_ref[rs, hd * DV_B:(hd + 1) * DV_B]
                yb_ref[rs, hd * DV_B:(hd + 1) * DV_B] = (on * _silu(zs)).astype(BF16)

    @pl.when(c == nsteps - 1)
    def _():
        sout_ref[...] = s_sc[...]


def _gdn(u, small, smallt, z, s0, conv0, w_conv, g_onorm, b, t):
    clen = GDN_CHUNK if t % GDN_CHUNK == 0 else t
    assert clen & (clen - 1) == 0, clen
    cps = GDN_CHUNKS_PER_STEP if (t // clen) % GDN_CHUNKS_PER_STEP == 0 else 1
    chunk = clen * cps
    nc = t // chunk
    smallt3 = smallt.reshape(smallt.shape[0], b * nc, chunk).transpose(1, 0, 2)
    keep = CONV_W - 1
    return pl.pallas_call(
        functools.partial(_gdn_kernel, chunk=clen, cps=cps, nsteps=nc), grid=(b, nc),
        in_specs=[
            pl.BlockSpec((chunk, CONV_DIM), lambda i, j: (i * nc + j, 0)),
            pl.BlockSpec((chunk, LANES), lambda i, j: (i * nc + j, 0)),
            pl.BlockSpec((None, smallt.shape[0], chunk), lambda i, j: (i * nc + j, 0, 0)),
            pl.BlockSpec((chunk, V_B), lambda i, j: (i * nc + j, 0)),
            pl.BlockSpec((None, H_B * DK_B, DV_B), lambda i, j: (i, 0, 0)),
            pl.BlockSpec((None, keep, CONV_DIM), lambda i, j: (i, 0, 0)),
            pl.BlockSpec((CONV_W, CONV_DIM), lambda i, j: (0, 0)),
            pl.BlockSpec((1, DV_B), lambda i, j: (0, 0)),
        ],
        out_specs=[
            pl.BlockSpec((chunk, V_B), lambda i, j: (i * nc + j, 0)),
            pl.BlockSpec((None, H_B * DK_B, DV_B), lambda i, j: (i, 0, 0)),
            pl.BlockSpec((None, keep, CONV_DIM), lambda i, j: (i, 0, 0)),
        ],
        out_shape=[jax.ShapeDtypeStruct((b * t, V_B), BF16),
                   jax.ShapeDtypeStruct((b, H_B * DK_B, DV_B), F32),
                   jax.ShapeDtypeStruct((b, keep, CONV_DIM), F32)],
        scratch_shapes=[pltpu.VMEM((H_B * DK_B, DV_B), F32), pltpu.VMEM((8 + chunk, CONV_DIM), F32)],
        compiler_params=_params("arbitrary", "arbitrary"), name="gdn_chunked",
    )(u, small, smallt3, z, s0, conv0, w_conv, g_onorm)


def _merge_kernel(x_ref, ya_ref, yb_ref, gate_ref, wa_ref, wb_ref, wo_ref, gf_ref, wr_ref, br_ref,
                  x1_ref, h2_ref, logit_ref):
    br_a = jnp.dot(ya_ref[...], wa_ref[...], preferred_element_type=F32)
    br_b = jnp.dot(yb_ref[...], wb_ref[...], preferred_element_type=F32)
    merged = gate_ref[:, :D_MODEL] * br_a + gate_ref[:, D_MODEL:] * br_b
    x1 = x_ref[...] + _dot(merged, wo_ref[...])
    x1_ref[...] = x1
    h2 = _rms_rows(x1, gf_ref[...])
    h2_ref[...] = h2
    logit_ref[...] = _dot(h2, wr_ref[...]) + br_ref[...]


def _route_kernel(logit_ref, meta_ref, wgt_ref, cnt_ref, run, start, *, block):
    sweep, step = pl.program_id(0), pl.program_id(1)

    @pl.when(jnp.logical_and(sweep == 0, step == 0))
    def _():
        run[...] = jnp.zeros_like(run)
        start[...] = jnp.zeros_like(start)

    @pl.when(jnp.logical_and(sweep == 1, step == 0))
    def _():
        counts = run[...]
        cnt_ref[...] = counts
        padded = jnp.floor((counts + (block - 1)) * (1.0 / block)) * block
        before_e = (_iota((LANES, LANES), 0) < _iota((LANES, LANES), 1)).astype(BF16)
        start[...] = _dot_exact_rhs(jnp.broadcast_to(padded, (8, LANES)), before_e, 3)[0:1, :]
        run[...] = jnp.zeros_like(run)

    x = logit_ref[...]
    tm = x.shape[0]
    lane = _iota((1, LANES), 1)
    lanef = lane.astype(F32)
    big = float(LANES)

    def first_argmax(vals, valid):
        mx = jnp.max(vals, axis=-1, keepdims=True)
        idx = jnp.min(jnp.where(jnp.logical_and(vals == mx, valid), lanef, big), axis=-1, keepdims=True)
        return mx, idx.astype(I32)

    gmask = lane < N_GROUPS
    gmax, gsel = first_argmax(jnp.where(gmask, x, NEG), gmask)
    p_g = 1.0 / jnp.sum(jnp.where(gmask, jnp.exp(x - gmax), 0.0), axis=-1, keepdims=True)
    first = N_GROUPS + EXPERTS_PER_GROUP * gsel
    emask = jnp.logical_and(lane >= first, lane < first + EXPERTS_PER_GROUP)
    ev = jnp.where(emask, x, NEG)
    m1, i1 = first_argmax(ev, emask)
    emask2 = jnp.logical_and(emask, lane != i1)
    m2, i2 = first_argmax(jnp.where(emask2, x, NEG), emask2)
    e21 = jnp.exp(m2 - m1)
    w1 = p_g / (1.0 + e21)
    w2 = p_g * e21 / (1.0 + e21)
    e1, e2 = i1 - N_GROUPS, i2 - N_GROUPS
    oh1, oh2 = lane == e1, lane == e2
    onehot = jnp.logical_or(oh1, oh2).astype(BF16)
    strict = (_iota((tm, tm), 0) > _iota((tm, tm), 1)).astype(BF16)
    slot = jnp.dot(strict, onehot, preferred_element_type=F32) + run[...] + start[...]
    run[...] = run[...] + jnp.sum(onehot.astype(F32), axis=0, keepdims=True)

    @pl.when(sweep == 1)
    def _():
        d1 = jnp.sum(jnp.where(oh1, slot, 0.0), axis=-1, keepdims=True).astype(I32)
        d2 = jnp.sum(jnp.where(oh2, slot, 0.0), axis=-1, keepdims=True).astype(I32)
        l8 = _iota((1, 8), 1)
        meta_ref[...] = jnp.where(l8 == 0, e1, jnp.where(l8 == 1, e2, jnp.where(l8 == 2, d1, jnp.where(l8 == 3, d2, 0))))
        wgt_ref[...] = jnp.where(l8 == 0, w1, jnp.where(l8 == 1, w2, 0.0))


def _row_copy(src, dst, sem, s, d):
    return pltpu.make_async_copy(src.at[pl.ds(s, 1)], dst.at[pl.ds(d, 1)], sem)


def _dispatch_kernel(dest_ref, h_ref, zero_ref, xs_ref, sem, *, td):
    del zero_ref

    def issue(t, carry):
        _row_copy(h_ref, xs_ref, sem, t, dest_ref[0, t]).start()
        _row_copy(h_ref, xs_ref, sem, t, dest_ref[0, td + t]).start()
        return carry

    lax.fori_loop(0, td, issue, 0)

    def drain(t, carry):
        _row_copy(h_ref, xs_ref, sem, 0, 0).wait()
        _row_copy(h_ref, xs_ref, sem, 0, 0).wait()
        return carry

    lax.fori_loop(0, td, drain, 0)


def _expert_kernel(be_ref, nu_ref, x_ref, w1_ref, w3_ref, w2_ref, y_ref):
    blk = pl.program_id(0)

    @pl.when(blk < nu_ref[0])
    def _():
        x = x_ref[...].astype(BF16)
        a = jnp.dot(x, w1_ref[...], preferred_element_type=F32)
        g = jnp.dot(x, w3_ref[...], preferred_element_type=F32)
        y_ref[...] = _dot(_silu(a) * g, w2_ref[...])

    @pl.when(blk >= nu_ref[0])
    def _():
        y_ref[...] = jnp.zeros_like(y_ref)


def _combine_kernel(dest_ref, x1_ref, wgt_ref, yb_ref, o_ref, g0, g1, sem, *, tc):
    def issue(t, carry):
        _row_copy(yb_ref, g0, sem, dest_ref[0, t], t).start()
        _row_copy(yb_ref, g1, sem, dest_ref[0, tc + t], t).start()
        return carry

    lax.fori_loop(0, tc, issue, 0)

    def drain(t, carry):
        _row_copy(yb_ref, g0, sem, 0, 0).wait()
        _row_copy(yb_ref, g1, sem, 0, 0).wait()
        return carry

    lax.fori_loop(0, tc, drain, 0)
    wg = wgt_ref[...]
    o_ref[...] = x1_ref[...] + (g0[...] * wg[:, 0:1] + g1[...] * wg[:, 1:2])


def _hier_moe(x1, h2, logits, w_e1, w_e3, w_e2):
    n = x1.shape[0]
    tr = _pick_tile(n, 512)
    bm = MOE_BLOCK
    assert bm & (bm - 1) == 0
    meta, wgt, cnt = pl.pallas_call(
        functools.partial(_route_kernel, block=bm), grid=(2, n // tr),
        in_specs=[pl.BlockSpec((tr, LANES), lambda s, i: (i, 0))],
        out_specs=[pl.BlockSpec((tr, 8), lambda s, i: (i * s, 0)), pl.BlockSpec((tr, 8), lambda s, i: (i * s, 0)),
                   pl.BlockSpec((1, LANES), lambda s, i: (0, 0))],
        out_shape=[jax.ShapeDtypeStruct((n, 8), I32), jax.ShapeDtypeStruct((n, 8), F32),
                   jax.ShapeDtypeStruct((1, LANES), F32)],
        scratch_shapes=[pltpu.VMEM((1, LANES), F32), pltpu.VMEM((1, LANES), F32)],
        compiler_params=_params("arbitrary", "arbitrary"), name="moe_route")(logits)

    counts = cnt[0, :N_EXPERTS].astype(I32)
    pad_end = jnp.cumsum(((counts + bm - 1) // bm) * bm)
    nblk = -(-(2 * n) // bm) + N_EXPERTS
    n_used = (pad_end[-1] // bm).astype(I32).reshape(1)
    blk_first = jnp.arange(nblk, dtype=I32) * bm
    blk_expert = jnp.minimum(jnp.sum((pad_end[None, :] <= blk_first[:, None]).astype(I32), axis=1), N_EXPERTS - 1)

    td = _pick_tile(n, 256)
    dest = meta[:, 2:4].reshape(n // td, td, 2).transpose(0, 2, 1).reshape(n // td, 1, 2 * td)
    xs = pl.pallas_call(
        functools.partial(_dispatch_kernel, td=td), grid=(n // td,),
        in_specs=[pl.BlockSpec((None, 1, 2 * td), lambda i: (i, 0, 0), memory_space=pltpu.SMEM),
                  pl.BlockSpec((td, D_MODEL), lambda i: (i, 0)),
                  pl.BlockSpec(memory_space=pl.ANY)],
        out_specs=pl.BlockSpec(memory_space=pl.ANY),
        out_shape=jax.ShapeDtypeStruct((nblk * bm, D_MODEL), F32),
        scratch_shapes=[pltpu.SemaphoreType.DMA(())],
        input_output_aliases={2: 0},
        compiler_params=_params("arbitrary"), name="moe_dispatch")(dest, h2, jnp.zeros((nblk * bm, D_MODEL), F32))

    grid_spec = pltpu.PrefetchScalarGridSpec(
        num_scalar_prefetch=2, grid=(nblk,),
        in_specs=[pl.BlockSpec((bm, D_MODEL), lambda i, be, nu: (i, 0)),
                  pl.BlockSpec((None, D_MODEL, D_EXPERT), lambda i, be, nu: (be[i], 0, 0)),
                  pl.BlockSpec((None, D_MODEL, D_EXPERT), lambda i, be, nu: (be[i], 0, 0)),
                  pl.BlockSpec((None, D_EXPERT, D_MODEL), lambda i, be, nu: (be[i], 0, 0))],
        out_specs=pl.BlockSpec((bm, D_MODEL), lambda i, be, nu: (i, 0)))
    yb = pl.pallas_call(
        _expert_kernel, grid_spec=grid_spec, out_shape=jax.ShapeDtypeStruct((nblk * bm, D_MODEL), F32),
        compiler_params=_params("arbitrary"), name="moe_experts")(blk_expert, n_used, xs, w_e1, w_e3, w_e2)

    tc = td
    return pl.pallas_call(
        functools.partial(_combine_kernel, tc=tc), grid=(n // tc,),
        in_specs=[pl.BlockSpec((None, 1, 2 * tc), lambda i: (i, 0, 0), memory_space=pltpu.SMEM),
                  pl.BlockSpec((tc, D_MODEL), lambda i: (i, 0)),
                  pl.BlockSpec((tc, 8), lambda i: (i, 0)),
                  pl.BlockSpec(memory_space=pl.ANY)],
        out_specs=pl.BlockSpec((tc, D_MODEL), lambda i: (i, 0)),
        out_shape=jax.ShapeDtypeStruct((n, D_MODEL), F32),
        scratch_shapes=[pltpu.VMEM((tc, D_MODEL), F32), pltpu.VMEM((tc, D_MODEL), F32), pltpu.SemaphoreType.DMA(())],
        compiler_params=_params("arbitrary"), name="moe_combine")(dest, x1, wgt, yb)


def _prep_weights(l, g_mix, w_in, b_f, g_qnorm, g_knorm, w_conv, a_log, dt_bias, g_onorm, b_gate,
                  w_br_a, w_br_b, w_out, g_ffn, w_grp, b_grp, w_exp, b_exp, w_e1, w_e3, w_e2):
    w = w_in[l]
    offs = [0]
    for s in (WA, WA, WA, H_A, QK_B, QK_B, V_B, H_B, H_B, V_B, D_MODEL, D_MODEL):
        offs.append(offs[-1] + s)
    col = lambda i, j=None: w[:, offs[i]:offs[(i if j is None else j) + 1]]
    n_small = H_A + 2 * H_B
    w_small = jnp.concatenate([col(3), col(7), col(8)], axis=1)
    zeros8 = jnp.zeros((H_A,), F32)
    bias = jnp.concatenate([b_f[l], zeros8, dt_bias[l]])
    alog = jnp.concatenate([zeros8, zeros8, a_log[l]])
    pad = lambda v, n: jnp.pad(v, (0, n - v.shape[0]))
    rows_t = 32
    return dict(
        g_mix=g_mix[l][None, :],
        wq=col(0).astype(BF16), wk=col(1).astype(BF16), wv=col(2).astype(BF16),
        gq=jnp.tile(g_qnorm[l], H_A)[None, :], gk=jnp.tile(g_knorm[l], H_A)[None, :],
        ws=jnp.pad(w_small, ((0, 0), (0, LANES - n_small))).astype(BF16),
        wst=jnp.pad(w_small.T, ((0, rows_t - n_small), (0, 0))).astype(BF16),
        wu=col(4, 6).astype(BF16),
        bias=pad(bias, LANES)[None, :], alog=pad(alog, LANES)[None, :],
        biast=pad(bias, rows_t)[:, None], alogt=pad(alog, rows_t)[:, None],
        wz=col(9).astype(BF16), wg=col(10, 11).astype(BF16), bg=b_gate[l][None, :],
        w_conv=w_conv[l], g_onorm=g_onorm[l][None, :],
        w_br_a=w_br_a[l].astype(BF16), w_br_b=w_br_b[l].astype(BF16), w_out=w_out[l].astype(BF16),
        g_ffn=g_ffn[l][None, :],
        wr=jnp.pad(jnp.concatenate([w_grp[l], w_exp[l]], axis=1), ((0, 0), (0, LANES - N_GROUPS - N_EXPERTS))).astype(BF16),
        br=pad(jnp.concatenate([b_grp[l], b_exp[l]]), LANES)[None, :],
        w_e1=w_e1[l].astype(BF16), w_e3=w_e3[l].astype(BF16), w_e2=w_e2[l].astype(BF16),
    )


def _layer(x, past_k, past_v, past_logf, s0, conv0, p):
    b, t, _ = x.shape
    past = past_k.shape[1]
    n = b * t
    xf = x.reshape(n, D_MODEL)
    tm = _pick_tile(n, 512)

    q16, k32, k16, v32, v16 = _row_call(
        _inproj_fox_kernel, n, tm, [xf], [p["g_mix"], p["wq"], p["wk"], p["wv"], p["gq"], p["gk"]],
        [((n, WA), BF16, False), ((n, WA), F32, False), ((n, WA), BF16, False), ((n, WA), F32, False),
         ((n, WA), BF16, False)], "inproj_fox")
    small, smallt, u = _row_call(
        _inproj_gdn_kernel, n, tm, [xf],
        [p["g_mix"], p["ws"], p["wst"], p["wu"], p["bias"], p["alog"], p["biast"], p["alogt"]],
        [((n, LANES), F32, False), ((32, n), F32, True), ((n, CONV_DIM), F32, False)], "inproj_gdn")
    z, gates = _row_call(
        _inproj_gate_kernel, n, tm, [xf], [p["g_mix"], p["wz"], p["wg"], p["bg"]],
        [((n, V_B), F32, False), ((n, 2 * D_MODEL), F32, False)], "inproj_gate")

    logf = small[:, :H_A].reshape(b, t, H_A)
    k_aug, q_aug = _forget_companions(jnp.concatenate([past_logf.astype(F32), logf], axis=1))
    k_all = jnp.concatenate([past_k.reshape(b, past, WA).astype(BF16), k16.reshape(b, t, WA)], axis=1)
    v_all = jnp.concatenate([past_v.reshape(b, past, WA).astype(BF16), v16.reshape(b, t, WA)], axis=1)
    y_a = _fox_attention(q16, q_aug[:, past:].reshape(n, H_A * LANES), k_all.reshape(b * (past + t), WA),
                         k_aug.reshape(b * (past + t), WA), v_all.reshape(b * (past + t), WA), b, t, past)

    y_b, s_new, conv_new = _gdn(u, small, smallt, z, s0.reshape(b, H_B * DK_B, DV_B), conv0,
                                p["w_conv"], p["g_onorm"], b, t)

    x1, h2, logits = _row_call(
        _merge_kernel, n, tm, [xf, y_a, y_b, gates],
        [p["w_br_a"], p["w_br_b"], p["w_out"], p["g_ffn"], p["wr"], p["br"]],
        [((n, D_MODEL), F32, False), ((n, D_MODEL), F32, False), ((n, LANES), F32, False)], "merge_outproj")
    x2 = _hier_moe(x1, h2, logits, p["w_e1"], p["w_e3"], p["w_e2"])
    return (x2.reshape(b, t, D_MODEL), k32.reshape(b, t, H_A, DH_A), v32.reshape(b, t, H_A, DH_A), logf,
            s_new.reshape(b, H_B, DK_B, DV_B), conv_new)


def kernel(x_prompt, x_sample, cache_fox_k, cache_fox_v, cache_fox_logf, state_gdn, state_gdn_conv, g_mix, w_in, b_f, g_qnorm, g_knorm, w_conv, a_log, dt_bias, g_onorm, b_gate, w_br_a, w_br_b, w_out, g_ffn, w_grp, b_grp, w_exp, b_exp, w_e1, w_e3, w_e2):
    depth = w_in.shape[0]
    bp = x_prompt.shape[0]
    yp, ys = x_prompt, x_sample
    outs_p, outs_s = [], []
    for l in range(depth):
        p = _prep_weights(l, g_mix, w_in, b_f, g_qnorm, g_knorm, w_conv, a_log, dt_bias, g_onorm, b_gate,
                          w_br_a, w_br_b, w_out, g_ffn, w_grp, b_grp, w_exp, b_exp, w_e1, w_e3, w_e2)
        ys, *rest_s = _layer(ys, cache_fox_k[l], cache_fox_v[l], cache_fox_logf[l], state_gdn[l], state_gdn_conv[l], p)
        yp, *rest_p = _layer(
            yp, jnp.zeros((bp, 0, H_A, DH_A), F32), jnp.zeros((bp, 0, H_A, DH_A), F32), jnp.zeros((bp, 0, H_A), F32),
            jnp.zeros((bp, H_B, DK_B, DV_B), F32), jnp.zeros((bp, CONV_W - 1, CONV_DIM), F32), p)
        outs_p.append(rest_p)
        outs_s.append(rest_s)
    stack = lambda outs, i: jnp.stack([o[i] for o in outs])
    return (yp, ys,
            stack(outs_p, 0), stack(outs_p, 1), stack(outs_p, 2), stack(outs_p, 3), stack(outs_p, 4),
            stack(outs_s, 0), stack(outs_s, 1), stack(outs_s, 2), stack(outs_s, 3), stack(outs_s, 4))
```

```python
import functools
import math

import jax
import jax.numpy as jnp
from jax import lax
from jax.experimental import pallas as pl
from jax.experimental.pallas import tpu as pltpu

F32, BF16, I32, U32 = jnp.float32, jnp.bfloat16, jnp.int32, jnp.uint32

D_MODEL = 1024
H_A, DH_A = 8, 64
WA = H_A * DH_A
H_B, DK_B, DV_B = 8, 64, 128
QK_B, V_B = H_B * DK_B, H_B * DV_B
CONV_W = 4
CONV_DIM = 2 * QK_B + V_B
N_GROUPS, EXPERTS_PER_GROUP = 4, 8
N_EXPERTS = N_GROUPS * EXPERTS_PER_GROUP
D_EXPERT = 512
GDN_CHUNK = 64
GDN_CHUNKS_PER_STEP = 4
EPS = 1e-6
LOG2E = math.log2(math.e)
NEG = -1e30
LANES = 128
V7X_VMEM_LIMIT = 56 * 1024 * 1024
MOE_BLOCK = 256


def _params(*sem):
    return pltpu.CompilerParams(dimension_semantics=sem, vmem_limit_bytes=V7X_VMEM_LIMIT)


def _pick_tile(n, cap, mult=8):
    if n <= cap:
        return n
    for t in range(cap - cap % mult, 0, -mult):
        if n % t == 0:
            return t
    return n


def _dot(a, b):
    return jnp.dot(a.astype(BF16), b.astype(BF16), preferred_element_type=F32)


def _dot_t(a, b):
    return lax.dot_general(a.astype(BF16), b.astype(BF16), (((1,), (1,)), ((), ())), preferred_element_type=F32)


def _dot_tl(a, b):
    return lax.dot_general(a.astype(BF16), b.astype(BF16), (((0,), (0,)), ((), ())), preferred_element_type=F32)


def _split(x, n):
    parts = []
    for _ in range(n):
        p = x.astype(BF16)
        parts.append(p)
        x = x - p.astype(F32)
    return parts


def _dot_exact_lhs(ones, x, n):
    acc = None
    for p in _split(x, n):
        t = jnp.dot(ones, p, preferred_element_type=F32)
        acc = t if acc is None else acc + t
    return acc


def _dot_exact_rhs(x, ones, n):
    acc = None
    for p in _split(x, n):
        t = jnp.dot(p, ones, preferred_element_type=F32)
        acc = t if acc is None else acc + t
    return acc


def _iota(shape, dim):
    return lax.broadcasted_iota(I32, shape, dim)


def _softplus(x):
    return jnp.maximum(x, 0.0) + jnp.log1p(jnp.exp(-jnp.abs(x)))


def _silu(x):
    return x * jax.nn.sigmoid(x)


def _rms_rows(x, g):
    return x * lax.rsqrt(jnp.mean(x * x, axis=-1, keepdims=True) + EPS) * g


def _head_ones(width, head):
    shift = head.bit_length() - 1
    assert head == 1 << shift
    r = jnp.right_shift(_iota((width, width), 0), shift)
    c = jnp.right_shift(_iota((width, width), 1), shift)
    return (r == c).astype(BF16)


def _group_sumsq(y, head):
    ones = _head_ones(2 * LANES, head)
    sq = y * y
    parts = [_dot_exact_rhs(sq[:, c:c + 2 * LANES], ones, 2) for c in range(0, y.shape[1], 2 * LANES)]
    return parts[0] if len(parts) == 1 else jnp.concatenate(parts, axis=1)


def _inproj_fox_kernel(x_ref, g_ref, wq_ref, wk_ref, wv_ref, gq_ref, gk_ref,
                       q_ref, k32_ref, k16_ref, v32_ref, v16_ref):
    h = _rms_rows(x_ref[...], g_ref[...]).astype(BF16)

    def headnorm(y, gain):
        ms = _group_sumsq(y, DH_A) * (1.0 / DH_A)
        return y * lax.rsqrt(ms + EPS) * gain

    q = headnorm(jnp.dot(h, wq_ref[...], preferred_element_type=F32), gq_ref[...])
    q_ref[...] = (q * (DH_A ** -0.5 * LOG2E)).astype(BF16)
    k = headnorm(jnp.dot(h, wk_ref[...], preferred_element_type=F32), gk_ref[...])
    k32_ref[...] = k
    k16_ref[...] = k.astype(BF16)
    v = jnp.dot(h, wv_ref[...], preferred_element_type=F32)
    v32_ref[...] = v
    v16_ref[...] = v.astype(BF16)


def _small_epilogue(y, bias, a_log, idx):
    yb = y + bias
    logf = -_softplus(-yb)
    beta = jax.nn.sigmoid(y)
    g = -jnp.exp(a_log) * _softplus(yb)
    return jnp.where(idx < H_A, logf, jnp.where(idx < H_A + H_B, beta, g))


def _inproj_gdn_kernel(x_ref, g_ref, ws_ref, wst_ref, wu_ref, bias_ref, alog_ref, biast_ref, alogt_ref,
                       small_ref, smallt_ref, u_ref):
    h = _rms_rows(x_ref[...], g_ref[...]).astype(BF16)
    ys = jnp.dot(h, ws_ref[...], preferred_element_type=F32)
    small_ref[...] = _small_epilogue(ys, bias_ref[...], alog_ref[...], _iota(ys.shape, 1))
    yt = lax.dot_general(wst_ref[...], h, (((1,), (1,)), ((), ())), preferred_element_type=F32)
    smallt_ref[...] = _small_epilogue(yt, biast_ref[...], alogt_ref[...], _iota(yt.shape, 0))
    for c in range(0, CONV_DIM, 512):
        u_ref[:, c:c + 512] = jnp.dot(h, wu_ref[:, c:c + 512], preferred_element_type=F32)


def _inproj_gate_kernel(x_ref, g_ref, wz_ref, wg_ref, bg_ref, z_ref, gate_ref):
    h = _rms_rows(x_ref[...], g_ref[...]).astype(BF16)
    for c in range(0, V_B, 512):
        z_ref[:, c:c + 512] = jnp.dot(h, wz_ref[:, c:c + 512], preferred_element_type=F32)
    for c in range(0, 2 * D_MODEL, 512):
        y = jnp.dot(h, wg_ref[:, c:c + 512], preferred_element_type=F32)
        gate_ref[:, c:c + 512] = jax.nn.sigmoid(y + bg_ref[:, c:c + 512])


def _row_call(kernel, n, tm, row_inputs, full_inputs, outs, name):
    in_specs = [pl.BlockSpec((tm, a.shape[1]), lambda i: (i, 0)) for a in row_inputs]
    in_specs += [pl.BlockSpec(a.shape, lambda i, nd=a.ndim: (0,) * nd) for a in full_inputs]
    out_specs, out_shapes = [], []
    for shape, dtype, transposed in outs:
        if transposed:
            out_specs.append(pl.BlockSpec((shape[0], tm), lambda i: (0, i)))
        else:
            out_specs.append(pl.BlockSpec((tm, shape[1]), lambda i: (i, 0)))
        out_shapes.append(jax.ShapeDtypeStruct(shape, dtype))
    return pl.pallas_call(kernel, grid=(n // tm,), in_specs=in_specs, out_specs=out_specs, out_shape=out_shapes,
                          compiler_params=_params("arbitrary"), name=name)(*row_inputs, *full_inputs)


AUG_PIECES = 3


def _lane_table(lane, columns):
    rows = next(c for c in columns if c is not None).shape[0]
    out = jnp.zeros((rows, LANES), F32)
    for idx, colv in enumerate(columns):
        if colv is not None:
            out = jnp.where(lane == idx, colv, out)
    return out


def _cumsum_kernel(x_ref, kaug_ref, qaug_ref, carry):
    @pl.when(pl.program_id(1) == 0)
    def _():
        carry[...] = jnp.zeros_like(carry)

    x = x_ref[...]
    t = x.shape[0]
    tril = (_iota((t, t), 0) >= _iota((t, t), 1)).astype(BF16)
    c = _dot_exact_lhs(tril, x, 3) + carry[...]
    carry[...] = c[t - 1:t, :]
    pieces = [p.astype(F32) for p in _split(c * LOG2E, AUG_PIECES)]
    lane = _iota((1, LANES), 1)
    ones = jnp.ones((t, 1), F32)
    n = AUG_PIECES
    for pair in range(H_A // 2):
        a, b = 2 * pair, 2 * pair + 1
        cols = [-p[:, a:a + 1] for p in pieces] + [-p[:, b:b + 1] for p in pieces] + [ones] * n
        kaug_ref[:, pair * LANES:(pair + 1) * LANES] = _lane_table(lane, cols).astype(BF16)
    for hd in range(H_A):
        sel = [ones] * n + [None] * n if hd % 2 == 0 else [None] * n + [ones] * n
        cols = sel + [p[:, hd:hd + 1] for p in pieces]
        qaug_ref[:, hd * LANES:(hd + 1) * LANES] = _lane_table(lane, cols).astype(BF16)


def _forget_companions(x):
    b, tt, c = x.shape
    tc = _pick_tile(tt, 512, 16)
    return pl.pallas_call(
        _cumsum_kernel, grid=(b, tt // tc),
        in_specs=[pl.BlockSpec((None, tc, c), lambda i, j: (i, j, 0))],
        out_specs=[pl.BlockSpec((None, tc, WA), lambda i, j: (i, j, 0)),
                   pl.BlockSpec((None, tc, H_A * LANES), lambda i, j: (i, j, 0))],
        out_shape=[jax.ShapeDtypeStruct((b, tt, WA), BF16), jax.ShapeDtypeStruct((b, tt, H_A * LANES), BF16)],
        scratch_shapes=[pltpu.VMEM((1, c), F32)],
        compiler_params=_params("arbitrary", "arbitrary"), name="fox_cumsum")(x)


def _attn_kernel(qi_ref, kj_ref, last_ref, q_ref, qa_ref, k_ref, ka_ref, v_ref, o_ref, qq_sc, m_sc, l_sc, acc_sc,
                 *, tq, tk, past):
    n = pl.program_id(1)
    i, j = qi_ref[n], kj_ref[n]
    lo = _iota((1, LANES), 1) < DH_A
    halves = (lo, jnp.logical_not(lo))

    @pl.when(j == 0)
    def _():
        m_sc[...] = jnp.full_like(m_sc, NEG)
        l_sc[...] = jnp.zeros_like(l_sc)
        acc_sc[...] = jnp.zeros_like(acc_sc)
        for hd in range(H_A):
            q2 = q_ref[:, (hd // 2) * LANES:(hd // 2 + 1) * LANES]
            qq_sc[hd, :, :LANES] = jnp.where(halves[hd % 2], q2, jnp.zeros_like(q2))
            qq_sc[hd, :, LANES:] = qa_ref[:, hd * LANES:(hd + 1) * LANES]

    def step(masked):
        if masked:
            visible = j * tk + _iota((1, tk), 1) <= past + i * tq + _iota((tq, 1), 0)
        scores = []
        for p in range(H_A // 2):
            sl = slice(p * LANES, (p + 1) * LANES)
            kk = jnp.concatenate([k_ref[:, sl], ka_ref[:, sl]], axis=1)
            scores += [_dot_t(qq_sc[2 * p + half], kk) for half in range(2)]
        probs, alphas = [], []
        for hd in range(H_A):
            s = jnp.where(visible, scores[hd], NEG) if masked else scores[hd]
            m_prev = m_sc[hd]
            m_new = jnp.maximum(m_prev, jnp.max(s, axis=-1, keepdims=True))
            alpha = jnp.exp2(m_prev - m_new)
            pr = jnp.exp2(s - m_new)
            l_sc[hd] = alpha * l_sc[hd] + jnp.sum(pr, axis=-1, keepdims=True)
            m_sc[hd] = m_new
            alphas.append(alpha)
            probs.append(pr.astype(BF16))
        for p in range(H_A // 2):
            v2 = v_ref[:, p * LANES:(p + 1) * LANES]
            pvs = [_dot(probs[2 * p + half], jnp.where(halves[half], v2, jnp.zeros_like(v2))) for half in range(2)]
            acc_sc[p] = jnp.where(lo, alphas[2 * p], alphas[2 * p + 1]) * acc_sc[p] + pvs[0] + pvs[1]

    some_hidden = (j + 1) * tk - 1 > past + i * tq
    pl.when(some_hidden)(functools.partial(step, True))
    pl.when(jnp.logical_not(some_hidden))(functools.partial(step, False))

    @pl.when(last_ref[n] == 1)
    def _():
        for p in range(H_A // 2):
            l2 = jnp.where(lo, l_sc[2 * p], l_sc[2 * p + 1])
            o_ref[:, p * LANES:(p + 1) * LANES] = (acc_sc[p] / l2).astype(BF16)


def _fox_attention(q, q_aug, k_all, k_aug, v_all, b, t, past):
    tt = past + t
    tq = _pick_tile(t, 512, 16)
    tk = _pick_tile(tt, 512, 16)
    nq, nk = t // tq, tt // tk
    pairs = [(i, j) for i in range(nq) for j in range(nk) if j * tk <= past + (i + 1) * tq - 1]
    qi = jnp.array([p[0] for p in pairs], I32)
    kj = jnp.array([p[1] for p in pairs], I32)
    last = jnp.array([1 if (idx + 1 == len(pairs) or pairs[idx + 1][0] != p[0]) else 0
                      for idx, p in enumerate(pairs)], I32)
    grid_spec = pltpu.PrefetchScalarGridSpec(
        num_scalar_prefetch=3, grid=(b, len(pairs)),
        in_specs=[
            pl.BlockSpec((tq, WA), lambda bi, n, qi, kj, la: (bi * nq + qi[n], 0)),
            pl.BlockSpec((tq, H_A * LANES), lambda bi, n, qi, kj, la: (bi * nq + qi[n], 0)),
            pl.BlockSpec((tk, WA), lambda bi, n, qi, kj, la: (bi * nk + kj[n], 0)),
            pl.BlockSpec((tk, WA), lambda bi, n, qi, kj, la: (bi * nk + kj[n], 0)),
            pl.BlockSpec((tk, WA), lambda bi, n, qi, kj, la: (bi * nk + kj[n], 0)),
        ],
        out_specs=pl.BlockSpec((tq, WA), lambda bi, n, qi, kj, la: (bi * nq + qi[n], 0)),
        scratch_shapes=[pltpu.VMEM((H_A, tq, 2 * LANES), BF16), pltpu.VMEM((H_A, tq, 1), F32),
                        pltpu.VMEM((H_A, tq, 1), F32), pltpu.VMEM((H_A // 2, tq, LANES), F32)])
    return pl.pallas_call(
        functools.partial(_attn_kernel, tq=tq, tk=tk, past=past), grid_spec=grid_spec,
        out_shape=jax.ShapeDtypeStruct((b * t, WA), BF16),
        compiler_params=_params("arbitrary", "arbitrary"), name="fox_attention")(qi, kj, last, q, q_aug, k_all, k_aug, v_all)


def _gdn_kernel(u_ref, small_ref, smallt_ref, z_ref, s0_ref, conv0_ref, wconv_ref, gon_ref,
                yb_ref, sout_ref, cout_ref, s_sc, ubuf, *, chunk, cps, nsteps):
    c = pl.program_id(1)
    L, C = chunk, cps
    R = L * C
    lg = L.bit_length() - 1
    keep = CONV_W - 1

    @pl.when(c == 0)
    def _():
        s_sc[...] = s0_ref[...]
        ubuf[8 - keep:8, :] = conv0_ref[...]

    ubuf[8:8 + R, :] = u_ref[...]
    w = wconv_ref[...]
    conv = ubuf[8 - keep:8 - keep + R, :] * w[0:1, :]
    for i in range(1, CONV_W):
        conv = conv + ubuf[8 - keep + i:8 - keep + i + R, :] * w[i:i + 1, :]
    act = _silu(conv)
    tail = ubuf[8 + R - keep:8 + R, :]
    ubuf[8 - keep:8, :] = tail

    @pl.when(c == nsteps - 1)
    def _():
        cout_ref[...] = tail

    sm = small_ref[...]
    smt = smallt_ref[...]
    rr, rc = _iota((R, R), 0), _iota((R, R), 1)
    same = jnp.right_shift(rr, lg) == jnp.right_shift(rc, lg)
    g_col = _dot_exact_lhs(jnp.logical_and(same, rr >= rc).astype(BF16), sm, 3)
    qn_all = act[:, :QK_B]
    qn_all = qn_all * lax.rsqrt(_group_sumsq(qn_all, DK_B) + EPS) * (DK_B ** -0.5)
    kn_all = act[:, QK_B:2 * QK_B]
    kn_all = kn_all * lax.rsqrt(_group_sumsq(kn_all, DK_B) + EPS)

    W2 = 2 * L
    lo64 = _iota((1, LANES), 1) < DK_B
    hi64 = jnp.logical_not(lo64)
    lane2 = _iota((1, W2), 1)
    lo_l, hi_l = lane2 < L, lane2 >= L
    ri, ci = _iota((L, W2), 0), jnp.bitwise_and(_iota((L, W2), 1), L - 1)
    incl, strict = ri >= ci, ri > ci
    eye = (ri == ci).astype(F32)
    n_sq = max(int(math.ceil(math.log2(L))) - 1, 0)
    sub128 = _iota((LANES, 1), 0) < DK_B

    def stack_heads(x, lo, hi):
        return jnp.concatenate([jnp.where(lo, x, 0.0), jnp.where(hi, x, 0.0)], axis=0)

    items = [(ch, p) for ch in range(C) for p in range(H_B // 2)]
    g_rows = []
    for ch in range(C):
        tt, jj = _iota((R, W2), 0), jnp.bitwise_and(_iota((R, W2), 1), L - 1)
        sel = jnp.logical_and(jnp.right_shift(tt, lg) == ch, jnp.bitwise_and(tt, L - 1) <= jj)
        g_rows.append(_dot_exact_rhs(smt, sel.astype(BF16), 3))

    st = {}
    for ch, p in items:
        rs = slice(ch * L, (ch + 1) * L)
        ha, hb = 2 * H_B + 2 * p, 2 * H_B + 2 * p + 1
        gca, gcb = g_col[rs, ha:ha + 1], g_col[rs, hb:hb + 1]
        gla, glb = gca[L - 1:L, :], gcb[L - 1:L, :]
        q2, k2 = qn_all[rs, p * LANES:(p + 1) * LANES], kn_all[rs, p * LANES:(p + 1) * LANES]
        beta2 = jnp.where(lo64, sm[rs, H_B + 2 * p:H_B + 2 * p + 1], sm[rs, H_B + 2 * p + 1:H_B + 2 * p + 2])
        eg2 = jnp.exp(jnp.where(lo64, gca, gcb))
        egl2 = jnp.exp(jnp.where(lo64, gla - gca, glb - gcb))
        kb2 = k2 * beta2
        gr = jnp.where(lo_l, g_rows[ch][ha:ha + 1, :], g_rows[ch][hb:hb + 1, :])
        decay = jnp.exp(jnp.where(incl, jnp.where(lo_l, gca, gcb) - gr, NEG))
        ybd = stack_heads(k2, lo64, hi64)
        va = act[rs, 2 * QK_B + 2 * p * DV_B:2 * QK_B + (2 * p + 1) * DV_B] * sm[rs, H_B + 2 * p:H_B + 2 * p + 1]
        vb = act[rs, 2 * QK_B + (2 * p + 1) * DV_B:2 * QK_B + (2 * p + 2) * DV_B] * sm[rs, H_B + 2 * p + 1:H_B + 2 * p + 2]
        st[ch, p] = dict(
            decay=decay, ybd=ybd, kb2=kb2, q2=q2,
            kbg_bd=stack_heads(kb2 * eg2, lo64, hi64), qd_bd=stack_heads(q2 * eg2, lo64, hi64),
            kd_bd=stack_heads(k2 * egl2, lo64, hi64), vst=jnp.concatenate([va, vb], axis=0),
            gl2=jnp.where(sub128, jnp.exp(gla), jnp.exp(glb)))
    for it in items:
        d = st[it]
        d["m"] = jnp.where(strict, _dot_t(d["kb2"], d["ybd"]) * d["decay"], 0.0)
        d["attn"] = jnp.where(incl, _dot_t(d["q2"], d["ybd"]) * d["decay"], 0.0)
    for it in items:
        d = st[it]
        d["t"] = eye - d["m"]
        d["mp"] = d["m"]
    for lvl in range(n_sq):
        for it in items:
            d = st[it]
            d["mp"] = _dot(d["mp"], stack_heads(d["mp"], lo_l, hi_l))
        for it in items:
            d = st[it]
            d["t"] = d["t"] + _dot(d["t"], stack_heads(d["mp"], lo_l, hi_l))
    for it in items:
        d = st[it]
        d["ust"] = _dot(stack_heads(d["t"], lo_l, hi_l), d["vst"])
        d["w2"] = _dot(d["t"], d["kbg_bd"])
    for ch in range(C):
        rs = slice(ch * L, (ch + 1) * L)
        s2s, ass = [], []
        for p in range(H_B // 2):
            d = st[ch, p]
            s2 = s_sc[p * LANES:(p + 1) * LANES, :]
            s2s.append(s2)
            ass.append(_dot(jnp.concatenate([stack_heads(d["w2"], lo64, hi64), d["qd_bd"]], axis=0), s2))
        for p in range(H_B // 2):
            d = st[ch, p]
            v_new = d["ust"] - ass[p][:W2, :]
            o_st = ass[p][W2:, :] + _dot(stack_heads(d["attn"], lo_l, hi_l), v_new)
            s_sc[p * LANES:(p + 1) * LANES, :] = s2s[p] * d["gl2"] + _dot_tl(d["kd_bd"], v_new)
            for half in range(2):
                hd = 2 * p + half
                o = o_st[half * L:(half + 1) * L, :]
                on = o * lax.rsqrt(jnp.mean(o * o, axis=-1, keepdims=True) + EPS) * gon_ref[...]
                zs = z_ref[rs, hd * DV_B:(hd + 1) * DV_B]
                yb_ref[rs, hd * DV_B:(hd + 1) * DV_B] = (on * _silu(zs)).astype(BF16)

    @pl.when(c == nsteps - 1)
    def _():
        sout_ref[...] = s_sc[...]


def _gdn(u, small, smallt, z, s0, conv0, w_conv, g_onorm, b, t):
    clen = GDN_CHUNK if t % GDN_CHUNK == 0 else t
    assert clen & (clen - 1) == 0, clen
    cps = GDN_CHUNKS_PER_STEP if (t // clen) % GDN_CHUNKS_PER_STEP == 0 else 1
    chunk = clen * cps
    nc = t // chunk
    smallt3 = smallt.reshape(smallt.shape[0], b * nc, chunk).transpose(1, 0, 2)
    keep = CONV_W - 1
    return pl.pallas_call(
        functools.partial(_gdn_kernel, chunk=clen, cps=cps, nsteps=nc), grid=(b, nc),
        in_specs=[
            pl.BlockSpec((chunk, CONV_DIM), lambda i, j: (i * nc + j, 0)),
            pl.BlockSpec((chunk, LANES), lambda i, j: (i * nc + j, 0)),
            pl.BlockSpec((None, smallt.shape[0], chunk), lambda i, j: (i * nc + j, 0, 0)),
            pl.BlockSpec((chunk, V_B), lambda i, j: (i * nc + j, 0)),
            pl.BlockSpec((None, H_B * DK_B, DV_B), lambda i, j: (i, 0, 0)),
            pl.BlockSpec((None, keep, CONV_DIM), lambda i, j: (i, 0, 0)),
            pl.BlockSpec((CONV_W, CONV_DIM), lambda i, j: (0, 0)),
            pl.BlockSpec((1, DV_B), lambda i, j: (0, 0)),
        ],
        out_specs=[
            pl.BlockSpec((chunk, V_B), lambda i, j: (i * nc + j, 0)),
            pl.BlockSpec((None, H_B * DK_B, DV_B), lambda i, j: (i, 0, 0)),
            pl.BlockSpec((None, keep, CONV_DIM), lambda i, j: (i, 0, 0)),
        ],
        out_shape=[jax.ShapeDtypeStruct((b * t, V_B), BF16),
                   jax.ShapeDtypeStruct((b, H_B * DK_B, DV_B), F32),
                   jax.ShapeDtypeStruct((b, keep, CONV_DIM), F32)],
        scratch_shapes=[pltpu.VMEM((H_B * DK_B, DV_B), F32), pltpu.VMEM((8 + chunk, CONV_DIM), F32)],
        compiler_params=_params("arbitrary", "arbitrary"), name="gdn_chunked",
    )(u, small, smallt3, z, s0, conv0, w_conv, g_onorm)


def _merge_kernel(x_ref, ya_ref, yb_ref, gate_ref, wa_ref, wb_ref, wo_ref, gf_ref, wr_ref, br_ref,
                  x1_ref, h2_ref, logit_ref):
    br_a = jnp.dot(ya_ref[...], wa_ref[...], preferred_element_type=F32)
    br_b = jnp.dot(yb_ref[...], wb_ref[...], preferred_element_type=F32)
    merged = gate_ref[:, :D_MODEL] * br_a + gate_ref[:, D_MODEL:] * br_b
    x1 = x_ref[...] + _dot(merged, wo_ref[...])
    x1_ref[...] = x1
    h2 = _rms_rows(x1, gf_ref[...])
    h2_ref[...] = h2
    logit_ref[...] = _dot(h2, wr_ref[...]) + br_ref[...]


def _route_kernel(logit_ref, meta_ref, wgt_ref, cnt_ref, run, start, *, block):
    sweep, step = pl.program_id(0), pl.program_id(1)

    @pl.when(jnp.logical_and(sweep == 0, step == 0))
    def _():
        run[...] = jnp.zeros_like(run)
        start[...] = jnp.zeros_like(start)

    @pl.when(jnp.logical_and(sweep == 1, step == 0))
    def _():
        counts = run[...]
        cnt_ref[...] = counts
        padded = jnp.floor((counts + (block - 1)) * (1.0 / block)) * block
        before_e = (_iota((LANES, LANES), 0) < _iota((LANES, LANES), 1)).astype(BF16)
        start[...] = _dot_exact_rhs(jnp.broadcast_to(padded, (8, LANES)), before_e, 3)[0:1, :]
        run[...] = jnp.zeros_like(run)

    x = logit_ref[...]
    tm = x.shape[0]
    lane = _iota((1, LANES), 1)
    lanef = lane.astype(F32)
    big = float(LANES)

    def first_argmax(vals, valid):
        mx = jnp.max(vals, axis=-1, keepdims=True)
        idx = jnp.min(jnp.where(jnp.logical_and(vals == mx, valid), lanef, big), axis=-1, keepdims=True)
        return mx, idx.astype(I32)

    gmask = lane < N_GROUPS
    gmax, gsel = first_argmax(jnp.where(gmask, x, NEG), gmask)
    p_g = 1.0 / jnp.sum(jnp.where(gmask, jnp.exp(x - gmax), 0.0), axis=-1, keepdims=True)
    first = N_GROUPS + EXPERTS_PER_GROUP * gsel
    emask = jnp.logical_and(lane >= first, lane < first + EXPERTS_PER_GROUP)
    ev = jnp.where(emask, x, NEG)
    m1, i1 = first_argmax(ev, emask)
    emask2 = jnp.logical_and(emask, lane != i1)
    m2, i2 = first_argmax(jnp.where(emask2, x, NEG), emask2)
    e21 = jnp.exp(m2 - m1)
    w1 = p_g / (1.0 + e21)
    w2 = p_g * e21 / (1.0 + e21)
    e1, e2 = i1 - N_GROUPS, i2 - N_GROUPS
    oh1, oh2 = lane == e1, lane == e2
    onehot = jnp.logical_or(oh1, oh2).astype(BF16)
    strict = (_iota((tm, tm), 0) > _iota((tm, tm), 1)).astype(BF16)
    slot = jnp.dot(strict, onehot, preferred_element_type=F32) + run[...] + start[...]
    run[...] = run[...] + jnp.sum(onehot.astype(F32), axis=0, keepdims=True)

    @pl.when(sweep == 1)
    def _():
        d1 = jnp.sum(jnp.where(oh1, slot, 0.0), axis=-1, keepdims=True).astype(I32)
        d2 = jnp.sum(jnp.where(oh2, slot, 0.0), axis=-1, keepdims=True).astype(I32)
        l8 = _iota((1, 8), 1)
        meta_ref[...] = jnp.where(l8 == 0, e1, jnp.where(l8 == 1, e2, jnp.where(l8 == 2, d1, jnp.where(l8 == 3, d2, 0))))
        wgt_ref[...] = jnp.where(l8 == 0, w1, jnp.where(l8 == 1, w2, 0.0))


def _row_copy(src, dst, sem, s, d):
    return pltpu.make_async_copy(src.at[pl.ds(s, 1)], dst.at[pl.ds(d, 1)], sem)


DMA_UNROLL = 8


def _dispatch_kernel(dest_ref, h_ref, zero_ref, xs_ref, sem, *, td):
    del zero_ref
    step, nsteps = pl.program_id(0), pl.num_programs(0)
    base = step * td

    def issue(t, carry):
        _row_copy(h_ref, xs_ref, sem, base + t, dest_ref[0, t]).start(priority=0)
        _row_copy(h_ref, xs_ref, sem, base + t, dest_ref[0, td + t]).start(priority=1)
        return carry

    def drain(t, carry):
        _row_copy(h_ref, xs_ref, sem, 0, 0).wait()
        _row_copy(h_ref, xs_ref, sem, 0, 0).wait()
        return carry

    lax.fori_loop(0, td, issue, 0, unroll=DMA_UNROLL)

    @pl.when(step > 0)
    def _():
        lax.fori_loop(0, td, drain, 0, unroll=DMA_UNROLL)

    @pl.when(step == nsteps - 1)
    def _():
        lax.fori_loop(0, td, drain, 0, unroll=DMA_UNROLL)


def _expert_kernel(be_ref, nu_ref, x_ref, w1_ref, w3_ref, w2_ref, y_ref):
    blk = pl.program_id(0)

    @pl.when(blk < nu_ref[0])
    def _():
        x = x_ref[...].astype(BF16)
        a = jnp.dot(x, w1_ref[...], preferred_element_type=F32)
        g = jnp.dot(x, w3_ref[...], preferred_element_type=F32)
        y_ref[...] = _dot(_silu(a) * g, w2_ref[...])

    @pl.when(blk >= nu_ref[0])
    def _():
        y_ref[...] = jnp.zeros_like(y_ref)


def _combine_kernel(dest_ref, next_ref, x1_ref, wgt_ref, yb_ref, o_ref, g0, g1, sems, *, tc):
    step, nsteps = pl.program_id(0), pl.num_programs(0)
    slot = step % 2

    def issue(idx_ref, s):
        def body(t, carry):
            _row_copy(yb_ref, g0.at[s], sems.at[s], idx_ref[0, t], t).start(priority=0)
            _row_copy(yb_ref, g1.at[s], sems.at[s], idx_ref[0, tc + t], t).start(priority=1)
            return carry
        lax.fori_loop(0, tc, body, 0, unroll=DMA_UNROLL)

    @pl.when(step == 0)
    def _():
        issue(dest_ref, 0)

    @pl.when(step + 1 < nsteps)
    def _():
        issue(next_ref, 1 - slot)

    def drain(t, carry):
        _row_copy(yb_ref, g0.at[slot], sems.at[slot], 0, 0).wait()
        _row_copy(yb_ref, g1.at[slot], sems.at[slot], 0, 0).wait()
        return carry

    lax.fori_loop(0, tc, drain, 0, unroll=DMA_UNROLL)
    wg = wgt_ref[...]
    o_ref[...] = x1_ref[...] + (g0[slot] * wg[:, 0:1] + g1[slot] * wg[:, 1:2])


def _hier_moe(x1, h2, logits, w_e1, w_e3, w_e2):
    n = x1.shape[0]
    tr = _pick_tile(n, 512)
    bm = MOE_BLOCK
    assert bm & (bm - 1) == 0
    meta, wgt, cnt = pl.pallas_call(
        functools.partial(_route_kernel, block=bm), grid=(2, n // tr),
        in_specs=[pl.BlockSpec((tr, LANES), lambda s, i: (i, 0))],
        out_specs=[pl.BlockSpec((tr, 8), lambda s, i: (i * s, 0)), pl.BlockSpec((tr, 8), lambda s, i: (i * s, 0)),
                   pl.BlockSpec((1, LANES), lambda s, i: (0, 0))],
        out_shape=[jax.ShapeDtypeStruct((n, 8), I32), jax.ShapeDtypeStruct((n, 8), F32),
                   jax.ShapeDtypeStruct((1, LANES), F32)],
        scratch_shapes=[pltpu.VMEM((1, LANES), F32), pltpu.VMEM((1, LANES), F32)],
        compiler_params=_params("arbitrary", "arbitrary"), name="moe_route")(logits)

    counts = cnt[0, :N_EXPERTS].astype(I32)
    pad_end = jnp.cumsum(((counts + bm - 1) // bm) * bm)
    nblk = -(-(2 * n) // bm) + N_EXPERTS
    n_used = (pad_end[-1] // bm).astype(I32).reshape(1)
    blk_first = jnp.arange(nblk, dtype=I32) * bm
    blk_expert = jnp.minimum(jnp.sum((pad_end[None, :] <= blk_first[:, None]).astype(I32), axis=1), N_EXPERTS - 1)

    td = _pick_tile(n, 256)
    dest = meta[:, 2:4].reshape(n // td, td, 2).transpose(0, 2, 1).reshape(n // td, 1, 2 * td)
    xs = pl.pallas_call(
        functools.partial(_dispatch_kernel, td=td), grid=(n // td,),
        in_specs=[pl.BlockSpec((None, 1, 2 * td), lambda i: (i, 0, 0), memory_space=pltpu.SMEM),
                  pl.BlockSpec(memory_space=pl.ANY),
                  pl.BlockSpec(memory_space=pl.ANY)],
        out_specs=pl.BlockSpec(memory_space=pl.ANY),
        out_shape=jax.ShapeDtypeStruct((nblk * bm, D_MODEL), F32),
        scratch_shapes=[pltpu.SemaphoreType.DMA(())],
        input_output_aliases={2: 0},
        compiler_params=_params("arbitrary"), name="moe_dispatch")(dest, h2, jnp.zeros((nblk * bm, D_MODEL), F32))

    grid_spec = pltpu.PrefetchScalarGridSpec(
        num_scalar_prefetch=2, grid=(nblk,),
        in_specs=[pl.BlockSpec((bm, D_MODEL), lambda i, be, nu: (i, 0)),
                  pl.BlockSpec((None, D_MODEL, D_EXPERT), lambda i, be, nu: (be[i], 0, 0)),
                  pl.BlockSpec((None, D_MODEL, D_EXPERT), lambda i, be, nu: (be[i], 0, 0)),
                  pl.BlockSpec((None, D_EXPERT, D_MODEL), lambda i, be, nu: (be[i], 0, 0))],
        out_specs=pl.BlockSpec((bm, D_MODEL), lambda i, be, nu: (i, 0)))
    yb = pl.pallas_call(
        _expert_kernel, grid_spec=grid_spec, out_shape=jax.ShapeDtypeStruct((nblk * bm, D_MODEL), F32),
        compiler_params=_params("arbitrary"), name="moe_experts")(blk_expert, n_used, xs, w_e1, w_e3, w_e2)

    tc = td
    return pl.pallas_call(
        functools.partial(_combine_kernel, tc=tc), grid=(n // tc,),
        in_specs=[pl.BlockSpec((None, 1, 2 * tc), lambda i: (i, 0, 0), memory_space=pltpu.SMEM),
                  pl.BlockSpec((None, 1, 2 * tc), lambda i: (jnp.minimum(i + 1, n // tc - 1), 0, 0),
                               memory_space=pltpu.SMEM),
                  pl.BlockSpec((tc, D_MODEL), lambda i: (i, 0)),
                  pl.BlockSpec((tc, 8), lambda i: (i, 0)),
                  pl.BlockSpec(memory_space=pl.ANY)],
        out_specs=pl.BlockSpec((tc, D_MODEL), lambda i: (i, 0)),
        out_shape=jax.ShapeDtypeStruct((n, D_MODEL), F32),
        scratch_shapes=[pltpu.VMEM((2, tc, D_MODEL), F32), pltpu.VMEM((2, tc, D_MODEL), F32),
                        pltpu.SemaphoreType.DMA((2,))],
        compiler_params=_params("arbitrary"), name="moe_combine")(dest, dest, x1, wgt, yb)


def _prep_weights(l, g_mix, w_in, b_f, g_qnorm, g_knorm, w_conv, a_log, dt_bias, g_onorm, b_gate,
                  w_br_a, w_br_b, w_out, g_ffn, w_grp, b_grp, w_exp, b_exp, w_e1, w_e3, w_e2):
    w = w_in[l]
    offs = [0]
    for s in (WA, WA, WA, H_A, QK_B, QK_B, V_B, H_B, H_B, V_B, D_MODEL, D_MODEL):
        offs.append(offs[-1] + s)
    col = lambda i, j=None: w[:, offs[i]:offs[(i if j is None else j) + 1]]
    n_small = H_A + 2 * H_B
    w_small = jnp.concatenate([col(3), col(7), col(8)], axis=1)
    zeros8 = jnp.zeros((H_A,), F32)
    bias = jnp.concatenate([b_f[l], zeros8, dt_bias[l]])
    alog = jnp.concatenate([zeros8, zeros8, a_log[l]])
    pad = lambda v, n: jnp.pad(v, (0, n - v.shape[0]))
    rows_t = 32
    return dict(
        g_mix=g_mix[l][None, :],
        wq=col(0).astype(BF16), wk=col(1).astype(BF16), wv=col(2).astype(BF16),
        gq=jnp.tile(g_qnorm[l], H_A)[None, :], gk=jnp.tile(g_knorm[l], H_A)[None, :],
        ws=jnp.pad(w_small, ((0, 0), (0, LANES - n_small))).astype(BF16),
        wst=jnp.pad(w_small.T, ((0, rows_t - n_small), (0, 0))).astype(BF16),
        wu=col(4, 6).astype(BF16),
        bias=pad(bias, LANES)[None, :], alog=pad(alog, LANES)[None, :],
        biast=pad(bias, rows_t)[:, None], alogt=pad(alog, rows_t)[:, None],
        wz=col(9).astype(BF16), wg=col(10, 11).astype(BF16), bg=b_gate[l][None, :],
        w_conv=w_conv[l], g_onorm=g_onorm[l][None, :],
        w_br_a=w_br_a[l].astype(BF16), w_br_b=w_br_b[l].astype(BF16), w_out=w_out[l].astype(BF16),
        g_ffn=g_ffn[l][None, :],
        wr=jnp.pad(jnp.concatenate([w_grp[l], w_exp[l]], axis=1), ((0, 0), (0, LANES - N_GROUPS - N_EXPERTS))).astype(BF16),
        br=pad(jnp.concatenate([b_grp[l], b_exp[l]]), LANES)[None, :],
        w_e1=w_e1[l].astype(BF16), w_e3=w_e3[l].astype(BF16), w_e2=w_e2[l].astype(BF16),
    )


def _layer(x, past_k, past_v, past_logf, s0, conv0, p):
    b, t, _ = x.shape
    past = past_k.shape[1]
    n = b * t
    xf = x.reshape(n, D_MODEL)
    tm = _pick_tile(n, 512)

    q16, k32, k16, v32, v16 = _row_call(
        _inproj_fox_kernel, n, tm, [xf], [p["g_mix"], p["wq"], p["wk"], p["wv"], p["gq"], p["gk"]],
        [((n, WA), BF16, False), ((n, WA), F32, False), ((n, WA), BF16, False), ((n, WA), F32, False),
         ((n, WA), BF16, False)], "inproj_fox")
    small, smallt, u = _row_call(
        _inproj_gdn_kernel, n, tm, [xf],
        [p["g_mix"], p["ws"], p["wst"], p["wu"], p["bias"], p["alog"], p["biast"], p["alogt"]],
        [((n, LANES), F32, False), ((32, n), F32, True), ((n, CONV_DIM), F32, False)], "inproj_gdn")
    z, gates = _row_call(
        _inproj_gate_kernel, n, tm, [xf], [p["g_mix"], p["wz"], p["wg"], p["bg"]],
        [((n, V_B), F32, False), ((n, 2 * D_MODEL), F32, False)], "inproj_gate")

    logf = small[:, :H_A].reshape(b, t, H_A)
    k_aug, q_aug = _forget_companions(jnp.concatenate([past_logf.astype(F32), logf], axis=1))
    k_all = jnp.concatenate([past_k.reshape(b, past, WA).astype(BF16), k16.reshape(b, t, WA)], axis=1)
    v_all = jnp.concatenate([past_v.reshape(b, past, WA).astype(BF16), v16.reshape(b, t, WA)], axis=1)
    y_a = _fox_attention(q16, q_aug[:, past:].reshape(n, H_A * LANES), k_all.reshape(b * (past + t), WA),
                         k_aug.reshape(b * (past + t), WA), v_all.reshape(b * (past + t), WA), b, t, past)

    y_b, s_new, conv_new = _gdn(u, small, smallt, z, s0.reshape(b, H_B * DK_B, DV_B), conv0,
                                p["w_conv"], p["g_onorm"], b, t)

    x1, h2, logits = _row_call(
        _merge_kernel, n, tm, [xf, y_a, y_b, gates],
        [p["w_br_a"], p["w_br_b"], p["w_out"], p["g_ffn"], p["wr"], p["br"]],
        [((n, D_MODEL), F32, False), ((n, D_MODEL), F32, False), ((n, LANES), F32, False)], "merge_outproj")
    x2 = _hier_moe(x1, h2, logits, p["w_e1"], p["w_e3"], p["w_e2"])
    return (x2.reshape(b, t, D_MODEL), k32.reshape(b, t, H_A, DH_A), v32.reshape(b, t, H_A, DH_A), logf,
            s_new.reshape(b, H_B, DK_B, DV_B), conv_new)


def kernel(x_prompt, x_sample, cache_fox_k, cache_fox_v, cache_fox_logf, state_gdn, state_gdn_conv, g_mix, w_in, b_f, g_qnorm, g_knorm, w_conv, a_log, dt_bias, g_onorm, b_gate, w_br_a, w_br_b, w_out, g_ffn, w_grp, b_grp, w_exp, b_exp, w_e1, w_e3, w_e2):
    depth = w_in.shape[0]
    bp = x_prompt.shape[0]
    yp, ys = x_prompt, x_sample
    outs_p, outs_s = [], []
    for l in range(depth):
        p = _prep_weights(l, g_mix, w_in, b_f, g_qnorm, g_knorm, w_conv, a_log, dt_bias, g_onorm, b_gate,
                          w_br_a, w_br_b, w_out, g_ffn, w_grp, b_grp, w_exp, b_exp, w_e1, w_e3, w_e2)
        ys, *rest_s = _layer(ys, cache_fox_k[l], cache_fox_v[l], cache_fox_logf[l], state_gdn[l], state_gdn_conv[l], p)
        yp, *rest_p = _layer(
            yp, jnp.zeros((bp, 0, H_A, DH_A), F32), jnp.zeros((bp, 0, H_A, DH_A), F32), jnp.zeros((bp, 0, H_A), F32),
            jnp.zeros((bp, H_B, DK_B, DV_B), F32), jnp.zeros((bp, CONV_W - 1, CONV_DIM), F32), p)
        outs_p.append(rest_p)
        outs_s.append(rest_s)
    stack = lambda outs, i: jnp.stack([o[i] for o in outs])
    return (yp, ys,
            stack(outs_p, 0), stack(outs_p, 1), stack(outs_p, 2), stack(outs_p, 3), stack(outs_p, 4),
            stack(outs_s, 0), stack(outs_s, 1), stack(outs_s, 2), stack(outs_s, 3), stack(outs_s, 4))
```

```python
import functools
import math

import jax
import jax.numpy as jnp
import numpy as np
from jax import lax
from jax.experimental import pallas as pl
from jax.experimental.pallas import tpu as pltpu

F32, BF16, I32, U32 = jnp.float32, jnp.bfloat16, jnp.int32, jnp.uint32

D_MODEL = 1024
H_A, DH_A = 8, 64
WA = H_A * DH_A
H_B, DK_B, DV_B = 8, 64, 128
QK_B, V_B = H_B * DK_B, H_B * DV_B
CONV_W = 4
CONV_DIM = 2 * QK_B + V_B
N_GROUPS, EXPERTS_PER_GROUP = 4, 8
N_EXPERTS = N_GROUPS * EXPERTS_PER_GROUP
D_EXPERT = 512
GDN_CHUNK = 64
GDN_CHUNKS_PER_STEP = 4
EPS = 1e-6
LOG2E = math.log2(math.e)
NEG = -1e30
LANES = 128
V7X_VMEM_LIMIT = 56 * 1024 * 1024
MOE_BLOCK = 256


def _params(*sem):
    return pltpu.CompilerParams(dimension_semantics=sem, vmem_limit_bytes=V7X_VMEM_LIMIT)


def _pick_tile(n, cap, mult=8):
    if n <= cap:
        return n
    for t in range(cap - cap % mult, 0, -mult):
        if n % t == 0:
            return t
    return n


def _dot(a, b):
    return jnp.dot(a.astype(BF16), b.astype(BF16), preferred_element_type=F32)


def _dot_t(a, b):
    return lax.dot_general(a.astype(BF16), b.astype(BF16), (((1,), (1,)), ((), ())), preferred_element_type=F32)


def _dot_tl(a, b):
    return lax.dot_general(a.astype(BF16), b.astype(BF16), (((0,), (0,)), ((), ())), preferred_element_type=F32)


def _split(x, n):
    parts = []
    for _ in range(n):
        p = x.astype(BF16)
        parts.append(p)
        x = x - p.astype(F32)
    return parts


def _dot_exact_lhs(ones, x, n):
    acc = None
    for p in _split(x, n):
        t = jnp.dot(ones, p, preferred_element_type=F32)
        acc = t if acc is None else acc + t
    return acc


def _dot_exact_rhs(x, ones, n):
    acc = None
    for p in _split(x, n):
        t = jnp.dot(p, ones, preferred_element_type=F32)
        acc = t if acc is None else acc + t
    return acc


def _iota(shape, dim):
    return lax.broadcasted_iota(I32, shape, dim)


def _softplus(x):
    return jnp.maximum(x, 0.0) + jnp.log1p(jnp.exp(-jnp.abs(x)))


def _silu(x):
    return x * jax.nn.sigmoid(x)


def _rms_rows(x, g):
    return x * lax.rsqrt(jnp.mean(x * x, axis=-1, keepdims=True) + EPS) * g


def _head_ones(width, head):
    shift = head.bit_length() - 1
    assert head == 1 << shift
    r = jnp.right_shift(_iota((width, width), 0), shift)
    c = jnp.right_shift(_iota((width, width), 1), shift)
    return (r == c).astype(BF16)


def _group_sumsq(y, head):
    ones = _head_ones(2 * LANES, head)
    sq = y * y
    parts = [_dot_exact_rhs(sq[:, c:c + 2 * LANES], ones, 2) for c in range(0, y.shape[1], 2 * LANES)]
    return parts[0] if len(parts) == 1 else jnp.concatenate(parts, axis=1)


def _inproj_fox_kernel(x_ref, g_ref, wq_ref, wk_ref, wv_ref, gq_ref, gk_ref,
                       q_ref, k32_ref, k16_ref, v32_ref, v16_ref):
    h = _rms_rows(x_ref[...], g_ref[...]).astype(BF16)

    def headnorm(y, gain):
        ms = _group_sumsq(y, DH_A) * (1.0 / DH_A)
        return y * lax.rsqrt(ms + EPS) * gain

    q = headnorm(jnp.dot(h, wq_ref[...], preferred_element_type=F32), gq_ref[...])
    q_ref[...] = (q * (DH_A ** -0.5 * LOG2E)).astype(BF16)
    k = headnorm(jnp.dot(h, wk_ref[...], preferred_element_type=F32), gk_ref[...])
    k32_ref[...] = k
    k16_ref[...] = k.astype(BF16)
    v = jnp.dot(h, wv_ref[...], preferred_element_type=F32)
    v32_ref[...] = v
    v16_ref[...] = v.astype(BF16)


def _small_epilogue(y, bias, a_log, idx):
    yb = y + bias
    logf = -_softplus(-yb)
    beta = jax.nn.sigmoid(y)
    g = -jnp.exp(a_log) * _softplus(yb)
    return jnp.where(idx < H_A, logf, jnp.where(idx < H_A + H_B, beta, g))


def _inproj_gdn_kernel(x_ref, g_ref, ws_ref, wst_ref, wu_ref, bias_ref, alog_ref, biast_ref, alogt_ref,
                       small_ref, smallt_ref, u_ref):
    h = _rms_rows(x_ref[...], g_ref[...]).astype(BF16)
    ys = jnp.dot(h, ws_ref[...], preferred_element_type=F32)
    small_ref[...] = _small_epilogue(ys, bias_ref[...], alog_ref[...], _iota(ys.shape, 1))
    yt = lax.dot_general(wst_ref[...], h, (((1,), (1,)), ((), ())), preferred_element_type=F32)
    smallt_ref[...] = _small_epilogue(yt, biast_ref[...], alogt_ref[...], _iota(yt.shape, 0))
    for c in range(0, CONV_DIM, 512):
        u_ref[:, c:c + 512] = jnp.dot(h, wu_ref[:, c:c + 512], preferred_element_type=F32)


def _inproj_gate_kernel(x_ref, g_ref, wz_ref, wg_ref, bg_ref, z_ref, gate_ref):
    h = _rms_rows(x_ref[...], g_ref[...]).astype(BF16)
    for c in range(0, V_B, 512):
        z_ref[:, c:c + 512] = jnp.dot(h, wz_ref[:, c:c + 512], preferred_element_type=F32)
    for c in range(0, 2 * D_MODEL, 512):
        y = jnp.dot(h, wg_ref[:, c:c + 512], preferred_element_type=F32)
        gate_ref[:, c:c + 512] = jax.nn.sigmoid(y + bg_ref[:, c:c + 512])


def _row_call(kernel, n, tm, row_inputs, full_inputs, outs, name):
    in_specs = [pl.BlockSpec((tm, a.shape[1]), lambda i: (i, 0)) for a in row_inputs]
    in_specs += [pl.BlockSpec(a.shape, lambda i, nd=a.ndim: (0,) * nd) for a in full_inputs]
    out_specs, out_shapes = [], []
    for shape, dtype, transposed in outs:
        if transposed:
            out_specs.append(pl.BlockSpec((shape[0], tm), lambda i: (0, i)))
        else:
            out_specs.append(pl.BlockSpec((tm, shape[1]), lambda i: (i, 0)))
        out_shapes.append(jax.ShapeDtypeStruct(shape, dtype))
    return pl.pallas_call(kernel, grid=(n // tm,), in_specs=in_specs, out_specs=out_specs, out_shape=out_shapes,
                          compiler_params=_params("arbitrary"), name=name)(*row_inputs, *full_inputs)


AUG_PIECES = 3
AUG_GROUP = 16


def _companion_placement():
    place = np.zeros((AUG_PIECES, LANES, 3 * LANES), np.float32)
    ones = np.zeros((1, 3 * LANES), np.float32)
    n = AUG_PIECES
    for hd in range(H_A):
        pair, odd = divmod(hd, 2)
        g = AUG_GROUP * pair
        for k in range(n):
            place[k, hd, g + n * odd + k] = -1.0
            place[k, hd, (1 + odd) * LANES + g + 2 * n + k] = 1.0
            ones[0, g + 2 * n + k] = 1.0
            ones[0, (1 + odd) * LANES + g + n * odd + k] = 1.0
    return jnp.asarray(place, BF16), jnp.asarray(ones, F32)


def _cumsum_kernel(x_ref, place_ref, ones_ref, kaug_ref, qaug_ref, carry):
    @pl.when(pl.program_id(1) == 0)
    def _():
        carry[...] = jnp.zeros_like(carry)

    x = x_ref[...]
    t = x.shape[0]
    tril = (_iota((t, t), 0) >= _iota((t, t), 1)).astype(BF16)
    c = _dot_exact_lhs(tril, x, 3) + carry[...]
    carry[...] = c[t - 1:t, :]
    slab = ones_ref[...]
    for k, piece in enumerate(_split(c * LOG2E, AUG_PIECES)):
        slab = slab + jnp.dot(piece, place_ref[k], preferred_element_type=F32)
    kaug_ref[...] = slab[:, :LANES].astype(BF16)
    qaug_ref[...] = slab[:, LANES:].astype(BF16)


def _forget_companions(x):
    b, tt, c = x.shape
    tc = _pick_tile(tt, 512, 16)
    place, ones = _companion_placement()
    return pl.pallas_call(
        _cumsum_kernel, grid=(b, tt // tc),
        in_specs=[pl.BlockSpec((None, tc, c), lambda i, j: (i, j, 0)),
                  pl.BlockSpec(place.shape, lambda i, j: (0, 0, 0)), pl.BlockSpec(ones.shape, lambda i, j: (0, 0))],
        out_specs=[pl.BlockSpec((None, tc, LANES), lambda i, j: (i, j, 0)),
                   pl.BlockSpec((None, tc, 2 * LANES), lambda i, j: (i, j, 0))],
        out_shape=[jax.ShapeDtypeStruct((b, tt, LANES), BF16), jax.ShapeDtypeStruct((b, tt, 2 * LANES), BF16)],
        scratch_shapes=[pltpu.VMEM((1, c), F32)],
        compiler_params=_params("arbitrary", "arbitrary"), name="fox_cumsum")(x, place, ones)


def _attn_kernel(qi_ref, kj_ref, last_ref, q_ref, qa_ref, k_ref, ka_ref, v_ref, o_ref, qq_sc, m_sc, l_sc, acc_sc,
                 *, tq, tk, past):
    n = pl.program_id(1)
    i, j = qi_ref[n], kj_ref[n]
    lo = _iota((1, LANES), 1) < DH_A
    halves = (lo, jnp.logical_not(lo))

    @pl.when(j == 0)
    def _():
        m_sc[...] = jnp.full_like(m_sc, NEG)
        l_sc[...] = jnp.zeros_like(l_sc)
        acc_sc[...] = jnp.zeros_like(acc_sc)
        lane = _iota((1, LANES), 1)
        for hd in range(H_A):
            pair, odd = divmod(hd, 2)
            q2 = q_ref[:, pair * LANES:(pair + 1) * LANES]
            qq_sc[hd, :, :LANES] = jnp.where(halves[odd], q2, jnp.zeros_like(q2))
            comp = qa_ref[:, odd * LANES:(odd + 1) * LANES]
            own = jnp.logical_and(lane >= AUG_GROUP * pair, lane < AUG_GROUP * (pair + 1))
            qq_sc[hd, :, LANES:] = jnp.where(own, comp, jnp.zeros_like(comp))

    def step(masked):
        if masked:
            visible = j * tk + _iota((1, tk), 1) <= past + i * tq + _iota((tq, 1), 0)
        scores = []
        for p in range(H_A // 2):
            sl = slice(p * LANES, (p + 1) * LANES)
            kk = jnp.concatenate([k_ref[:, sl], ka_ref[...]], axis=1)
            scores += [_dot_t(qq_sc[2 * p + half], kk) for half in range(2)]
        probs, alphas = [], []
        for hd in range(H_A):
            s = jnp.where(visible, scores[hd], NEG) if masked else scores[hd]
            m_prev = m_sc[hd]
            m_new = jnp.maximum(m_prev, jnp.max(s, axis=-1, keepdims=True))
            alpha = jnp.exp2(m_prev - m_new)
            pr = jnp.exp2(s - m_new)
            l_sc[hd] = alpha * l_sc[hd] + jnp.sum(pr, axis=-1, keepdims=True)
            m_sc[hd] = m_new
            alphas.append(alpha)
            probs.append(pr.astype(BF16))
        for p in range(H_A // 2):
            v2 = v_ref[:, p * LANES:(p + 1) * LANES]
            pvs = [_dot(probs[2 * p + half], jnp.where(halves[half], v2, jnp.zeros_like(v2))) for half in range(2)]
            acc_sc[p] = jnp.where(lo, alphas[2 * p], alphas[2 * p + 1]) * acc_sc[p] + pvs[0] + pvs[1]

    some_hidden = (j + 1) * tk - 1 > past + i * tq
    pl.when(some_hidden)(functools.partial(step, True))
    pl.when(jnp.logical_not(some_hidden))(functools.partial(step, False))

    @pl.when(last_ref[n] == 1)
    def _():
        for p in range(H_A // 2):
            l2 = jnp.where(lo, l_sc[2 * p], l_sc[2 * p + 1])
            o_ref[:, p * LANES:(p + 1) * LANES] = (acc_sc[p] / l2).astype(BF16)


def _fox_attention(q, q_aug, k_all, k_aug, v_all, b, t, past):
    tt = past + t
    tq = _pick_tile(t, 512, 16)
    tk = _pick_tile(tt, 512, 16)
    nq, nk = t // tq, tt // tk
    pairs = [(i, j) for i in range(nq) for j in range(nk) if j * tk <= past + (i + 1) * tq - 1]
    qi = jnp.array([p[0] for p in pairs], I32)
    kj = jnp.array([p[1] for p in pairs], I32)
    last = jnp.array([1 if (idx + 1 == len(pairs) or pairs[idx + 1][0] != p[0]) else 0
                      for idx, p in enumerate(pairs)], I32)
    grid_spec = pltpu.PrefetchScalarGridSpec(
        num_scalar_prefetch=3, grid=(b, len(pairs)),
        in_specs=[
            pl.BlockSpec((tq, WA), lambda bi, n, qi, kj, la: (bi * nq + qi[n], 0)),
            pl.BlockSpec((tq, 2 * LANES), lambda bi, n, qi, kj, la: (bi * nq + qi[n], 0)),
            pl.BlockSpec((tk, WA), lambda bi, n, qi, kj, la: (bi * nk + kj[n], 0)),
            pl.BlockSpec((tk, LANES), lambda bi, n, qi, kj, la: (bi * nk + kj[n], 0)),
            pl.BlockSpec((tk, WA), lambda bi, n, qi, kj, la: (bi * nk + kj[n], 0)),
        ],
        out_specs=pl.BlockSpec((tq, WA), lambda bi, n, qi, kj, la: (bi * nq + qi[n], 0)),
        scratch_shapes=[pltpu.VMEM((H_A, tq, 2 * LANES), BF16), pltpu.VMEM((H_A, tq, 1), F32),
                        pltpu.VMEM((H_A, tq, 1), F32), pltpu.VMEM((H_A // 2, tq, LANES), F32)])
    return pl.pallas_call(
        functools.partial(_attn_kernel, tq=tq, tk=tk, past=past), grid_spec=grid_spec,
        out_shape=jax.ShapeDtypeStruct((b * t, WA), BF16),
        compiler_params=_params("arbitrary", "arbitrary"), name="fox_attention")(qi, kj, last, q, q_aug, k_all, k_aug, v_all)


def _gdn_kernel(u_ref, small_ref, smallt_ref, z_ref, s0_ref, conv0_ref, wconv_ref, gon_ref,
                yb_ref, sout_ref, cout_ref, s_sc, ubuf, *, chunk, cps, nsteps):
    c = pl.program_id(1)
    L, C = chunk, cps
    R = L * C
    lg = L.bit_length() - 1
    keep = CONV_W - 1

    @pl.when(c == 0)
    def _():
        s_sc[...] = s0_ref[...]
        ubuf[8 - keep:8, :] = conv0_ref[...]

    ubuf[8:8 + R, :] = u_ref[...]
    w = wconv_ref[...]
    conv = ubuf[8 - keep:8 - keep + R, :] * w[0:1, :]
    for i in range(1, CONV_W):
        conv = conv + ubuf[8 - keep + i:8 - keep + i + R, :] * w[i:i + 1, :]
    act = _silu(conv)
    tail = ubuf[8 + R - keep:8 + R, :]
    ubuf[8 - keep:8, :] = tail

    @pl.when(c == nsteps - 1)
    def _():
        cout_ref[...] = tail

    sm = small_ref[...]
    smt = smallt_ref[...]
    rr, rc = _iota((R, R), 0), _iota((R, R), 1)
    same = jnp.right_shift(rr, lg) == jnp.right_shift(rc, lg)
    g_col = _dot_exact_lhs(jnp.logical_and(same, rr >= rc).astype(BF16), sm, 3)
    qn_all = act[:, :QK_B]
    qn_all = qn_all * lax.rsqrt(_group_sumsq(qn_all, DK_B) + EPS) * (DK_B ** -0.5)
    kn_all = act[:, QK_B:2 * QK_B]
    kn_all = kn_all * lax.rsqrt(_group_sumsq(kn_all, DK_B) + EPS)

    W2 = 2 * L
    lo64 = _iota((1, LANES), 1) < DK_B
    hi64 = jnp.logical_not(lo64)
    lane2 = _iota((1, W2), 1)
    lo_l, hi_l = lane2 < L, lane2 >= L
    ri, ci = _iota((L, W2), 0), jnp.bitwise_and(_iota((L, W2), 1), L - 1)
    incl, strict = ri >= ci, ri > ci
    eye = (ri == ci).astype(F32)
    n_sq = max(int(math.ceil(math.log2(L))) - 1, 0)
    sub128 = _iota((LANES, 1), 0) < DK_B

    def stack_heads(x, lo, hi):
        return jnp.concatenate([jnp.where(lo, x, 0.0), jnp.where(hi, x, 0.0)], axis=0)

    items = [(ch, p) for ch in range(C) for p in range(H_B // 2)]
    g_rows = []
    for ch in range(C):
        tt, jj = _iota((R, W2), 0), jnp.bitwise_and(_iota((R, W2), 1), L - 1)
        sel = jnp.logical_and(jnp.right_shift(tt, lg) == ch, jnp.bitwise_and(tt, L - 1) <= jj)
        g_rows.append(_dot_exact_rhs(smt, sel.astype(BF16), 3))

    st = {}
    for ch, p in items:
        rs = slice(ch * L, (ch + 1) * L)
        ha, hb = 2 * H_B + 2 * p, 2 * H_B + 2 * p + 1
        gca, gcb = g_col[rs, ha:ha + 1], g_col[rs, hb:hb + 1]
        gla, glb = gca[L - 1:L, :], gcb[L - 1:L, :]
        q2, k2 = qn_all[rs, p * LANES:(p + 1) * LANES], kn_all[rs, p * LANES:(p + 1) * LANES]
        beta2 = jnp.where(lo64, sm[rs, H_B + 2 * p:H_B + 2 * p + 1], sm[rs, H_B + 2 * p + 1:H_B + 2 * p + 2])
        eg2 = jnp.exp(jnp.where(lo64, gca, gcb))
        egl2 = jnp.exp(jnp.where(lo64, gla - gca, glb - gcb))
        kb2 = k2 * beta2
        gr = jnp.where(lo_l, g_rows[ch][ha:ha + 1, :], g_rows[ch][hb:hb + 1, :])
        decay = jnp.exp(jnp.where(incl, jnp.where(lo_l, gca, gcb) - gr, NEG))
        ybd = stack_heads(k2, lo64, hi64)
        va = act[rs, 2 * QK_B + 2 * p * DV_B:2 * QK_B + (2 * p + 1) * DV_B] * sm[rs, H_B + 2 * p:H_B + 2 * p + 1]
        vb = act[rs, 2 * QK_B + (2 * p + 1) * DV_B:2 * QK_B + (2 * p + 2) * DV_B] * sm[rs, H_B + 2 * p + 1:H_B + 2 * p + 2]
        st[ch, p] = dict(
            decay=decay, ybd=ybd, kb2=kb2, q2=q2,
            kbg_bd=stack_heads(kb2 * eg2, lo64, hi64), qd_bd=stack_heads(q2 * eg2, lo64, hi64),
            kd_bd=stack_heads(k2 * egl2, lo64, hi64), vst=jnp.concatenate([va, vb], axis=0),
            gl2=jnp.where(sub128, jnp.exp(gla), jnp.exp(glb)))
    for it in items:
        d = st[it]
        d["m"] = jnp.where(strict, _dot_t(d["kb2"], d["ybd"]) * d["decay"], 0.0)
        d["attn"] = jnp.where(incl, _dot_t(d["q2"], d["ybd"]) * d["decay"], 0.0)
    for it in items:
        d = st[it]
        d["t"] = eye - d["m"]
        d["mp"] = d["m"]
    for lvl in range(n_sq):
        for it in items:
            d = st[it]
            d["mp"] = _dot(d["mp"], stack_heads(d["mp"], lo_l, hi_l))
        for it in items:
            d = st[it]
            d["t"] = d["t"] + _dot(d["t"], stack_heads(d["mp"], lo_l, hi_l))
    for it in items:
        d = st[it]
        d["ust"] = _dot(stack_heads(d["t"], lo_l, hi_l), d["vst"])
        d["w2"] = _dot(d["t"], d["kbg_bd"])
    for ch in range(C):
        rs = slice(ch * L, (ch + 1) * L)
        s2s, ass = [], []
        for p in range(H_B // 2):
            d = st[ch, p]
            s2 = s_sc[p * LANES:(p + 1) * LANES, :]
            s2s.append(s2)
            ass.append(_dot(jnp.concatenate([stack_heads(d["w2"], lo64, hi64), d["qd_bd"]], axis=0), s2))
        for p in range(H_B // 2):
            d = st[ch, p]
            v_new = d["ust"] - ass[p][:W2, :]
            o_st = ass[p][W2:, :] + _dot(stack_heads(d["attn"], lo_l, hi_l), v_new)
            s_sc[p * LANES:(p + 1) * LANES, :] = s2s[p] * d["gl2"] + _dot_tl(d["kd_bd"], v_new)
            for half in range(2):
                hd = 2 * p + half
                o = o_st[half * L:(half + 1) * L, :]
                on = o * lax.rsqrt(jnp.mean(o * o, axis=-1, keepdims=True) + EPS) * gon_ref[...]
                zs = z_ref[rs, hd * DV_B:(hd + 1) * DV_B]
                yb_ref[rs, hd * DV_B:(hd + 1) * DV_B] = (on * _silu(zs)).astype(BF16)

    @pl.when(c == nsteps - 1)
    def _():
        sout_ref[...] = s_sc[...]


def _gdn(u, small, smallt, z, s0, conv0, w_conv, g_onorm, b, t):
    clen = GDN_CHUNK if t % GDN_CHUNK == 0 else t
    assert clen & (clen - 1) == 0, clen
    cps = GDN_CHUNKS_PER_STEP if (t // clen) % GDN_CHUNKS_PER_STEP == 0 else 1
    chunk = clen * cps
    nc = t // chunk
    smallt3 = smallt.reshape(smallt.shape[0], b * nc, chunk).transpose(1, 0, 2)
    keep = CONV_W - 1
    return pl.pallas_call(
        functools.partial(_gdn_kernel, chunk=clen, cps=cps, nsteps=nc), grid=(b, nc),
        in_specs=[
            pl.BlockSpec((chunk, CONV_DIM), lambda i, j: (i * nc + j, 0)),
            pl.BlockSpec((chunk, LANES), lambda i, j: (i * nc + j, 0)),
            pl.BlockSpec((None, smallt.shape[0], chunk), lambda i, j: (i * nc + j, 0, 0)),
            pl.BlockSpec((chunk, V_B), lambda i, j: (i * nc + j, 0)),
            pl.BlockSpec((None, H_B * DK_B, DV_B), lambda i, j: (i, 0, 0)),
            pl.BlockSpec((None, keep, CONV_DIM), lambda i, j: (i, 0, 0)),
            pl.BlockSpec((CONV_W, CONV_DIM), lambda i, j: (0, 0)),
            pl.BlockSpec((1, DV_B), lambda i, j: (0, 0)),
        ],
        out_specs=[
            pl.BlockSpec((chunk, V_B), lambda i, j: (i * nc + j, 0)),
            pl.BlockSpec((None, H_B * DK_B, DV_B), lambda i, j: (i, 0, 0)),
            pl.BlockSpec((None, keep, CONV_DIM), lambda i, j: (i, 0, 0)),
        ],
        out_shape=[jax.ShapeDtypeStruct((b * t, V_B), BF16),
                   jax.ShapeDtypeStruct((b, H_B * DK_B, DV_B), F32),
                   jax.ShapeDtypeStruct((b, keep, CONV_DIM), F32)],
        scratch_shapes=[pltpu.VMEM((H_B * DK_B, DV_B), F32), pltpu.VMEM((8 + chunk, CONV_DIM), F32)],
        compiler_params=_params("arbitrary", "arbitrary"), name="gdn_chunked",
    )(u, small, smallt3, z, s0, conv0, w_conv, g_onorm)


def _merge_kernel(x_ref, ya_ref, yb_ref, gate_ref, wa_ref, wb_ref, wo_ref, gf_ref, wr_ref, br_ref,
                  x1_ref, h2_ref, logit_ref):
    br_a = jnp.dot(ya_ref[...], wa_ref[...], preferred_element_type=F32)
    br_b = jnp.dot(yb_ref[...], wb_ref[...], preferred_element_type=F32)
    merged = gate_ref[:, :D_MODEL] * br_a + gate_ref[:, D_MODEL:] * br_b
    x1 = x_ref[...] + _dot(merged, wo_ref[...])
    x1_ref[...] = x1
    h2 = _rms_rows(x1, gf_ref[...])
    h2_ref[...] = h2
    logit_ref[...] = _dot(h2, wr_ref[...]) + br_ref[...]


def _route_kernel(logit_ref, meta_ref, wgt_ref, cnt_ref, run, start, *, block):
    sweep, step = pl.program_id(0), pl.program_id(1)

    @pl.when(jnp.logical_and(sweep == 0, step == 0))
    def _():
        run[...] = jnp.zeros_like(run)
        start[...] = jnp.zeros_like(start)

    @pl.when(jnp.logical_and(sweep == 1, step == 0))
    def _():
        counts = run[...]
        cnt_ref[...] = counts
        padded = jnp.floor((counts + (block - 1)) * (1.0 / block)) * block
        before_e = (_iota((LANES, LANES), 0) < _iota((LANES, LANES), 1)).astype(BF16)
        start[...] = _dot_exact_rhs(jnp.broadcast_to(padded, (8, LANES)), before_e, 3)[0:1, :]
        run[...] = jnp.zeros_like(run)

    x = logit_ref[...]
    tm = x.shape[0]
    lane = _iota((1, LANES), 1)
    lanef = lane.astype(F32)
    big = float(LANES)

    def first_argmax(vals, valid):
        mx = jnp.max(vals, axis=-1, keepdims=True)
        idx = jnp.min(jnp.where(jnp.logical_and(vals == mx, valid), lanef, big), axis=-1, keepdims=True)
        return mx, idx.astype(I32)

    gmask = lane < N_GROUPS
    gmax, gsel = first_argmax(jnp.where(gmask, x, NEG), gmask)
    p_g = 1.0 / jnp.sum(jnp.where(gmask, jnp.exp(x - gmax), 0.0), axis=-1, keepdims=True)
    first = N_GROUPS + EXPERTS_PER_GROUP * gsel
    emask = jnp.logical_and(lane >= first, lane < first + EXPERTS_PER_GROUP)
    ev = jnp.where(emask, x, NEG)
    m1, i1 = first_argmax(ev, emask)
    emask2 = jnp.logical_and(emask, lane != i1)
    m2, i2 = first_argmax(jnp.where(emask2, x, NEG), emask2)
    e21 = jnp.exp(m2 - m1)
    w1 = p_g / (1.0 + e21)
    w2 = p_g * e21 / (1.0 + e21)
    e1, e2 = i1 - N_GROUPS, i2 - N_GROUPS
    oh1, oh2 = lane == e1, lane == e2
    onehot = jnp.logical_or(oh1, oh2).astype(BF16)
    strict = (_iota((tm, tm), 0) > _iota((tm, tm), 1)).astype(BF16)
    slot = jnp.dot(strict, onehot, preferred_element_type=F32) + run[...] + start[...]
    run[...] = run[...] + jnp.sum(onehot.astype(F32), axis=0, keepdims=True)

    @pl.when(sweep == 1)
    def _():
        d1 = jnp.sum(jnp.where(oh1, slot, 0.0), axis=-1, keepdims=True).astype(I32)
        d2 = jnp.sum(jnp.where(oh2, slot, 0.0), axis=-1, keepdims=True).astype(I32)
        l8 = _iota((1, 8), 1)
        meta_ref[...] = jnp.where(l8 == 0, e1, jnp.where(l8 == 1, e2, jnp.where(l8 == 2, d1, jnp.where(l8 == 3, d2, 0))))
        wgt_ref[...] = jnp.where(l8 == 0, w1, jnp.where(l8 == 1, w2, 0.0))


def _row_copy(src, dst, sem, s, d):
    return pltpu.make_async_copy(src.at[pl.ds(s, 1)], dst.at[pl.ds(d, 1)], sem)


DMA_UNROLL = 8


DISPATCH_SLOTS = 3


def _dispatch_kernel(dest_ref, h_ref, zero_ref, xs_ref, buf, sem_in, sem_out, *, td, nsteps):
    del zero_ref
    step = pl.program_id(0)
    slot = lax.rem(step, DISPATCH_SLOTS)

    def tile_copy(tile, s):
        return pltpu.make_async_copy(h_ref.at[pl.ds(tile * td, td)], buf.at[s], sem_in.at[s])

    def drain(s):
        def body(t, carry):
            _row_copy(buf.at[s], xs_ref, sem_out.at[s], 0, 0).wait()
            _row_copy(buf.at[s], xs_ref, sem_out.at[s], 0, 0).wait()
            return carry
        lax.fori_loop(0, td, body, 0, unroll=DMA_UNROLL)

    @pl.when(step == 0)
    def _():
        tile_copy(0, 0).start()

    @pl.when(step >= 2)
    def _():
        drain(lax.rem(step + 1, DISPATCH_SLOTS))

    @pl.when(step + 1 < nsteps)
    def _():
        tile_copy(step + 1, lax.rem(step + 1, DISPATCH_SLOTS)).start()

    tile_copy(step, slot).wait()

    def issue(t, carry):
        _row_copy(buf.at[slot], xs_ref, sem_out.at[slot], t, dest_ref[0, t]).start(priority=0)
        _row_copy(buf.at[slot], xs_ref, sem_out.at[slot], t, dest_ref[0, td + t]).start(priority=1)
        return carry

    lax.fori_loop(0, td, issue, 0, unroll=DMA_UNROLL)

    @pl.when(step == nsteps - 1)
    def _():
        if nsteps >= 2:
            drain(lax.rem(step + 2, DISPATCH_SLOTS))
        drain(slot)


def _expert_kernel(be_ref, nu_ref, x_ref, w1_ref, w3_ref, w2_ref, y_ref):
    blk = pl.program_id(0)

    @pl.when(blk < nu_ref[0])
    def _():
        x = x_ref[...].astype(BF16)
        a = jnp.dot(x, w1_ref[...], preferred_element_type=F32)
        g = jnp.dot(x, w3_ref[...], preferred_element_type=F32)
        y_ref[...] = _dot(_silu(a) * g, w2_ref[...])

    @pl.when(blk >= nu_ref[0])
    def _():
        y_ref[...] = jnp.zeros_like(y_ref)


def _combine_kernel(dest_ref, next_ref, x1_ref, wgt_ref, yb_ref, o_ref, g0, g1, sems, *, tc):
    step, nsteps = pl.program_id(0), pl.num_programs(0)
    slot = step % 2

    def issue(idx_ref, s):
        def body(t, carry):
            _row_copy(yb_ref, g0.at[s], sems.at[s], idx_ref[0, t], t).start(priority=0)
            _row_copy(yb_ref, g1.at[s], sems.at[s], idx_ref[0, tc + t], t).start(priority=1)
            return carry
        lax.fori_loop(0, tc, body, 0, unroll=DMA_UNROLL)

    @pl.when(step == 0)
    def _():
        issue(dest_ref, 0)

    @pl.when(step + 1 < nsteps)
    def _():
        issue(next_ref, 1 - slot)

    def drain(t, carry):
        _row_copy(yb_ref, g0.at[slot], sems.at[slot], 0, 0).wait()
        _row_copy(yb_ref, g1.at[slot], sems.at[slot], 0, 0).wait()
        return carry

    lax.fori_loop(0, tc, drain, 0, unroll=DMA_UNROLL)
    wg = wgt_ref[...]
    o_ref[...] = x1_ref[...] + (g0[slot] * wg[:, 0:1] + g1[slot] * wg[:, 1:2])


def _hier_moe(x1, h2, logits, w_e1, w_e3, w_e2):
    n = x1.shape[0]
    tr = _pick_tile(n, 512)
    bm = MOE_BLOCK
    assert bm & (bm - 1) == 0
    meta, wgt, cnt = pl.pallas_call(
        functools.partial(_route_kernel, block=bm), grid=(2, n // tr),
        in_specs=[pl.BlockSpec((tr, LANES), lambda s, i: (i, 0))],
        out_specs=[pl.BlockSpec((tr, 8), lambda s, i: (i * s, 0)), pl.BlockSpec((tr, 8), lambda s, i: (i * s, 0)),
                   pl.BlockSpec((1, LANES), lambda s, i: (0, 0))],
        out_shape=[jax.ShapeDtypeStruct((n, 8), I32), jax.ShapeDtypeStruct((n, 8), F32),
                   jax.ShapeDtypeStruct((1, LANES), F32)],
        scratch_shapes=[pltpu.VMEM((1, LANES), F32), pltpu.VMEM((1, LANES), F32)],
        compiler_params=_params("arbitrary", "arbitrary"), name="moe_route")(logits)

    counts = cnt[0, :N_EXPERTS].astype(I32)
    pad_end = jnp.cumsum(((counts + bm - 1) // bm) * bm)
    nblk = -(-(2 * n) // bm) + N_EXPERTS
    n_used = (pad_end[-1] // bm).astype(I32).reshape(1)
    blk_first = jnp.arange(nblk, dtype=I32) * bm
    blk_expert = jnp.minimum(jnp.sum((pad_end[None, :] <= blk_first[:, None]).astype(I32), axis=1), N_EXPERTS - 1)

    td = _pick_tile(n, 256)
    dest = meta[:, 2:4].reshape(n // td, td, 2).transpose(0, 2, 1).reshape(n // td, 1, 2 * td)
    xs = pl.pallas_call(
        functools.partial(_dispatch_kernel, td=td, nsteps=n // td), grid=(n // td,),
        in_specs=[pl.BlockSpec((None, 1, 2 * td), lambda i: (i, 0, 0), memory_space=pltpu.SMEM),
                  pl.BlockSpec(memory_space=pl.ANY),
                  pl.BlockSpec(memory_space=pl.ANY)],
        out_specs=pl.BlockSpec(memory_space=pl.ANY),
        out_shape=jax.ShapeDtypeStruct((nblk * bm, D_MODEL), F32),
        scratch_shapes=[pltpu.VMEM((DISPATCH_SLOTS, td, D_MODEL), F32), pltpu.SemaphoreType.DMA((DISPATCH_SLOTS,)),
                        pltpu.SemaphoreType.DMA((DISPATCH_SLOTS,))],
        input_output_aliases={2: 0},
        compiler_params=_params("arbitrary"), name="moe_dispatch")(dest, h2, jnp.zeros((nblk * bm, D_MODEL), F32))

    grid_spec = pltpu.PrefetchScalarGridSpec(
        num_scalar_prefetch=2, grid=(nblk,),
        in_specs=[pl.BlockSpec((bm, D_MODEL), lambda i, be, nu: (i, 0)),
                  pl.BlockSpec((None, D_MODEL, D_EXPERT), lambda i, be, nu: (be[i], 0, 0)),
                  pl.BlockSpec((None, D_MODEL, D_EXPERT), lambda i, be, nu: (be[i], 0, 0)),
                  pl.BlockSpec((None, D_EXPERT, D_MODEL), lambda i, be, nu: (be[i], 0, 0))],
        out_specs=pl.BlockSpec((bm, D_MODEL), lambda i, be, nu: (i, 0)))
    yb = pl.pallas_call(
        _expert_kernel, grid_spec=grid_spec, out_shape=jax.ShapeDtypeStruct((nblk * bm, D_MODEL), F32),
        compiler_params=_params("arbitrary"), name="moe_experts")(blk_expert, n_used, xs, w_e1, w_e3, w_e2)

    tc = td
    return pl.pallas_call(
        functools.partial(_combine_kernel, tc=tc), grid=(n // tc,),
        in_specs=[pl.BlockSpec((None, 1, 2 * tc), lambda i: (i, 0, 0), memory_space=pltpu.SMEM),
                  pl.BlockSpec((None, 1, 2 * tc), lambda i: (jnp.minimum(i + 1, n // tc - 1), 0, 0),
                               memory_space=pltpu.SMEM),
                  pl.BlockSpec((tc, D_MODEL), lambda i: (i, 0)),
                  pl.BlockSpec((tc, 8), lambda i: (i, 0)),
                  pl.BlockSpec(memory_space=pl.ANY)],
        out_specs=pl.BlockSpec((tc, D_MODEL), lambda i: (i, 0)),
        out_shape=jax.ShapeDtypeStruct((n, D_MODEL), F32),
        scratch_shapes=[pltpu.VMEM((2, tc, D_MODEL), F32), pltpu.VMEM((2, tc, D_MODEL), F32),
                        pltpu.SemaphoreType.DMA((2,))],
        compiler_params=_params("arbitrary"), name="moe_combine")(dest, dest, x1, wgt, yb)


def _prep_weights(l, g_mix, w_in, b_f, g_qnorm, g_knorm, w_conv, a_log, dt_bias, g_onorm, b_gate,
                  w_br_a, w_br_b, w_out, g_ffn, w_grp, b_grp, w_exp, b_exp, w_e1, w_e3, w_e2):
    w = w_in[l]
    offs = [0]
    for s in (WA, WA, WA, H_A, QK_B, QK_B, V_B, H_B, H_B, V_B, D_MODEL, D_MODEL):
        offs.append(offs[-1] + s)
    col = lambda i, j=None: w[:, offs[i]:offs[(i if j is None else j) + 1]]
    n_small = H_A + 2 * H_B
    w_small = jnp.concatenate([col(3), col(7), col(8)], axis=1)
    zeros8 = jnp.zeros((H_A,), F32)
    bias = jnp.concatenate([b_f[l], zeros8, dt_bias[l]])
    alog = jnp.concatenate([zeros8, zeros8, a_log[l]])
    pad = lambda v, n: jnp.pad(v, (0, n - v.shape[0]))
    rows_t = 32
    return dict(
        g_mix=g_mix[l][None, :],
        wq=col(0).astype(BF16), wk=col(1).astype(BF16), wv=col(2).astype(BF16),
        gq=jnp.tile(g_qnorm[l], H_A)[None, :], gk=jnp.tile(g_knorm[l], H_A)[None, :],
        ws=jnp.pad(w_small, ((0, 0), (0, LANES - n_small))).astype(BF16),
        wst=jnp.pad(w_small.T, ((0, rows_t - n_small), (0, 0))).astype(BF16),
        wu=col(4, 6).astype(BF16),
        bias=pad(bias, LANES)[None, :], alog=pad(alog, LANES)[None, :],
        biast=pad(bias, rows_t)[:, None], alogt=pad(alog, rows_t)[:, None],
        wz=col(9).astype(BF16), wg=col(10, 11).astype(BF16), bg=b_gate[l][None, :],
        w_conv=w_conv[l], g_onorm=g_onorm[l][None, :],
        w_br_a=w_br_a[l].astype(BF16), w_br_b=w_br_b[l].astype(BF16), w_out=w_out[l].astype(BF16),
        g_ffn=g_ffn[l][None, :],
        wr=jnp.pad(jnp.concatenate([w_grp[l], w_exp[l]], axis=1), ((0, 0), (0, LANES - N_GROUPS - N_EXPERTS))).astype(BF16),
        br=pad(jnp.concatenate([b_grp[l], b_exp[l]]), LANES)[None, :],
        w_e1=w_e1[l].astype(BF16), w_e3=w_e3[l].astype(BF16), w_e2=w_e2[l].astype(BF16),
    )


def _layer(x, past_k, past_v, past_logf, s0, conv0, p):
    b, t, _ = x.shape
    past = past_k.shape[1]
    n = b * t
    xf = x.reshape(n, D_MODEL)
    tm = _pick_tile(n, 512)

    q16, k32, k16, v32, v16 = _row_call(
        _inproj_fox_kernel, n, tm, [xf], [p["g_mix"], p["wq"], p["wk"], p["wv"], p["gq"], p["gk"]],
        [((n, WA), BF16, False), ((n, WA), F32, False), ((n, WA), BF16, False), ((n, WA), F32, False),
         ((n, WA), BF16, False)], "inproj_fox")
    small, smallt, u = _row_call(
        _inproj_gdn_kernel, n, tm, [xf],
        [p["g_mix"], p["ws"], p["wst"], p["wu"], p["bias"], p["alog"], p["biast"], p["alogt"]],
        [((n, LANES), F32, False), ((32, n), F32, True), ((n, CONV_DIM), F32, False)], "inproj_gdn")
    z, gates = _row_call(
        _inproj_gate_kernel, n, tm, [xf], [p["g_mix"], p["wz"], p["wg"], p["bg"]],
        [((n, V_B), F32, False), ((n, 2 * D_MODEL), F32, False)], "inproj_gate")

    logf = small[:, :H_A].reshape(b, t, H_A)
    past_lanes = jnp.pad(past_logf.astype(F32), ((0, 0), (0, 0), (0, LANES - H_A)))
    k_aug, q_aug = _forget_companions(jnp.concatenate([past_lanes, small.reshape(b, t, LANES)], axis=1))
    k_all = jnp.concatenate([past_k.reshape(b, past, WA).astype(BF16), k16.reshape(b, t, WA)], axis=1)
    v_all = jnp.concatenate([past_v.reshape(b, past, WA).astype(BF16), v16.reshape(b, t, WA)], axis=1)
    y_a = _fox_attention(q16, q_aug[:, past:].reshape(n, 2 * LANES), k_all.reshape(b * (past + t), WA),
                         k_aug.reshape(b * (past + t), LANES), v_all.reshape(b * (past + t), WA), b, t, past)

    y_b, s_new, conv_new = _gdn(u, small, smallt, z, s0.reshape(b, H_B * DK_B, DV_B), conv0,
                                p["w_conv"], p["g_onorm"], b, t)

    x1, h2, logits = _row_call(
        _merge_kernel, n, tm, [xf, y_a, y_b, gates],
        [p["w_br_a"], p["w_br_b"], p["w_out"], p["g_ffn"], p["wr"], p["br"]],
        [((n, D_MODEL), F32, False), ((n, D_MODEL), F32, False), ((n, LANES), F32, False)], "merge_outproj")
    x2 = _hier_moe(x1, h2, logits, p["w_e1"], p["w_e3"], p["w_e2"])
    return (x2.reshape(b, t, D_MODEL), k32.reshape(b, t, H_A, DH_A), v32.reshape(b, t, H_A, DH_A), logf,
            s_new.reshape(b, H_B, DK_B, DV_B), conv_new)


def kernel(x_prompt, x_sample, cache_fox_k, cache_fox_v, cache_fox_logf, state_gdn, state_gdn_conv, g_mix, w_in, b_f, g_qnorm, g_knorm, w_conv, a_log, dt_bias, g_onorm, b_gate, w_br_a, w_br_b, w_out, g_ffn, w_grp, b_grp, w_exp, b_exp, w_e1, w_e3, w_e2):
    depth = w_in.shape[0]
    bp = x_prompt.shape[0]
    yp, ys = x_prompt, x_sample
    outs_p, outs_s = [], []
    for l in range(depth):
        p = _prep_weights(l, g_mix, w_in, b_f, g_qnorm, g_knorm, w_conv, a_log, dt_bias, g_onorm, b_gate,
                          w_br_a, w_br_b, w_out, g_ffn, w_grp, b_grp, w_exp, b_exp, w_e1, w_e3, w_e2)
        ys, *rest_s = _layer(ys, cache_fox_k[l], cache_fox_v[l], cache_fox_logf[l], state_gdn[l], state_gdn_conv[l], p)
        yp, *rest_p = _layer(
            yp, jnp.zeros((bp, 0, H_A, DH_A), F32), jnp.zeros((bp, 0, H_A, DH_A), F32), jnp.zeros((bp, 0, H_A), F32),
            jnp.zeros((bp, H_B, DK_B, DV_B), F32), jnp.zeros((bp, CONV_W - 1, CONV_DIM), F32), p)
        outs_p.append(rest_p)
        outs_s.append(rest_s)
    stack = lambda outs, i: jnp.stack([o[i] for o in outs])
    return (yp, ys,
            stack(outs_p, 0), stack(outs_p, 1), stack(outs_p, 2), stack(outs_p, 3), stack(outs_p, 4),
            stack(outs_s, 0), stack(outs_s, 1), stack(outs_s, 2), stack(outs_s, 3), stack(outs_s, 4))
```

```python
import functools
import math

import jax
import jax.numpy as jnp
import numpy as np
from jax import lax
from jax.experimental import pallas as pl
from jax.experimental.pallas import tpu as pltpu

F32, BF16, I32, U32 = jnp.float32, jnp.bfloat16, jnp.int32, jnp.uint32

D_MODEL = 1024
H_A, DH_A = 8, 64
WA = H_A * DH_A
H_B, DK_B, DV_B = 8, 64, 128
QK_B, V_B = H_B * DK_B, H_B * DV_B
CONV_W = 4
CONV_DIM = 2 * QK_B + V_B
N_GROUPS, EXPERTS_PER_GROUP = 4, 8
N_EXPERTS = N_GROUPS * EXPERTS_PER_GROUP
D_EXPERT = 512
GDN_CHUNK = 64
GDN_CHUNKS_PER_STEP = 4
EPS = 1e-6
LOG2E = math.log2(math.e)
NEG = -1e30
LANES = 128
V7X_VMEM_LIMIT = 56 * 1024 * 1024
MOE_BLOCK = 512


def _params(*sem):
    return pltpu.CompilerParams(dimension_semantics=sem, vmem_limit_bytes=V7X_VMEM_LIMIT)


def _pick_tile(n, cap, mult=8):
    if n <= cap:
        return n
    for t in range(cap - cap % mult, 0, -mult):
        if n % t == 0:
            return t
    return n


def _dot(a, b):
    return jnp.dot(a.astype(BF16), b.astype(BF16), preferred_element_type=F32)


def _dot_t(a, b):
    return lax.dot_general(a.astype(BF16), b.astype(BF16), (((1,), (1,)), ((), ())), preferred_element_type=F32)


def _dot_tl(a, b):
    return lax.dot_general(a.astype(BF16), b.astype(BF16), (((0,), (0,)), ((), ())), preferred_element_type=F32)


def _split(x, n):
    parts = []
    for _ in range(n):
        p = x.astype(BF16)
        parts.append(p)
        x = x - p.astype(F32)
    return parts


def _dot_exact_lhs(ones, x, n):
    acc = None
    for p in _split(x, n):
        t = jnp.dot(ones, p, preferred_element_type=F32)
        acc = t if acc is None else acc + t
    return acc


def _dot_exact_rhs(x, ones, n):
    acc = None
    for p in _split(x, n):
        t = jnp.dot(p, ones, preferred_element_type=F32)
        acc = t if acc is None else acc + t
    return acc


def _iota(shape, dim):
    return lax.broadcasted_iota(I32, shape, dim)


def _softplus(x):
    return jnp.maximum(x, 0.0) + jnp.log1p(jnp.exp(-jnp.abs(x)))


def _silu(x):
    return x * jax.nn.sigmoid(x)


def _rms_rows(x, g):
    return x * lax.rsqrt(jnp.mean(x * x, axis=-1, keepdims=True) + EPS) * g


def _head_ones(width, head):
    shift = head.bit_length() - 1
    assert head == 1 << shift
    r = jnp.right_shift(_iota((width, width), 0), shift)
    c = jnp.right_shift(_iota((width, width), 1), shift)
    return (r == c).astype(BF16)


def _group_sumsq(y, head):
    ones = _head_ones(2 * LANES, head)
    sq = y * y
    parts = [_dot_exact_rhs(sq[:, c:c + 2 * LANES], ones, 2) for c in range(0, y.shape[1], 2 * LANES)]
    return parts[0] if len(parts) == 1 else jnp.concatenate(parts, axis=1)


def _inproj_fox_kernel(x_ref, g_ref, wq_ref, wkt_ref, wvt_ref, gq_ref, gkt_ref,
                       q_ref, kt32_ref, kt16_ref, vt32_ref, vt16_ref):
    h = _rms_rows(x_ref[...], g_ref[...]).astype(BF16)
    q = jnp.dot(h, wq_ref[...], preferred_element_type=F32)
    q = q * lax.rsqrt(_group_sumsq(q, DH_A) * (1.0 / DH_A) + EPS) * gq_ref[...]
    q_ref[...] = (q * (DH_A ** -0.5 * LOG2E)).astype(BF16)
    kt = lax.dot_general(wkt_ref[...], h, (((1,), (1,)), ((), ())), preferred_element_type=F32)
    ones = _head_ones(2 * LANES, DH_A)
    sq = kt * kt
    ms = jnp.concatenate([_dot_exact_lhs(ones, sq[r:r + 2 * LANES, :], 2) for r in range(0, WA, 2 * LANES)], axis=0)
    kt = kt * lax.rsqrt(ms * (1.0 / DH_A) + EPS) * gkt_ref[...]
    kt32_ref[...] = kt
    kt16_ref[...] = kt.astype(BF16)
    vt = lax.dot_general(wvt_ref[...], h, (((1,), (1,)), ((), ())), preferred_element_type=F32)
    vt32_ref[...] = vt
    vt16_ref[...] = vt.astype(BF16)


def _inproj_fox(xf, p, b, t):
    n = b * t
    tm = _pick_tile(t, 512, LANES)
    per = t // tm
    row = lambda i: (i, 0)
    col = lambda i: (i // per, 0, i % per)
    full = [p["g_mix"], p["wq"], p["wkt"], p["wvt"], p["gq"], p["gkt"]]
    return pl.pallas_call(
        _inproj_fox_kernel, grid=(n // tm,),
        in_specs=[pl.BlockSpec((tm, D_MODEL), row)] + [pl.BlockSpec(a.shape, lambda i: (0, 0)) for a in full],
        out_specs=[pl.BlockSpec((tm, WA), row)] + [pl.BlockSpec((None, WA, tm), col)] * 4,
        out_shape=[jax.ShapeDtypeStruct((n, WA), BF16), jax.ShapeDtypeStruct((b, WA, t), F32),
                   jax.ShapeDtypeStruct((b, WA, t), BF16), jax.ShapeDtypeStruct((b, WA, t), F32),
                   jax.ShapeDtypeStruct((b, WA, t), BF16)],
        compiler_params=_params("arbitrary"), name="inproj_fox")(xf, *full)


def _small_epilogue(y, bias, a_log, idx):
    yb = y + bias
    logf = -_softplus(-yb)
    beta = jax.nn.sigmoid(y)
    g = -jnp.exp(a_log) * _softplus(yb)
    return jnp.where(idx < H_A, logf, jnp.where(idx < H_A + H_B, beta, g))


def _inproj_gdn_kernel(x_ref, g_ref, ws_ref, wst_ref, wu_ref, bias_ref, alog_ref, biast_ref, alogt_ref,
                       small_ref, smallt_ref, u_ref):
    h = _rms_rows(x_ref[...], g_ref[...]).astype(BF16)
    ys = jnp.dot(h, ws_ref[...], preferred_element_type=F32)
    small_ref[...] = _small_epilogue(ys, bias_ref[...], alog_ref[...], _iota(ys.shape, 1))
    yt = lax.dot_general(wst_ref[...], h, (((1,), (1,)), ((), ())), preferred_element_type=F32)
    smallt_ref[...] = _small_epilogue(yt, biast_ref[...], alogt_ref[...], _iota(yt.shape, 0))
    for c in range(0, CONV_DIM, 512):
        u_ref[:, c:c + 512] = jnp.dot(h, wu_ref[:, c:c + 512], preferred_element_type=F32)


def _inproj_gate_kernel(x_ref, g_ref, wz_ref, wg_ref, bg_ref, z_ref, gate_ref):
    h = _rms_rows(x_ref[...], g_ref[...]).astype(BF16)
    for c in range(0, V_B, 512):
        z_ref[:, c:c + 512] = jnp.dot(h, wz_ref[:, c:c + 512], preferred_element_type=F32)
    for c in range(0, 2 * D_MODEL, 512):
        y = jnp.dot(h, wg_ref[:, c:c + 512], preferred_element_type=F32)
        gate_ref[:, c:c + 512] = jax.nn.sigmoid(y + bg_ref[:, c:c + 512])


def _row_call(kernel, n, tm, row_inputs, full_inputs, outs, name):
    in_specs = [pl.BlockSpec((tm, a.shape[1]), lambda i: (i, 0)) for a in row_inputs]
    in_specs += [pl.BlockSpec(a.shape, lambda i, nd=a.ndim: (0,) * nd) for a in full_inputs]
    out_specs, out_shapes = [], []
    for shape, dtype, transposed in outs:
        if transposed:
            out_specs.append(pl.BlockSpec((shape[0], tm), lambda i: (0, i)))
        else:
            out_specs.append(pl.BlockSpec((shape[0] // n * tm, shape[1]), lambda i: (i, 0)))
        out_shapes.append(jax.ShapeDtypeStruct(shape, dtype))
    return pl.pallas_call(kernel, grid=(n // tm,), in_specs=in_specs, out_specs=out_specs, out_shape=out_shapes,
                          compiler_params=_params("arbitrary"), name=name)(*row_inputs, *full_inputs)


AUG_PIECES = 3
AUG_GROUP = 16


def _companion_placement():
    n = AUG_PIECES
    place_k = np.zeros((n, LANES, LANES), np.float32)
    place_q = np.zeros((n, LANES, 2 * LANES), np.float32)
    ones_k = np.zeros((LANES, 1), np.float32)
    ones_q = np.zeros((1, 2 * LANES), np.float32)
    for hd in range(H_A):
        pair, odd = divmod(hd, 2)
        g = AUG_GROUP * pair
        for k in range(n):
            place_k[k, g + n * odd + k, hd] = -1.0
            place_q[k, hd, odd * LANES + g + 2 * n + k] = 1.0
            ones_k[g + 2 * n + k, 0] = 1.0
            ones_q[0, odd * LANES + g + n * odd + k] = 1.0
    return jnp.asarray(place_k, BF16), jnp.asarray(place_q, BF16), jnp.asarray(ones_k, F32), jnp.asarray(ones_q, F32)


def _cumsum_kernel(x_ref, pk_ref, pq_ref, ok_ref, oq_ref, kaug_ref, qaug_ref, carry):
    @pl.when(pl.program_id(1) == 0)
    def _():
        carry[...] = jnp.zeros_like(carry)

    x = x_ref[...]
    t = x.shape[0]
    tril = (_iota((t, t), 0) >= _iota((t, t), 1)).astype(BF16)
    c = _dot_exact_lhs(tril, x, 3) + carry[...]
    carry[...] = c[t - 1:t, :]
    slab_k, slab_q = ok_ref[...], oq_ref[...]
    for k, piece in enumerate(_split(c * LOG2E, AUG_PIECES)):
        slab_k = slab_k + lax.dot_general(pk_ref[k], piece, (((1,), (1,)), ((), ())), preferred_element_type=F32)
        slab_q = slab_q + jnp.dot(piece, pq_ref[k], preferred_element_type=F32)
    kaug_ref[...] = slab_k.astype(BF16)
    qaug_ref[...] = slab_q.astype(BF16)


def _forget_companions(x):
    b, tt, c = x.shape
    tc = _pick_tile(tt, 512, LANES)
    consts = _companion_placement()
    return pl.pallas_call(
        _cumsum_kernel, grid=(b, tt // tc),
        in_specs=[pl.BlockSpec((None, tc, c), lambda i, j: (i, j, 0))]
                 + [pl.BlockSpec(a.shape, lambda i, j, nd=a.ndim: (0,) * nd) for a in consts],
        out_specs=[pl.BlockSpec((None, LANES, tc), lambda i, j: (i, 0, j)),
                   pl.BlockSpec((None, tc, 2 * LANES), lambda i, j: (i, j, 0))],
        out_shape=[jax.ShapeDtypeStruct((b, LANES, tt), BF16), jax.ShapeDtypeStruct((b, tt, 2 * LANES), BF16)],
        scratch_shapes=[pltpu.VMEM((1, c), F32)],
        compiler_params=_params("arbitrary", "arbitrary"), name="fox_cumsum")(x, *consts)


def _attn_kernel(qi_ref, kj_ref, last_ref, q_ref, qa_ref, kt_ref, kat_ref, vt_ref, o_ref, qq_sc, m_sc, l_sc, acc_sc,
                 *, tq, tk, past):
    n = pl.program_id(1)
    i, j = qi_ref[n], kj_ref[n]
    lo = _iota((1, LANES), 1) < DH_A
    halves = (lo, jnp.logical_not(lo))
    row_lo = _iota((LANES, 1), 0) < DH_A
    row_halves = (row_lo, jnp.logical_not(row_lo))

    @pl.when(j == 0)
    def _():
        m_sc[...] = jnp.full_like(m_sc, NEG)
        l_sc[...] = jnp.zeros_like(l_sc)
        acc_sc[...] = jnp.zeros_like(acc_sc)
        lane = _iota((1, LANES), 1)
        for hd in range(H_A):
            pair, odd = divmod(hd, 2)
            q2 = q_ref[:, pair * LANES:(pair + 1) * LANES]
            qq_sc[hd, :, :LANES] = jnp.where(halves[odd], q2, jnp.zeros_like(q2))
            comp = qa_ref[:, odd * LANES:(odd + 1) * LANES]
            own = jnp.logical_and(lane >= AUG_GROUP * pair, lane < AUG_GROUP * (pair + 1))
            qq_sc[hd, :, LANES:] = jnp.where(own, comp, jnp.zeros_like(comp))

    def step(masked):
        if masked:
            visible = j * tk + _iota((1, tk), 1) <= past + i * tq + _iota((tq, 1), 0)
        scores = []
        for p in range(H_A // 2):
            kk = jnp.concatenate([kt_ref[p * LANES:(p + 1) * LANES, :], kat_ref[...]], axis=0)
            scores += [jnp.dot(qq_sc[2 * p + half], kk, preferred_element_type=F32) for half in range(2)]
        probs, alphas = [], []
        for hd in range(H_A):
            s = jnp.where(visible, scores[hd], NEG) if masked else scores[hd]
            m_prev = m_sc[hd]
            m_new = jnp.maximum(m_prev, jnp.max(s, axis=-1, keepdims=True))
            alpha = jnp.exp2(m_prev - m_new)
            pr = jnp.exp2(s - m_new)
            l_sc[hd] = alpha * l_sc[hd] + jnp.sum(pr, axis=-1, keepdims=True)
            m_sc[hd] = m_new
            alphas.append(alpha)
            probs.append(pr.astype(BF16))
        for p in range(H_A // 2):
            v2 = vt_ref[p * LANES:(p + 1) * LANES, :]
            pvs = [_dot_t(probs[2 * p + half], jnp.where(row_halves[half], v2, jnp.zeros_like(v2))) for half in range(2)]
            acc_sc[p] = jnp.where(lo, alphas[2 * p], alphas[2 * p + 1]) * acc_sc[p] + pvs[0] + pvs[1]

    some_hidden = (j + 1) * tk - 1 > past + i * tq
    pl.when(some_hidden)(functools.partial(step, True))
    pl.when(jnp.logical_not(some_hidden))(functools.partial(step, False))

    @pl.when(last_ref[n] == 1)
    def _():
        for p in range(H_A // 2):
            l2 = jnp.where(lo, l_sc[2 * p], l_sc[2 * p + 1])
            o_ref[:, p * LANES:(p + 1) * LANES] = (acc_sc[p] / l2).astype(BF16)


def _fox_attention(q, q_aug, kt_all, kt_aug, vt_all, b, t, past):
    tt = past + t
    tq = _pick_tile(t, 512, 16)
    tk = _pick_tile(tt, 512, LANES)
    nq, nk = t // tq, tt // tk
    pairs = [(i, j) for i in range(nq) for j in range(nk) if j * tk <= past + (i + 1) * tq - 1]
    qi = jnp.array([p[0] for p in pairs], I32)
    kj = jnp.array([p[1] for p in pairs], I32)
    last = jnp.array([1 if (idx + 1 == len(pairs) or pairs[idx + 1][0] != p[0]) else 0
                      for idx, p in enumerate(pairs)], I32)
    grid_spec = pltpu.PrefetchScalarGridSpec(
        num_scalar_prefetch=3, grid=(b, len(pairs)),
        in_specs=[
            pl.BlockSpec((tq, WA), lambda bi, n, qi, kj, la: (bi * nq + qi[n], 0)),
            pl.BlockSpec((tq, 2 * LANES), lambda bi, n, qi, kj, la: (bi * nq + qi[n], 0)),
            pl.BlockSpec((None, WA, tk), lambda bi, n, qi, kj, la: (bi, 0, kj[n])),
            pl.BlockSpec((None, LANES, tk), lambda bi, n, qi, kj, la: (bi, 0, kj[n])),
            pl.BlockSpec((None, WA, tk), lambda bi, n, qi, kj, la: (bi, 0, kj[n])),
        ],
        out_specs=pl.BlockSpec((tq, WA), lambda bi, n, qi, kj, la: (bi * nq + qi[n], 0)),
        scratch_shapes=[pltpu.VMEM((H_A, tq, 2 * LANES), BF16), pltpu.VMEM((H_A, tq, 1), F32),
                        pltpu.VMEM((H_A, tq, 1), F32), pltpu.VMEM((H_A // 2, tq, LANES), F32)])
    return pl.pallas_call(
        functools.partial(_attn_kernel, tq=tq, tk=tk, past=past), grid_spec=grid_spec,
        out_shape=jax.ShapeDtypeStruct((b * t, WA), BF16),
        compiler_params=_params("arbitrary", "arbitrary"), name="fox_attention")(qi, kj, last, q, q_aug, kt_all, kt_aug, vt_all)


def _gdn_kernel(u_ref, small_ref, smallt_ref, z_ref, s0_ref, conv0_ref, wconv_ref, gon_ref,
                yb_ref, sout_ref, cout_ref, s_sc, ubuf, *, chunk, cps, nsteps):
    c = pl.program_id(1)
    L, C = chunk, cps
    R = L * C
    lg = L.bit_length() - 1
    keep = CONV_W - 1

    @pl.when(c == 0)
    def _():
        s_sc[...] = s0_ref[...]
        ubuf[8 - keep:8, :] = conv0_ref[...]

    ubuf[8:8 + R, :] = u_ref[...]
    w = wconv_ref[...]
    conv = ubuf[8 - keep:8 - keep + R, :] * w[0:1, :]
    for i in range(1, CONV_W):
        conv = conv + ubuf[8 - keep + i:8 - keep + i + R, :] * w[i:i + 1, :]
    act = _silu(conv)
    tail = ubuf[8 + R - keep:8 + R, :]
    ubuf[8 - keep:8, :] = tail

    @pl.when(c == nsteps - 1)
    def _():
        cout_ref[...] = tail

    sm = small_ref[...]
    smt = smallt_ref[...]
    rr, rc = _iota((R, R), 0), _iota((R, R), 1)
    same = jnp.right_shift(rr, lg) == jnp.right_shift(rc, lg)
    g_col = _dot_exact_lhs(jnp.logical_and(same, rr >= rc).astype(BF16), sm, 3)
    qn_all = act[:, :QK_B]
    qn_all = qn_all * lax.rsqrt(_group_sumsq(qn_all, DK_B) + EPS) * (DK_B ** -0.5)
    kn_all = act[:, QK_B:2 * QK_B]
    kn_all = kn_all * lax.rsqrt(_group_sumsq(kn_all, DK_B) + EPS)

    W2 = 2 * L
    lo64 = _iota((1, LANES), 1) < DK_B
    hi64 = jnp.logical_not(lo64)
    lane2 = _iota((1, W2), 1)
    lo_l, hi_l = lane2 < L, lane2 >= L
    ri, ci = _iota((L, W2), 0), jnp.bitwise_and(_iota((L, W2), 1), L - 1)
    incl, strict = ri >= ci, ri > ci
    eye = (ri == ci).astype(F32)
    n_sq = max(int(math.ceil(math.log2(L))) - 1, 0)
    sub128 = _iota((LANES, 1), 0) < DK_B

    def stack_heads(x, lo, hi):
        return jnp.concatenate([jnp.where(lo, x, 0.0), jnp.where(hi, x, 0.0)], axis=0)

    items = [(ch, p) for ch in range(C) for p in range(H_B // 2)]
    g_rows = []
    for ch in range(C):
        tt, jj = _iota((R, W2), 0), jnp.bitwise_and(_iota((R, W2), 1), L - 1)
        sel = jnp.logical_and(jnp.right_shift(tt, lg) == ch, jnp.bitwise_and(tt, L - 1) <= jj)
        g_rows.append(_dot_exact_rhs(smt, sel.astype(BF16), 3))

    st = {}
    for ch, p in items:
        rs = slice(ch * L, (ch + 1) * L)
        ha, hb = 2 * H_B + 2 * p, 2 * H_B + 2 * p + 1
        gca, gcb = g_col[rs, ha:ha + 1], g_col[rs, hb:hb + 1]
        gla, glb = gca[L - 1:L, :], gcb[L - 1:L, :]
        q2, k2 = qn_all[rs, p * LANES:(p + 1) * LANES], kn_all[rs, p * LANES:(p + 1) * LANES]
        beta2 = jnp.where(lo64, sm[rs, H_B + 2 * p:H_B + 2 * p + 1], sm[rs, H_B + 2 * p + 1:H_B + 2 * p + 2])
        eg2 = jnp.exp(jnp.where(lo64, gca, gcb))
        egl2 = jnp.exp(jnp.where(lo64, gla - gca, glb - gcb))
        kb2 = k2 * beta2
        gr = jnp.where(lo_l, g_rows[ch][ha:ha + 1, :], g_rows[ch][hb:hb + 1, :])
        decay = jnp.exp(jnp.where(incl, jnp.where(lo_l, gca, gcb) - gr, NEG))
        ybd = stack_heads(k2, lo64, hi64)
        va = act[rs, 2 * QK_B + 2 * p * DV_B:2 * QK_B + (2 * p + 1) * DV_B] * sm[rs, H_B + 2 * p:H_B + 2 * p + 1]
        vb = act[rs, 2 * QK_B + (2 * p + 1) * DV_B:2 * QK_B + (2 * p + 2) * DV_B] * sm[rs, H_B + 2 * p + 1:H_B + 2 * p + 2]
        st[ch, p] = dict(
            decay=decay, ybd=ybd, kb2=kb2, q2=q2,
            kbg_bd=stack_heads(kb2 * eg2, lo64, hi64), qd_bd=stack_heads(q2 * eg2, lo64, hi64),
            kd_bd=stack_heads(k2 * egl2, lo64, hi64), vst=jnp.concatenate([va, vb], axis=0),
            gl2=jnp.where(sub128, jnp.exp(gla), jnp.exp(glb)))
    for it in items:
        d = st[it]
        d["m"] = jnp.where(strict, _dot_t(d["kb2"], d["ybd"]) * d["decay"], 0.0)
        d["attn"] = jnp.where(incl, _dot_t(d["q2"], d["ybd"]) * d["decay"], 0.0)
    for it in items:
        d = st[it]
        d["t"] = eye - d["m"]
        d["mp"] = d["m"]
    for lvl in range(n_sq):
        for it in items:
            d = st[it]
            d["mp"] = _dot(d["mp"], stack_heads(d["mp"], lo_l, hi_l))
        for it in items:
            d = st[it]
            d["t"] = d["t"] + _dot(d["t"], stack_heads(d["mp"], lo_l, hi_l))
    for it in items:
        d = st[it]
        d["ust"] = _dot(stack_heads(d["t"], lo_l, hi_l), d["vst"])
        d["w2"] = _dot(d["t"], d["kbg_bd"])
    for ch in range(C):
        rs = slice(ch * L, (ch + 1) * L)
        s2s, ass = [], []
        for p in range(H_B // 2):
            d = st[ch, p]
            s2 = s_sc[p * LANES:(p + 1) * LANES, :]
            s2s.append(s2)
            ass.append(_dot(jnp.concatenate([stack_heads(d["w2"], lo64, hi64), d["qd_bd"]], axis=0), s2))
        for p in range(H_B // 2):
            d = st[ch, p]
            v_new = d["ust"] - ass[p][:W2, :]
            o_st = ass[p][W2:, :] + _dot(stack_heads(d["attn"], lo_l, hi_l), v_new)
            s_sc[p * LANES:(p + 1) * LANES, :] = s2s[p] * d["gl2"] + _dot_tl(d["kd_bd"], v_new)
            for half in range(2):
                hd = 2 * p + half
                o = o_st[half * L:(half + 1) * L, :]
                on = o * lax.rsqrt(jnp.mean(o * o, axis=-1, keepdims=True) + EPS) * gon_ref[...]
                zs = z_ref[rs, hd * DV_B:(hd + 1) * DV_B]
                yb_ref[rs, hd * DV_B:(hd + 1) * DV_B] = (on * _silu(zs)).astype(BF16)

    @pl.when(c == nsteps - 1)
    def _():
        sout_ref[...] = s_sc[...]


def _gdn(u, small, smallt, z, s0, conv0, w_conv, g_onorm, b, t):
    clen = GDN_CHUNK if t % GDN_CHUNK == 0 else t
    assert clen & (clen - 1) == 0, clen
    cps = GDN_CHUNKS_PER_STEP if (t // clen) % GDN_CHUNKS_PER_STEP == 0 else 1
    chunk = clen * cps
    nc = t // chunk
    smallt3 = smallt.reshape(smallt.shape[0], b * nc, chunk).transpose(1, 0, 2)
    keep = CONV_W - 1
    return pl.pallas_call(
        functools.partial(_gdn_kernel, chunk=clen, cps=cps, nsteps=nc), grid=(b, nc),
        in_specs=[
            pl.BlockSpec((chunk, CONV_DIM), lambda i, j: (i * nc + j, 0)),
            pl.BlockSpec((chunk, LANES), lambda i, j: (i * nc + j, 0)),
            pl.BlockSpec((None, smallt.shape[0], chunk), lambda i, j: (i * nc + j, 0, 0)),
            pl.BlockSpec((chunk, V_B), lambda i, j: (i * nc + j, 0)),
            pl.BlockSpec((None, H_B * DK_B, DV_B), lambda i, j: (i, 0, 0)),
            pl.BlockSpec((None, keep, CONV_DIM), lambda i, j: (i, 0, 0)),
            pl.BlockSpec((CONV_W, CONV_DIM), lambda i, j: (0, 0)),
            pl.BlockSpec((1, DV_B), lambda i, j: (0, 0)),
        ],
        out_specs=[
            pl.BlockSpec((chunk, V_B), lambda i, j: (i * nc + j, 0)),
            pl.BlockSpec((None, H_B * DK_B, DV_B), lambda i, j: (i, 0, 0)),
            pl.BlockSpec((None, keep, CONV_DIM), lambda i, j: (i, 0, 0)),
        ],
        out_shape=[jax.ShapeDtypeStruct((b * t, V_B), BF16),
                   jax.ShapeDtypeStruct((b, H_B * DK_B, DV_B), F32),
                   jax.ShapeDtypeStruct((b, keep, CONV_DIM), F32)],
        scratch_shapes=[pltpu.VMEM((H_B * DK_B, DV_B), F32), pltpu.VMEM((8 + chunk, CONV_DIM), F32)],
        compiler_params=_params("arbitrary", "arbitrary"), name="gdn_chunked",
    )(u, small, smallt3, z, s0, conv0, w_conv, g_onorm)


def _merge_kernel(x_ref, ya_ref, yb_ref, gate_ref, wa_ref, wb_ref, wo_ref, gf_ref, wr_ref, br_ref,
                  x1_ref, h2_ref, logit_ref):
    br_a = jnp.dot(ya_ref[...], wa_ref[...], preferred_element_type=F32)
    br_b = jnp.dot(yb_ref[...], wb_ref[...], preferred_element_type=F32)
    merged = gate_ref[:, :D_MODEL] * br_a + gate_ref[:, D_MODEL:] * br_b
    x1 = x_ref[...] + _dot(merged, wo_ref[...])
    x1_ref[...] = x1
    h2 = _rms_rows(x1, gf_ref[...])
    h2_ref[...] = h2
    logit_ref[...] = _dot(h2, wr_ref[...]) + br_ref[...]


def _route_kernel(logit_ref, meta_ref, wgt_ref, cnt_ref, run, start, *, block):
    sweep, step = pl.program_id(0), pl.program_id(1)

    @pl.when(jnp.logical_and(sweep == 0, step == 0))
    def _():
        run[...] = jnp.zeros_like(run)
        start[...] = jnp.zeros_like(start)

    @pl.when(jnp.logical_and(sweep == 1, step == 0))
    def _():
        counts = run[...]
        cnt_ref[...] = counts
        padded = jnp.floor((counts + (block - 1)) * (1.0 / block)) * block
        before_e = (_iota((LANES, LANES), 0) < _iota((LANES, LANES), 1)).astype(BF16)
        start[...] = _dot_exact_rhs(jnp.broadcast_to(padded, (8, LANES)), before_e, 3)[0:1, :]
        run[...] = jnp.zeros_like(run)

    x = logit_ref[...]
    tm = x.shape[0]
    lane = _iota((1, LANES), 1)
    lanef = lane.astype(F32)
    big = float(LANES)

    def first_argmax(vals, valid):
        mx = jnp.max(vals, axis=-1, keepdims=True)
        idx = jnp.min(jnp.where(jnp.logical_and(vals == mx, valid), lanef, big), axis=-1, keepdims=True)
        return mx, idx.astype(I32)

    gmask = lane < N_GROUPS
    gmax, gsel = first_argmax(jnp.where(gmask, x, NEG), gmask)
    p_g = 1.0 / jnp.sum(jnp.where(gmask, jnp.exp(x - gmax), 0.0), axis=-1, keepdims=True)
    first = N_GROUPS + EXPERTS_PER_GROUP * gsel
    emask = jnp.logical_and(lane >= first, lane < first + EXPERTS_PER_GROUP)
    ev = jnp.where(emask, x, NEG)
    m1, i1 = first_argmax(ev, emask)
    emask2 = jnp.logical_and(emask, lane != i1)
    m2, i2 = first_argmax(jnp.where(emask2, x, NEG), emask2)
    e21 = jnp.exp(m2 - m1)
    w1 = p_g / (1.0 + e21)
    w2 = p_g * e21 / (1.0 + e21)
    e1, e2 = i1 - N_GROUPS, i2 - N_GROUPS
    oh1, oh2 = lane == e1, lane == e2
    onehot = jnp.logical_or(oh1, oh2).astype(BF16)
    strict = (_iota((tm, tm), 0) > _iota((tm, tm), 1)).astype(BF16)
    slot = jnp.dot(strict, onehot, preferred_element_type=F32) + run[...] + start[...]
    run[...] = run[...] + jnp.sum(onehot.astype(F32), axis=0, keepdims=True)

    @pl.when(sweep == 1)
    def _():
        d1 = jnp.sum(jnp.where(oh1, slot, 0.0), axis=-1, keepdims=True).astype(I32)
        d2 = jnp.sum(jnp.where(oh2, slot, 0.0), axis=-1, keepdims=True).astype(I32)
        l8 = _iota((1, 8), 1)
        meta_ref[...] = jnp.where(l8 == 0, e1, jnp.where(l8 == 1, e2, jnp.where(l8 == 2, d1, jnp.where(l8 == 3, d2, 0))))
        wgt_ref[...] = jnp.where(l8 == 0, w1, jnp.where(l8 == 1, w2, 0.0))


def _row_copy(src, dst, sem, s, d):
    return pltpu.make_async_copy(src.at[pl.ds(s, 1)], dst.at[pl.ds(d, 1)], sem)


DMA_UNROLL = 8


DISPATCH_SLOTS = 3


def _dispatch_kernel(dest_ref, h_ref, zero_ref, xs_ref, buf, sem_in, sem_out, *, td, nsteps):
    del zero_ref
    step = pl.program_id(0)
    slot = lax.rem(step, DISPATCH_SLOTS)

    def tile_copy(tile, s):
        return pltpu.make_async_copy(h_ref.at[pl.ds(tile * td, td)], buf.at[s], sem_in.at[s])

    def drain(s):
        def body(t, carry):
            _row_copy(buf.at[s], xs_ref, sem_out.at[s], 0, 0).wait()
            _row_copy(buf.at[s], xs_ref, sem_out.at[s], 0, 0).wait()
            return carry
        lax.fori_loop(0, td, body, 0, unroll=DMA_UNROLL)

    @pl.when(step == 0)
    def _():
        tile_copy(0, 0).start()

    @pl.when(step >= 2)
    def _():
        drain(lax.rem(step + 1, DISPATCH_SLOTS))

    @pl.when(step + 1 < nsteps)
    def _():
        tile_copy(step + 1, lax.rem(step + 1, DISPATCH_SLOTS)).start()

    tile_copy(step, slot).wait()

    def issue(t, carry):
        _row_copy(buf.at[slot], xs_ref, sem_out.at[slot], t, dest_ref[0, t]).start(priority=0)
        _row_copy(buf.at[slot], xs_ref, sem_out.at[slot], t, dest_ref[0, td + t]).start(priority=1)
        return carry

    lax.fori_loop(0, td, issue, 0, unroll=DMA_UNROLL)

    @pl.when(step == nsteps - 1)
    def _():
        if nsteps >= 2:
            drain(lax.rem(step + 2, DISPATCH_SLOTS))
        drain(slot)


def _expert_kernel(be_ref, nu_ref, x_ref, w1_ref, w3_ref, w2_ref, y_ref):
    blk = pl.program_id(0)

    @pl.when(blk < nu_ref[0])
    def _():
        x = x_ref[...].astype(BF16)
        a = jnp.dot(x, w1_ref[...], preferred_element_type=F32)
        g = jnp.dot(x, w3_ref[...], preferred_element_type=F32)
        y_ref[...] = _dot(_silu(a) * g, w2_ref[...])

    @pl.when(blk >= nu_ref[0])
    def _():
        y_ref[...] = jnp.zeros_like(y_ref)


def _combine_kernel(dest_ref, next_ref, x1_ref, wgt_ref, yb_ref, o_ref, g0, g1, sems, *, tc):
    step, nsteps = pl.program_id(0), pl.num_programs(0)
    slot = step % 2

    def issue(idx_ref, s):
        def body(t, carry):
            _row_copy(yb_ref, g0.at[s], sems.at[s], idx_ref[0, t], t).start(priority=0)
            _row_copy(yb_ref, g1.at[s], sems.at[s], idx_ref[0, tc + t], t).start(priority=1)
            return carry
        lax.fori_loop(0, tc, body, 0, unroll=DMA_UNROLL)

    @pl.when(step == 0)
    def _():
        issue(dest_ref, 0)

    @pl.when(step + 1 < nsteps)
    def _():
        issue(next_ref, 1 - slot)

    def drain(t, carry):
        _row_copy(yb_ref, g0.at[slot], sems.at[slot], 0, 0).wait()
        _row_copy(yb_ref, g1.at[slot], sems.at[slot], 0, 0).wait()
        return carry

    lax.fori_loop(0, tc, drain, 0, unroll=DMA_UNROLL)
    wg = wgt_ref[...]
    o_ref[...] = x1_ref[...] + (g0[slot] * wg[:, 0:1] + g1[slot] * wg[:, 1:2])


def _hier_moe(x1, h2, logits, w_e1, w_e3, w_e2):
    n = x1.shape[0]
    tr = _pick_tile(n, 512)
    bm = MOE_BLOCK
    assert bm & (bm - 1) == 0
    meta, wgt, cnt = pl.pallas_call(
        functools.partial(_route_kernel, block=bm), grid=(2, n // tr),
        in_specs=[pl.BlockSpec((tr, LANES), lambda s, i: (i, 0))],
        out_specs=[pl.BlockSpec((tr, 8), lambda s, i: (i * s, 0)), pl.BlockSpec((tr, 8), lambda s, i: (i * s, 0)),
                   pl.BlockSpec((1, LANES), lambda s, i: (0, 0))],
        out_shape=[jax.ShapeDtypeStruct((n, 8), I32), jax.ShapeDtypeStruct((n, 8), F32),
                   jax.ShapeDtypeStruct((1, LANES), F32)],
        scratch_shapes=[pltpu.VMEM((1, LANES), F32), pltpu.VMEM((1, LANES), F32)],
        compiler_params=_params("arbitrary", "arbitrary"), name="moe_route")(logits)

    counts = cnt[0, :N_EXPERTS].astype(I32)
    pad_end = jnp.cumsum(((counts + bm - 1) // bm) * bm)
    nblk = -(-(2 * n) // bm) + N_EXPERTS
    n_used = (pad_end[-1] // bm).astype(I32).reshape(1)
    blk_first = jnp.arange(nblk, dtype=I32) * bm
    blk_expert = jnp.minimum(jnp.sum((pad_end[None, :] <= blk_first[:, None]).astype(I32), axis=1), N_EXPERTS - 1)

    td = _pick_tile(n, 256)
    dest = meta[:, 2:4].reshape(n // td, td, 2).transpose(0, 2, 1).reshape(n // td, 1, 2 * td)
    xs = pl.pallas_call(
        functools.partial(_dispatch_kernel, td=td, nsteps=n // td), grid=(n // td,),
        in_specs=[pl.BlockSpec((None, 1, 2 * td), lambda i: (i, 0, 0), memory_space=pltpu.SMEM),
                  pl.BlockSpec(memory_space=pl.ANY),
                  pl.BlockSpec(memory_space=pl.ANY)],
        out_specs=pl.BlockSpec(memory_space=pl.ANY),
        out_shape=jax.ShapeDtypeStruct((nblk * bm, D_MODEL), F32),
        scratch_shapes=[pltpu.VMEM((DISPATCH_SLOTS, td, D_MODEL), F32), pltpu.SemaphoreType.DMA((DISPATCH_SLOTS,)),
                        pltpu.SemaphoreType.DMA((DISPATCH_SLOTS,))],
        input_output_aliases={2: 0},
        compiler_params=_params("arbitrary"), name="moe_dispatch")(dest, h2, jnp.zeros((nblk * bm, D_MODEL), F32))

    grid_spec = pltpu.PrefetchScalarGridSpec(
        num_scalar_prefetch=2, grid=(nblk,),
        in_specs=[pl.BlockSpec((bm, D_MODEL), lambda i, be, nu: (i, 0)),
                  pl.BlockSpec((None, D_MODEL, D_EXPERT), lambda i, be, nu: (be[i], 0, 0)),
                  pl.BlockSpec((None, D_MODEL, D_EXPERT), lambda i, be, nu: (be[i], 0, 0)),
                  pl.BlockSpec((None, D_EXPERT, D_MODEL), lambda i, be, nu: (be[i], 0, 0))],
        out_specs=pl.BlockSpec((bm, D_MODEL), lambda i, be, nu: (i, 0)))
    yb = pl.pallas_call(
        _expert_kernel, grid_spec=grid_spec, out_shape=jax.ShapeDtypeStruct((nblk * bm, D_MODEL), F32),
        compiler_params=_params("arbitrary"), name="moe_experts")(blk_expert, n_used, xs, w_e1, w_e3, w_e2)

    tc = td
    return pl.pallas_call(
        functools.partial(_combine_kernel, tc=tc), grid=(n // tc,),
        in_specs=[pl.BlockSpec((None, 1, 2 * tc), lambda i: (i, 0, 0), memory_space=pltpu.SMEM),
                  pl.BlockSpec((None, 1, 2 * tc), lambda i: (jnp.minimum(i + 1, n // tc - 1), 0, 0),
                               memory_space=pltpu.SMEM),
                  pl.BlockSpec((tc, D_MODEL), lambda i: (i, 0)),
                  pl.BlockSpec((tc, 8), lambda i: (i, 0)),
                  pl.BlockSpec(memory_space=pl.ANY)],
        out_specs=pl.BlockSpec((tc, D_MODEL), lambda i: (i, 0)),
        out_shape=jax.ShapeDtypeStruct((n, D_MODEL), F32),
        scratch_shapes=[pltpu.VMEM((2, tc, D_MODEL), F32), pltpu.VMEM((2, tc, D_MODEL), F32),
                        pltpu.SemaphoreType.DMA((2,))],
        compiler_params=_params("arbitrary"), name="moe_combine")(dest, dest, x1, wgt, yb)


def _prep_weights(l, g_mix, w_in, b_f, g_qnorm, g_knorm, w_conv, a_log, dt_bias, g_onorm, b_gate,
                  w_br_a, w_br_b, w_out, g_ffn, w_grp, b_grp, w_exp, b_exp, w_e1, w_e3, w_e2):
    w = w_in[l]
    offs = [0]
    for s in (WA, WA, WA, H_A, QK_B, QK_B, V_B, H_B, H_B, V_B, D_MODEL, D_MODEL):
        offs.append(offs[-1] + s)
    col = lambda i, j=None: w[:, offs[i]:offs[(i if j is None else j) + 1]]
    n_small = H_A + 2 * H_B
    w_small = jnp.concatenate([col(3), col(7), col(8)], axis=1)
    zeros8 = jnp.zeros((H_A,), F32)
    bias = jnp.concatenate([b_f[l], zeros8, dt_bias[l]])
    alog = jnp.concatenate([zeros8, zeros8, a_log[l]])
    pad = lambda v, n: jnp.pad(v, (0, n - v.shape[0]))
    rows_t = 32
    return dict(
        g_mix=g_mix[l][None, :],
        wq=col(0).astype(BF16), wkt=col(1).T.astype(BF16), wvt=col(2).T.astype(BF16),
        gq=jnp.tile(g_qnorm[l], H_A)[None, :], gkt=jnp.tile(g_knorm[l], H_A)[:, None],
        ws=jnp.pad(w_small, ((0, 0), (0, LANES - n_small))).astype(BF16),
        wst=jnp.pad(w_small.T, ((0, rows_t - n_small), (0, 0))).astype(BF16),
        wu=col(4, 6).astype(BF16),
        bias=pad(bias, LANES)[None, :], alog=pad(alog, LANES)[None, :],
        biast=pad(bias, rows_t)[:, None], alogt=pad(alog, rows_t)[:, None],
        wz=col(9).astype(BF16), wg=col(10, 11).astype(BF16), bg=b_gate[l][None, :],
        w_conv=w_conv[l], g_onorm=g_onorm[l][None, :],
        w_br_a=w_br_a[l].astype(BF16), w_br_b=w_br_b[l].astype(BF16), w_out=w_out[l].astype(BF16),
        g_ffn=g_ffn[l][None, :],
        wr=jnp.pad(jnp.concatenate([w_grp[l], w_exp[l]], axis=1), ((0, 0), (0, LANES - N_GROUPS - N_EXPERTS))).astype(BF16),
        br=pad(jnp.concatenate([b_grp[l], b_exp[l]]), LANES)[None, :],
        w_e1=w_e1[l].astype(BF16), w_e3=w_e3[l].astype(BF16), w_e2=w_e2[l].astype(BF16),
    )


def _layer(x, past_k, past_v, past_logf, s0, conv0, p):
    b, t, _ = x.shape
    past = past_k.shape[1]
    n = b * t
    xf = x.reshape(n, D_MODEL)
    tm = _pick_tile(n, 512)

    q16, kt32, kt16, vt32, vt16 = _inproj_fox(xf, p, b, t)
    small, smallt, u = _row_call(
        _inproj_gdn_kernel, n, tm, [xf],
        [p["g_mix"], p["ws"], p["wst"], p["wu"], p["bias"], p["alog"], p["biast"], p["alogt"]],
        [((n, LANES), F32, False), ((32, n), F32, True), ((n, CONV_DIM), F32, False)], "inproj_gdn")
    z, gates = _row_call(
        _inproj_gate_kernel, n, tm, [xf], [p["g_mix"], p["wz"], p["wg"], p["bg"]],
        [((n, V_B), F32, False), ((n, 2 * D_MODEL), F32, False)], "inproj_gate")

    logf = small[:, :H_A].reshape(b, t, H_A)
    past_lanes = jnp.pad(past_logf.astype(F32), ((0, 0), (0, 0), (0, LANES - H_A)))
    k_aug, q_aug = _forget_companions(jnp.concatenate([past_lanes, small.reshape(b, t, LANES)], axis=1))
    feature_major = lambda c: c.transpose(0, 2, 3, 1).reshape(b, WA, past).astype(BF16)
    kt_all = jnp.concatenate([feature_major(past_k), kt16], axis=2)
    vt_all = jnp.concatenate([feature_major(past_v), vt16], axis=2)
    y_a = _fox_attention(q16, q_aug[:, past:].reshape(n, 2 * LANES), kt_all, k_aug, vt_all, b, t, past)

    y_b, s_new, conv_new = _gdn(u, small, smallt, z, s0.reshape(b, H_B * DK_B, DV_B), conv0,
                                p["w_conv"], p["g_onorm"], b, t)

    x1, h2, logits = _row_call(
        _merge_kernel, n, tm, [xf, y_a, y_b, gates],
        [p["w_br_a"], p["w_br_b"], p["w_out"], p["g_ffn"], p["wr"], p["br"]],
        [((n, D_MODEL), F32, False), ((n, D_MODEL), F32, False), ((n, LANES), F32, False)], "merge_outproj")
    x2 = _hier_moe(x1, h2, logits, p["w_e1"], p["w_e3"], p["w_e2"])
    token_major = lambda c: c.reshape(b, H_A, DH_A, t).transpose(0, 3, 1, 2)
    return (x2.reshape(b, t, D_MODEL), token_major(kt32), token_major(vt32), logf,
            s_new.reshape(b, H_B, DK_B, DV_B), conv_new)


def kernel(x_prompt, x_sample, cache_fox_k, cache_fox_v, cache_fox_logf, state_gdn, state_gdn_conv, g_mix, w_in, b_f, g_qnorm, g_knorm, w_conv, a_log, dt_bias, g_onorm, b_gate, w_br_a, w_br_b, w_out, g_ffn, w_grp, b_grp, w_exp, b_exp, w_e1, w_e3, w_e2):
    depth = w_in.shape[0]
    bp = x_prompt.shape[0]
    yp, ys = x_prompt, x_sample
    outs_p, outs_s = [], []
    for l in range(depth):
        p = _prep_weights(l, g_mix, w_in, b_f, g_qnorm, g_knorm, w_conv, a_log, dt_bias, g_onorm, b_gate,
                          w_br_a, w_br_b, w_out, g_ffn, w_grp, b_grp, w_exp, b_exp, w_e1, w_e3, w_e2)
        ys, *rest_s = _layer(ys, cache_fox_k[l], cache_fox_v[l], cache_fox_logf[l], state_gdn[l], state_gdn_conv[l], p)
        yp, *rest_p = _layer(
            yp, jnp.zeros((bp, 0, H_A, DH_A), F32), jnp.zeros((bp, 0, H_A, DH_A), F32), jnp.zeros((bp, 0, H_A), F32),
            jnp.zeros((bp, H_B, DK_B, DV_B), F32), jnp.zeros((bp, CONV_W - 1, CONV_DIM), F32), p)
        outs_p.append(rest_p)
        outs_s.append(rest_s)
    stack = lambda outs, i: jnp.stack([o[i] for o in outs])
    return (yp, ys,
            stack(outs_p, 0), stack(outs_p, 1), stack(outs_p, 2), stack(outs_p, 3), stack(outs_p, 4),
            stack(outs_s, 0), stack(outs_s, 1), stack(outs_s, 2), stack(outs_s, 3), stack(outs_s, 4))
```

```python
import functools
import math

import jax
import jax.numpy as jnp
import numpy as np
from jax import lax
from jax.experimental import pallas as pl
from jax.experimental.pallas import tpu as pltpu

F32, BF16, I32, U32 = jnp.float32, jnp.bfloat16, jnp.int32, jnp.uint32

D_MODEL = 1024
H_A, DH_A = 8, 64
WA = H_A * DH_A
H_B, DK_B, DV_B = 8, 64, 128
QK_B, V_B = H_B * DK_B, H_B * DV_B
CONV_W = 4
CONV_DIM = 2 * QK_B + V_B
N_GROUPS, EXPERTS_PER_GROUP = 4, 8
N_EXPERTS = N_GROUPS * EXPERTS_PER_GROUP
D_EXPERT = 512
GDN_CHUNK = 64
GDN_CHUNKS_PER_STEP = 4
EPS = 1e-6
LOG2E = math.log2(math.e)
NEG = -1e30
LANES = 128
V7X_VMEM_LIMIT = 56 * 1024 * 1024
MOE_BLOCK = 512


def _params(*sem):
    return pltpu.CompilerParams(dimension_semantics=sem, vmem_limit_bytes=V7X_VMEM_LIMIT)


def _pick_tile(n, cap, mult=8):
    if n <= cap:
        return n
    for t in range(cap - cap % mult, 0, -mult):
        if n % t == 0:
            return t
    return n


def _dot(a, b):
    return jnp.dot(a.astype(BF16), b.astype(BF16), preferred_element_type=F32)


def _dot_t(a, b):
    return lax.dot_general(a.astype(BF16), b.astype(BF16), (((1,), (1,)), ((), ())), preferred_element_type=F32)


def _dot_tl(a, b):
    return lax.dot_general(a.astype(BF16), b.astype(BF16), (((0,), (0,)), ((), ())), preferred_element_type=F32)


def _split(x, n):
    parts = []
    for _ in range(n):
        p = x.astype(BF16)
        parts.append(p)
        x = x - p.astype(F32)
    return parts


def _dot_exact_lhs(ones, x, n):
    acc = None
    for p in _split(x, n):
        t = jnp.dot(ones, p, preferred_element_type=F32)
        acc = t if acc is None else acc + t
    return acc


def _dot_exact_rhs(x, ones, n):
    acc = None
    for p in _split(x, n):
        t = jnp.dot(p, ones, preferred_element_type=F32)
        acc = t if acc is None else acc + t
    return acc


def _iota(shape, dim):
    return lax.broadcasted_iota(I32, shape, dim)


def _softplus(x):
    return jnp.maximum(x, 0.0) + jnp.log1p(jnp.exp(-jnp.abs(x)))


def _silu(x):
    return x * jax.nn.sigmoid(x)


def _rms_rows(x, g):
    return x * lax.rsqrt(jnp.mean(x * x, axis=-1, keepdims=True) + EPS) * g


def _head_ones(width, head):
    shift = head.bit_length() - 1
    assert head == 1 << shift
    r = jnp.right_shift(_iota((width, width), 0), shift)
    c = jnp.right_shift(_iota((width, width), 1), shift)
    return (r == c).astype(BF16)


def _group_sumsq(y, head):
    ones = _head_ones(2 * LANES, head)
    sq = y * y
    parts = [_dot_exact_rhs(sq[:, c:c + 2 * LANES], ones, 2) for c in range(0, y.shape[1], 2 * LANES)]
    return parts[0] if len(parts) == 1 else jnp.concatenate(parts, axis=1)


def _inproj_fox_kernel(x_ref, g_ref, wq_ref, wkt_ref, wvt_ref, gq_ref, gkt_ref,
                       q_ref, kt32_ref, kt16_ref, vt32_ref, vt16_ref):
    h = _rms_rows(x_ref[...], g_ref[...]).astype(BF16)
    q = jnp.dot(h, wq_ref[...], preferred_element_type=F32)
    q = q * lax.rsqrt(_group_sumsq(q, DH_A) * (1.0 / DH_A) + EPS) * gq_ref[...]
    q_ref[...] = (q * (DH_A ** -0.5 * LOG2E)).astype(BF16)
    kt = lax.dot_general(wkt_ref[...], h, (((1,), (1,)), ((), ())), preferred_element_type=F32)
    ones = _head_ones(2 * LANES, DH_A)
    sq = kt * kt
    ms = jnp.concatenate([_dot_exact_lhs(ones, sq[r:r + 2 * LANES, :], 2) for r in range(0, WA, 2 * LANES)], axis=0)
    kt = kt * lax.rsqrt(ms * (1.0 / DH_A) + EPS) * gkt_ref[...]
    kt32_ref[...] = kt
    kt16_ref[...] = kt.astype(BF16)
    vt = lax.dot_general(wvt_ref[...], h, (((1,), (1,)), ((), ())), preferred_element_type=F32)
    vt32_ref[...] = vt
    vt16_ref[...] = vt.astype(BF16)


def _inproj_fox(xf, p, b, t):
    n = b * t
    tm = _pick_tile(t, 512, LANES)
    per = t // tm
    row = lambda i: (i, 0)
    col = lambda i: (i // per, 0, i % per)
    full = [p["g_mix"], p["wq"], p["wkt"], p["wvt"], p["gq"], p["gkt"]]
    return pl.pallas_call(
        _inproj_fox_kernel, grid=(n // tm,),
        in_specs=[pl.BlockSpec((tm, D_MODEL), row)] + [pl.BlockSpec(a.shape, lambda i: (0, 0)) for a in full],
        out_specs=[pl.BlockSpec((tm, WA), row)] + [pl.BlockSpec((None, WA, tm), col)] * 4,
        out_shape=[jax.ShapeDtypeStruct((n, WA), BF16), jax.ShapeDtypeStruct((b, WA, t), F32),
                   jax.ShapeDtypeStruct((b, WA, t), BF16), jax.ShapeDtypeStruct((b, WA, t), F32),
                   jax.ShapeDtypeStruct((b, WA, t), BF16)],
        compiler_params=_params("arbitrary"), name="inproj_fox")(xf, *full)


def _small_epilogue(y, bias, a_log, idx):
    yb = y + bias
    logf = -_softplus(-yb)
    beta = jax.nn.sigmoid(y)
    g = -jnp.exp(a_log) * _softplus(yb)
    return jnp.where(idx < H_A, logf, jnp.where(idx < H_A + H_B, beta, g))


def _inproj_gdn_kernel(x_ref, g_ref, ws_ref, wst_ref, wu_ref, bias_ref, alog_ref, biast_ref, alogt_ref,
                       small_ref, smallt_ref, u_ref):
    h = _rms_rows(x_ref[...], g_ref[...]).astype(BF16)
    ys = jnp.dot(h, ws_ref[...], preferred_element_type=F32)
    small_ref[...] = _small_epilogue(ys, bias_ref[...], alog_ref[...], _iota(ys.shape, 1))
    yt = lax.dot_general(wst_ref[...], h, (((1,), (1,)), ((), ())), preferred_element_type=F32)
    smallt_ref[...] = _small_epilogue(yt, biast_ref[...], alogt_ref[...], _iota(yt.shape, 0))
    for c in range(0, CONV_DIM, 512):
        u_ref[:, c:c + 512] = jnp.dot(h, wu_ref[:, c:c + 512], preferred_element_type=F32)


def _inproj_gate_kernel(x_ref, g_ref, wz_ref, wg_ref, bg_ref, z_ref, gate_ref):
    h = _rms_rows(x_ref[...], g_ref[...]).astype(BF16)
    for c in range(0, V_B, 512):
        z_ref[:, c:c + 512] = jnp.dot(h, wz_ref[:, c:c + 512], preferred_element_type=F32)
    for c in range(0, 2 * D_MODEL, 512):
        y = jnp.dot(h, wg_ref[:, c:c + 512], preferred_element_type=F32)
        gate_ref[:, c:c + 512] = jax.nn.sigmoid(y + bg_ref[:, c:c + 512])


def _row_call(kernel, n, tm, row_inputs, full_inputs, outs, name):
    in_specs = [pl.BlockSpec((tm, a.shape[1]), lambda i: (i, 0)) for a in row_inputs]
    in_specs += [pl.BlockSpec(a.shape, lambda i, nd=a.ndim: (0,) * nd) for a in full_inputs]
    out_specs, out_shapes = [], []
    for shape, dtype, transposed in outs:
        if transposed:
            out_specs.append(pl.BlockSpec((shape[0], tm), lambda i: (0, i)))
        else:
            out_specs.append(pl.BlockSpec((shape[0] // n * tm, shape[1]), lambda i: (i, 0)))
        out_shapes.append(jax.ShapeDtypeStruct(shape, dtype))
    return pl.pallas_call(kernel, grid=(n // tm,), in_specs=in_specs, out_specs=out_specs, out_shape=out_shapes,
                          compiler_params=_params("arbitrary"), name=name)(*row_inputs, *full_inputs)


AUG_PIECES = 3
AUG_GROUP = 16


def _companion_placement():
    n = AUG_PIECES
    place_k = np.zeros((n, LANES, LANES), np.float32)
    place_q = np.zeros((n, LANES, 2 * LANES), np.float32)
    ones_k = np.zeros((LANES, 1), np.float32)
    ones_q = np.zeros((1, 2 * LANES), np.float32)
    for hd in range(H_A):
        pair, odd = divmod(hd, 2)
        g = AUG_GROUP * pair
        for k in range(n):
            place_k[k, g + n * odd + k, hd] = -1.0
            place_q[k, hd, odd * LANES + g + 2 * n + k] = 1.0
            ones_k[g + 2 * n + k, 0] = 1.0
            ones_q[0, odd * LANES + g + n * odd + k] = 1.0
    return jnp.asarray(place_k, BF16), jnp.asarray(place_q, BF16), jnp.asarray(ones_k, F32), jnp.asarray(ones_q, F32)


def _cumsum_kernel(x_ref, pk_ref, pq_ref, ok_ref, oq_ref, kaug_ref, qaug_ref, carry):
    @pl.when(pl.program_id(1) == 0)
    def _():
        carry[...] = jnp.zeros_like(carry)

    x = x_ref[...]
    t = x.shape[0]
    sub = _pick_tile(t, 512)
    tril = (_iota((sub, sub), 0) >= _iota((sub, sub), 1)).astype(BF16)
    last, blocks = carry[...], []
    for r in range(0, t, sub):
        blocks.append(_dot_exact_lhs(tril, x[r:r + sub, :], 3) + last)
        last = blocks[-1][sub - 1:sub, :]
    carry[...] = last
    c = blocks[0] if len(blocks) == 1 else jnp.concatenate(blocks, axis=0)
    slab_k, slab_q = ok_ref[...], oq_ref[...]
    for k, piece in enumerate(_split(c * LOG2E, AUG_PIECES)):
        slab_k = slab_k + lax.dot_general(pk_ref[k], piece, (((1,), (1,)), ((), ())), preferred_element_type=F32)
        slab_q = slab_q + jnp.dot(piece, pq_ref[k], preferred_element_type=F32)
    kaug_ref[...] = slab_k.astype(BF16)
    qaug_ref[...] = slab_q.astype(BF16)


def _forget_companions(x):
    b, tt, c = x.shape
    tc = _pick_tile(tt, 512, LANES)
    consts = _companion_placement()
    return pl.pallas_call(
        _cumsum_kernel, grid=(b, tt // tc),
        in_specs=[pl.BlockSpec((None, tc, c), lambda i, j: (i, j, 0))]
                 + [pl.BlockSpec(a.shape, lambda i, j, nd=a.ndim: (0,) * nd) for a in consts],
        out_specs=[pl.BlockSpec((None, LANES, tc), lambda i, j: (i, 0, j)),
                   pl.BlockSpec((None, tc, 2 * LANES), lambda i, j: (i, j, 0))],
        out_shape=[jax.ShapeDtypeStruct((b, LANES, tt), BF16), jax.ShapeDtypeStruct((b, tt, 2 * LANES), BF16)],
        scratch_shapes=[pltpu.VMEM((1, c), F32)],
        compiler_params=_params("arbitrary", "arbitrary"), name="fox_cumsum")(x, *consts)


def _attn_kernel(qi_ref, kj_ref, last_ref, q_ref, qa_ref, kt_ref, kat_ref, vt_ref, o_ref, qq_sc, m_sc, l_sc, acc_sc,
                 *, tq, tk, past):
    n = pl.program_id(1)
    i, j = qi_ref[n], kj_ref[n]
    lo = _iota((1, LANES), 1) < DH_A
    halves = (lo, jnp.logical_not(lo))
    row_lo = _iota((LANES, 1), 0) < DH_A
    row_halves = (row_lo, jnp.logical_not(row_lo))

    @pl.when(j == 0)
    def _():
        m_sc[...] = jnp.full_like(m_sc, NEG)
        l_sc[...] = jnp.zeros_like(l_sc)
        acc_sc[...] = jnp.zeros_like(acc_sc)
        lane = _iota((1, LANES), 1)
        for hd in range(H_A):
            pair, odd = divmod(hd, 2)
            q2 = q_ref[:, pair * LANES:(pair + 1) * LANES]
            qq_sc[hd, :, :LANES] = jnp.where(halves[odd], q2, jnp.zeros_like(q2))
            comp = qa_ref[:, odd * LANES:(odd + 1) * LANES]
            own = jnp.logical_and(lane >= AUG_GROUP * pair, lane < AUG_GROUP * (pair + 1))
            qq_sc[hd, :, LANES:] = jnp.where(own, comp, jnp.zeros_like(comp))

    def step(masked):
        if masked:
            visible = j * tk + _iota((1, tk), 1) <= past + i * tq + _iota((tq, 1), 0)
        scores = []
        for p in range(H_A // 2):
            kk = jnp.concatenate([kt_ref[p * LANES:(p + 1) * LANES, :], kat_ref[...]], axis=0)
            scores += [jnp.dot(qq_sc[2 * p + half], kk, preferred_element_type=F32) for half in range(2)]
        probs, alphas = [], []
        for hd in range(H_A):
            s = jnp.where(visible, scores[hd], NEG) if masked else scores[hd]
            m_prev = m_sc[hd]
            m_new = jnp.maximum(m_prev, jnp.max(s, axis=-1, keepdims=True))
            alpha = jnp.exp2(m_prev - m_new)
            pr = jnp.exp2(s - m_new)
            l_sc[hd] = alpha * l_sc[hd] + jnp.sum(pr, axis=-1, keepdims=True)
            m_sc[hd] = m_new
            alphas.append(alpha)
            probs.append(pr.astype(BF16))
        for p in range(H_A // 2):
            v2 = vt_ref[p * LANES:(p + 1) * LANES, :]
            pvs = [_dot_t(probs[2 * p + half], jnp.where(row_halves[half], v2, jnp.zeros_like(v2))) for half in range(2)]
            acc_sc[p] = jnp.where(lo, alphas[2 * p], alphas[2 * p + 1]) * acc_sc[p] + pvs[0] + pvs[1]

    some_hidden = (j + 1) * tk - 1 > past + i * tq
    pl.when(some_hidden)(functools.partial(step, True))
    pl.when(jnp.logical_not(some_hidden))(functools.partial(step, False))

    @pl.when(last_ref[n] == 1)
    def _():
        for p in range(H_A // 2):
            l2 = jnp.where(lo, l_sc[2 * p], l_sc[2 * p + 1])
            o_ref[:, p * LANES:(p + 1) * LANES] = (acc_sc[p] / l2).astype(BF16)


def _fox_attention(q, q_aug, kt_all, kt_aug, vt_all, b, t, past):
    tt = past + t
    tq = _pick_tile(t, 512, 16)
    tk = _pick_tile(tt, 512, LANES)
    nq, nk = t // tq, tt // tk
    pairs = [(i, j) for i in range(nq) for j in range(nk) if j * tk <= past + (i + 1) * tq - 1]
    qi = jnp.array([p[0] for p in pairs], I32)
    kj = jnp.array([p[1] for p in pairs], I32)
    last = jnp.array([1 if (idx + 1 == len(pairs) or pairs[idx + 1][0] != p[0]) else 0
                      for idx, p in enumerate(pairs)], I32)
    grid_spec = pltpu.PrefetchScalarGridSpec(
        num_scalar_prefetch=3, grid=(b, len(pairs)),
        in_specs=[
            pl.BlockSpec((tq, WA), lambda bi, n, qi, kj, la: (bi * nq + qi[n], 0)),
            pl.BlockSpec((tq, 2 * LANES), lambda bi, n, qi, kj, la: (bi * nq + qi[n], 0)),
            pl.BlockSpec((None, WA, tk), lambda bi, n, qi, kj, la: (bi, 0, kj[n])),
            pl.BlockSpec((None, LANES, tk), lambda bi, n, qi, kj, la: (bi, 0, kj[n])),
            pl.BlockSpec((None, WA, tk), lambda bi, n, qi, kj, la: (bi, 0, kj[n])),
        ],
        out_specs=pl.BlockSpec((tq, WA), lambda bi, n, qi, kj, la: (bi * nq + qi[n], 0)),
        scratch_shapes=[pltpu.VMEM((H_A, tq, 2 * LANES), BF16), pltpu.VMEM((H_A, tq, 1), F32),
                        pltpu.VMEM((H_A, tq, 1), F32), pltpu.VMEM((H_A // 2, tq, LANES), F32)])
    return pl.pallas_call(
        functools.partial(_attn_kernel, tq=tq, tk=tk, past=past), grid_spec=grid_spec,
        out_shape=jax.ShapeDtypeStruct((b * t, WA), BF16),
        compiler_params=_params("arbitrary", "arbitrary"), name="fox_attention")(qi, kj, last, q, q_aug, kt_all, kt_aug, vt_all)


def _gdn_kernel(u_ref, small_ref, smallt_ref, z_ref, s0_ref, conv0_ref, wconv_ref, gon_ref,
                yb_ref, sout_ref, cout_ref, s_sc, ubuf, *, chunk, cps, nsteps):
    c = pl.program_id(1)
    L, C = chunk, cps
    R = L * C
    lg = L.bit_length() - 1
    keep = CONV_W - 1

    @pl.when(c == 0)
    def _():
        s_sc[...] = s0_ref[...]
        ubuf[8 - keep:8, :] = conv0_ref[...]

    ubuf[8:8 + R, :] = u_ref[...]
    w = wconv_ref[...]
    conv = ubuf[8 - keep:8 - keep + R, :] * w[0:1, :]
    for i in range(1, CONV_W):
        conv = conv + ubuf[8 - keep + i:8 - keep + i + R, :] * w[i:i + 1, :]
    act = _silu(conv)
    tail = ubuf[8 + R - keep:8 + R, :]
    ubuf[8 - keep:8, :] = tail

    @pl.when(c == nsteps - 1)
    def _():
        cout_ref[...] = tail

    sm = small_ref[...]
    smt = smallt_ref[...]
    rr, rc = _iota((R, R), 0), _iota((R, R), 1)
    same = jnp.right_shift(rr, lg) == jnp.right_shift(rc, lg)
    g_col = _dot_exact_lhs(jnp.logical_and(same, rr >= rc).astype(BF16), sm, 3)
    qn_all = act[:, :QK_B]
    qn_all = qn_all * lax.rsqrt(_group_sumsq(qn_all, DK_B) + EPS) * (DK_B ** -0.5)
    kn_all = act[:, QK_B:2 * QK_B]
    kn_all = kn_all * lax.rsqrt(_group_sumsq(kn_all, DK_B) + EPS)

    W2 = 2 * L
    lo64 = _iota((1, LANES), 1) < DK_B
    hi64 = jnp.logical_not(lo64)
    lane2 = _iota((1, W2), 1)
    lo_l, hi_l = lane2 < L, lane2 >= L
    ri, ci = _iota((L, W2), 0), jnp.bitwise_and(_iota((L, W2), 1), L - 1)
    incl, strict = ri >= ci, ri > ci
    eye = (ri == ci).astype(F32)
    n_sq = max(int(math.ceil(math.log2(L))) - 1, 0)
    sub128 = _iota((LANES, 1), 0) < DK_B

    def stack_heads(x, lo, hi):
        x = x.astype(BF16)
        zero = jnp.zeros_like(x)
        return jnp.concatenate([jnp.where(lo, x, zero), jnp.where(hi, x, zero)], axis=0)

    items = [(ch, p) for ch in range(C) for p in range(H_B // 2)]
    g_rows = []
    for ch in range(C):
        tt, jj = _iota((R, W2), 0), jnp.bitwise_and(_iota((R, W2), 1), L - 1)
        sel = jnp.logical_and(jnp.right_shift(tt, lg) == ch, jnp.bitwise_and(tt, L - 1) <= jj)
        g_rows.append(_dot_exact_rhs(smt, sel.astype(BF16), 3))

    st = {}
    for ch, p in items:
        rs = slice(ch * L, (ch + 1) * L)
        ha, hb = 2 * H_B + 2 * p, 2 * H_B + 2 * p + 1
        gca, gcb = g_col[rs, ha:ha + 1], g_col[rs, hb:hb + 1]
        gla, glb = gca[L - 1:L, :], gcb[L - 1:L, :]
        q2, k2 = qn_all[rs, p * LANES:(p + 1) * LANES], kn_all[rs, p * LANES:(p + 1) * LANES]
        beta2 = jnp.where(lo64, sm[rs, H_B + 2 * p:H_B + 2 * p + 1], sm[rs, H_B + 2 * p + 1:H_B + 2 * p + 2])
        eg2 = jnp.exp(jnp.where(lo64, gca, gcb))
        egl2 = jnp.exp(jnp.where(lo64, gla - gca, glb - gcb))
        kb2 = k2 * beta2
        gr = jnp.where(lo_l, g_rows[ch][ha:ha + 1, :], g_rows[ch][hb:hb + 1, :])
        decay = jnp.exp(jnp.where(incl, jnp.where(lo_l, gca, gcb) - gr, NEG))
        ybd = stack_heads(k2, lo64, hi64)
        va = act[rs, 2 * QK_B + 2 * p * DV_B:2 * QK_B + (2 * p + 1) * DV_B] * sm[rs, H_B + 2 * p:H_B + 2 * p + 1]
        vb = act[rs, 2 * QK_B + (2 * p + 1) * DV_B:2 * QK_B + (2 * p + 2) * DV_B] * sm[rs, H_B + 2 * p + 1:H_B + 2 * p + 2]
        st[ch, p] = dict(
            decay=decay, ybd=ybd, kb2=kb2, q2=q2,
            kbg_bd=stack_heads(kb2 * eg2, lo64, hi64), qd_bd=stack_heads(q2 * eg2, lo64, hi64),
            kd_bd=stack_heads(k2 * egl2, lo64, hi64), vst=jnp.concatenate([va, vb], axis=0),
            gl2=jnp.where(sub128, jnp.exp(gla), jnp.exp(glb)))
    for it in items:
        d = st[it]
        d["m"] = jnp.where(strict, _dot_t(d["kb2"], d["ybd"]) * d["decay"], 0.0)
        d["attn"] = jnp.where(incl, _dot_t(d["q2"], d["ybd"]) * d["decay"], 0.0)
    for it in items:
        d = st[it]
        d["t"] = eye - d["m"]
        d["mp"] = d["m"]
    for lvl in range(n_sq):
        for it in items:
            d = st[it]
            d["mp"] = _dot(d["mp"], stack_heads(d["mp"], lo_l, hi_l))
        for it in items:
            d = st[it]
            d["t"] = d["t"] + _dot(d["t"], stack_heads(d["mp"], lo_l, hi_l))
    for it in items:
        d = st[it]
        d["ust"] = _dot(stack_heads(d["t"], lo_l, hi_l), d["vst"])
        d["w2"] = _dot(d["t"], d["kbg_bd"])
    for ch in range(C):
        rs = slice(ch * L, (ch + 1) * L)
        s2s, ass = [], []
        for p in range(H_B // 2):
            d = st[ch, p]
            s2 = s_sc[p * LANES:(p + 1) * LANES, :]
            s2s.append(s2)
            ass.append(_dot(jnp.concatenate([stack_heads(d["w2"], lo64, hi64), d["qd_bd"]], axis=0), s2))
        for p in range(H_B // 2):
            d = st[ch, p]
            v_new = d["ust"] - ass[p][:W2, :]
            o_st = ass[p][W2:, :] + _dot(stack_heads(d["attn"], lo_l, hi_l), v_new)
            s_sc[p * LANES:(p + 1) * LANES, :] = s2s[p] * d["gl2"] + _dot_tl(d["kd_bd"], v_new)
            for half in range(2):
                hd = 2 * p + half
                o = o_st[half * L:(half + 1) * L, :]
                on = o * lax.rsqrt(jnp.mean(o * o, axis=-1, keepdims=True) + EPS) * gon_ref[...]
                zs = z_ref[rs, hd * DV_B:(hd + 1) * DV_B]
                yb_ref[rs, hd * DV_B:(hd + 1) * DV_B] = (on * _silu(zs)).astype(BF16)

    @pl.when(c == nsteps - 1)
    def _():
        sout_ref[...] = s_sc[...]


def _gdn(u, small, smallt, z, s0, conv0, w_conv, g_onorm, b, t):
    clen = GDN_CHUNK if t % GDN_CHUNK == 0 else t
    assert clen & (clen - 1) == 0, clen
    cps = GDN_CHUNKS_PER_STEP if (t // clen) % GDN_CHUNKS_PER_STEP == 0 else 1
    chunk = clen * cps
    nc = t // chunk
    smallt3 = smallt.reshape(smallt.shape[0], b * nc, chunk).transpose(1, 0, 2)
    keep = CONV_W - 1
    return pl.pallas_call(
        functools.partial(_gdn_kernel, chunk=clen, cps=cps, nsteps=nc), grid=(b, nc),
        in_specs=[
            pl.BlockSpec((chunk, CONV_DIM), lambda i, j: (i * nc + j, 0)),
            pl.BlockSpec((chunk, LANES), lambda i, j: (i * nc + j, 0)),
            pl.BlockSpec((None, smallt.shape[0], chunk), lambda i, j: (i * nc + j, 0, 0)),
            pl.BlockSpec((chunk, V_B), lambda i, j: (i * nc + j, 0)),
            pl.BlockSpec((None, H_B * DK_B, DV_B), lambda i, j: (i, 0, 0)),
            pl.BlockSpec((None, keep, CONV_DIM), lambda i, j: (i, 0, 0)),
            pl.BlockSpec((CONV_W, CONV_DIM), lambda i, j: (0, 0)),
            pl.BlockSpec((1, DV_B), lambda i, j: (0, 0)),
        ],
        out_specs=[
            pl.BlockSpec((chunk, V_B), lambda i, j: (i * nc + j, 0)),
            pl.BlockSpec((None, H_B * DK_B, DV_B), lambda i, j: (i, 0, 0)),
            pl.BlockSpec((None, keep, CONV_DIM), lambda i, j: (i, 0, 0)),
        ],
        out_shape=[jax.ShapeDtypeStruct((b * t, V_B), BF16),
                   jax.ShapeDtypeStruct((b, H_B * DK_B, DV_B), F32),
                   jax.ShapeDtypeStruct((b, keep, CONV_DIM), F32)],
        scratch_shapes=[pltpu.VMEM((H_B * DK_B, DV_B), F32), pltpu.VMEM((8 + chunk, CONV_DIM), F32)],
        compiler_params=_params("arbitrary", "arbitrary"), name="gdn_chunked",
    )(u, small, smallt3, z, s0, conv0, w_conv, g_onorm)


def _merge_kernel(x_ref, ya_ref, yb_ref, gate_ref, wa_ref, wb_ref, wo_ref, gf_ref, wr_ref, br_ref,
                  x1_ref, h2_ref, logit_ref):
    br_a = jnp.dot(ya_ref[...], wa_ref[...], preferred_element_type=F32)
    br_b = jnp.dot(yb_ref[...], wb_ref[...], preferred_element_type=F32)
    merged = gate_ref[:, :D_MODEL] * br_a + gate_ref[:, D_MODEL:] * br_b
    x1 = x_ref[...] + _dot(merged, wo_ref[...])
    x1_ref[...] = x1
    h2 = _rms_rows(x1, gf_ref[...])
    h2_ref[...] = h2
    logit_ref[...] = _dot(h2, wr_ref[...]) + br_ref[...]


def _route_kernel(logit_ref, meta_ref, wgt_ref, cnt_ref, run, start, *, block):
    sweep, step = pl.program_id(0), pl.program_id(1)

    @pl.when(jnp.logical_and(sweep == 0, step == 0))
    def _():
        run[...] = jnp.zeros_like(run)
        start[...] = jnp.zeros_like(start)

    @pl.when(jnp.logical_and(sweep == 1, step == 0))
    def _():
        counts = run[...]
        cnt_ref[...] = counts
        padded = jnp.floor((counts + (block - 1)) * (1.0 / block)) * block
        before_e = (_iota((LANES, LANES), 0) < _iota((LANES, LANES), 1)).astype(BF16)
        start[...] = _dot_exact_rhs(jnp.broadcast_to(padded, (8, LANES)), before_e, 3)[0:1, :]
        run[...] = jnp.zeros_like(run)

    x = logit_ref[...]
    tm = x.shape[0]
    lane = _iota((1, LANES), 1)
    lanef = lane.astype(F32)
    big = float(LANES)

    def first_argmax(vals, valid):
        mx = jnp.max(vals, axis=-1, keepdims=True)
        idx = jnp.min(jnp.where(jnp.logical_and(vals == mx, valid), lanef, big), axis=-1, keepdims=True)
        return mx, idx.astype(I32)

    gmask = lane < N_GROUPS
    gmax, gsel = first_argmax(jnp.where(gmask, x, NEG), gmask)
    p_g = 1.0 / jnp.sum(jnp.where(gmask, jnp.exp(x - gmax), 0.0), axis=-1, keepdims=True)
    first = N_GROUPS + EXPERTS_PER_GROUP * gsel
    emask = jnp.logical_and(lane >= first, lane < first + EXPERTS_PER_GROUP)
    ev = jnp.where(emask, x, NEG)
    m1, i1 = first_argmax(ev, emask)
    emask2 = jnp.logical_and(emask, lane != i1)
    m2, i2 = first_argmax(jnp.where(emask2, x, NEG), emask2)
    e21 = jnp.exp(m2 - m1)
    w1 = p_g / (1.0 + e21)
    w2 = p_g * e21 / (1.0 + e21)
    e1, e2 = i1 - N_GROUPS, i2 - N_GROUPS
    oh1, oh2 = lane == e1, lane == e2
    onehot = jnp.logical_or(oh1, oh2).astype(BF16)
    strict = (_iota((tm, tm), 0) > _iota((tm, tm), 1)).astype(BF16)
    slot = jnp.dot(strict, onehot, preferred_element_type=F32) + run[...] + start[...]
    run[...] = run[...] + jnp.sum(onehot.astype(F32), axis=0, keepdims=True)

    @pl.when(sweep == 1)
    def _():
        d1 = jnp.sum(jnp.where(oh1, slot, 0.0), axis=-1, keepdims=True).astype(I32)
        d2 = jnp.sum(jnp.where(oh2, slot, 0.0), axis=-1, keepdims=True).astype(I32)
        l8 = _iota((1, 8), 1)
        meta_ref[...] = jnp.where(l8 == 0, e1, jnp.where(l8 == 1, e2, jnp.where(l8 == 2, d1, jnp.where(l8 == 3, d2, 0))))
        wgt_ref[...] = jnp.where(l8 == 0, w1, jnp.where(l8 == 1, w2, 0.0))


def _row_copy(src, dst, sem, s, d):
    return pltpu.make_async_copy(src.at[pl.ds(s, 1)], dst.at[pl.ds(d, 1)], sem)


DMA_UNROLL = 8


DISPATCH_SLOTS = 3


def _dispatch_kernel(seg_ref, dest_ref, h_ref, xs_ref, buf, zrow, sem_in, sem_out, sem_pad, *, td, nsteps):
    step = pl.program_id(0)
    slot = lax.rem(step, DISPATCH_SLOTS)

    def tile_copy(tile, s):
        return pltpu.make_async_copy(h_ref.at[pl.ds(tile * td, td)], buf.at[s], sem_in.at[s])

    def drain(s):
        def body(t, carry):
            _row_copy(buf.at[s], xs_ref, sem_out.at[s], 0, 0).wait()
            _row_copy(buf.at[s], xs_ref, sem_out.at[s], 0, 0).wait()
            return carry
        lax.fori_loop(0, td, body, 0, unroll=DMA_UNROLL)

    @pl.when(step == 0)
    def _():
        tile_copy(0, 0).start()

    @pl.when(step >= 2)
    def _():
        drain(lax.rem(step + 1, DISPATCH_SLOTS))

    @pl.when(step + 1 < nsteps)
    def _():
        tile_copy(step + 1, lax.rem(step + 1, DISPATCH_SLOTS)).start()

    tile_copy(step, slot).wait()

    def issue(t, carry):
        _row_copy(buf.at[slot], xs_ref, sem_out.at[slot], t, dest_ref[0, t]).start(priority=0)
        _row_copy(buf.at[slot], xs_ref, sem_out.at[slot], t, dest_ref[0, td + t]).start(priority=1)
        return carry

    lax.fori_loop(0, td, issue, 0, unroll=DMA_UNROLL)

    @pl.when(step == nsteps - 1)
    def _():
        if nsteps >= 2:
            drain(lax.rem(step + 2, DISPATCH_SLOTS))
        drain(slot)
        zrow[...] = jnp.zeros_like(zrow)
        for e in range(N_EXPERTS):
            lax.fori_loop(seg_ref[e], seg_ref[N_EXPERTS + e],
                          lambda r, c: (_row_copy(zrow, xs_ref, sem_pad, 0, r).start(), c)[1], 0)
        for e in range(N_EXPERTS):
            lax.fori_loop(seg_ref[e], seg_ref[N_EXPERTS + e],
                          lambda r, c: (_row_copy(zrow, xs_ref, sem_pad, 0, 0).wait(), c)[1], 0)
        buf[0] = jnp.zeros((td, D_MODEL), F32)
        tail = lambda c: pltpu.make_async_copy(buf.at[0], xs_ref.at[pl.ds(c * td, td)], sem_pad)
        first, stop = lax.div(seg_ref[2 * N_EXPERTS - 1], td), xs_ref.shape[0] // td
        lax.fori_loop(first, stop, lambda c, x: (tail(c).start(), x)[1], 0)
        lax.fori_loop(first, stop, lambda c, x: (tail(0).wait(), x)[1], 0)


def _expert_kernel(be_ref, nu_ref, x_ref, w1_ref, w3_ref, w2_ref, y_ref, w1_sc, w3_sc, w2_sc):
    blk = pl.program_id(0)
    used = blk < nu_ref[0]
    new_expert = jnp.logical_or(blk == 0, be_ref[blk] != be_ref[jnp.maximum(blk - 1, 0)])

    @pl.when(jnp.logical_and(used, new_expert))
    def _():
        w1_sc[...] = w1_ref[...].astype(BF16)
        w3_sc[...] = w3_ref[...].astype(BF16)
        w2_sc[...] = w2_ref[...].astype(BF16)

    @pl.when(used)
    def _():
        x = x_ref[...].astype(BF16)
        a = jnp.dot(x, w1_sc[...], preferred_element_type=F32)
        g = jnp.dot(x, w3_sc[...], preferred_element_type=F32)
        y_ref[...] = _dot(_silu(a) * g, w2_sc[...])

    @pl.when(jnp.logical_not(used))
    def _():
        y_ref[...] = jnp.zeros_like(y_ref)


def _combine_kernel(dest_ref, next_ref, x1_ref, wgt_ref, yb_ref, o_ref, g0, g1, sems, *, tc):
    step, nsteps = pl.program_id(0), pl.num_programs(0)
    slot = step % 2

    def issue(idx_ref, s):
        def body(t, carry):
            _row_copy(yb_ref, g0.at[s], sems.at[s], idx_ref[0, t], t).start(priority=0)
            _row_copy(yb_ref, g1.at[s], sems.at[s], idx_ref[0, tc + t], t).start(priority=1)
            return carry
        lax.fori_loop(0, tc, body, 0, unroll=DMA_UNROLL)

    @pl.when(step == 0)
    def _():
        issue(dest_ref, 0)

    @pl.when(step + 1 < nsteps)
    def _():
        issue(next_ref, 1 - slot)

    def drain(t, carry):
        _row_copy(yb_ref, g0.at[slot], sems.at[slot], 0, 0).wait()
        _row_copy(yb_ref, g1.at[slot], sems.at[slot], 0, 0).wait()
        return carry

    lax.fori_loop(0, tc, drain, 0, unroll=DMA_UNROLL)
    wg = wgt_ref[...]
    o_ref[...] = x1_ref[...] + (g0[slot] * wg[:, 0:1] + g1[slot] * wg[:, 1:2])


def _hier_moe(x1, h2, logits, w_e1, w_e3, w_e2):
    n = x1.shape[0]
    tr = _pick_tile(n, 512)
    bm = MOE_BLOCK if 2 * n >= 4 * MOE_BLOCK * N_EXPERTS else MOE_BLOCK // 4
    assert bm & (bm - 1) == 0
    meta, wgt, cnt = pl.pallas_call(
        functools.partial(_route_kernel, block=bm), grid=(2, n // tr),
        in_specs=[pl.BlockSpec((tr, LANES), lambda s, i: (i, 0))],
        out_specs=[pl.BlockSpec((tr, 8), lambda s, i: (i * s, 0)), pl.BlockSpec((tr, 8), lambda s, i: (i * s, 0)),
                   pl.BlockSpec((1, LANES), lambda s, i: (0, 0))],
        out_shape=[jax.ShapeDtypeStruct((n, 8), I32), jax.ShapeDtypeStruct((n, 8), F32),
                   jax.ShapeDtypeStruct((1, LANES), F32)],
        scratch_shapes=[pltpu.VMEM((1, LANES), F32), pltpu.VMEM((1, LANES), F32)],
        compiler_params=_params("arbitrary", "arbitrary"), name="moe_route")(logits)

    counts = cnt[0, :N_EXPERTS].astype(I32)
    padded = ((counts + bm - 1) // bm) * bm
    pad_end = jnp.cumsum(padded)
    nblk = -(-(2 * n) // bm) + N_EXPERTS
    n_used = (pad_end[-1] // bm).astype(I32).reshape(1)
    blk_first = jnp.arange(nblk, dtype=I32) * bm
    blk_expert = jnp.minimum(jnp.sum((pad_end[None, :] <= blk_first[:, None]).astype(I32), axis=1), N_EXPERTS - 1)
    seg = jnp.concatenate([pad_end - padded + counts, pad_end])

    td = _pick_tile(n, min(256, bm))
    assert bm % td == 0
    dest = meta[:, 2:4].reshape(n // td, td, 2).transpose(0, 2, 1).reshape(n // td, 1, 2 * td)
    xs = pl.pallas_call(
        functools.partial(_dispatch_kernel, td=td, nsteps=n // td),
        grid_spec=pltpu.PrefetchScalarGridSpec(
            num_scalar_prefetch=1, grid=(n // td,),
            in_specs=[pl.BlockSpec((None, 1, 2 * td), lambda i, sg: (i, 0, 0), memory_space=pltpu.SMEM),
                      pl.BlockSpec(memory_space=pl.ANY)],
            out_specs=pl.BlockSpec(memory_space=pl.ANY),
            scratch_shapes=[pltpu.VMEM((DISPATCH_SLOTS, td, D_MODEL), F32), pltpu.VMEM((8, D_MODEL), F32),
                            pltpu.SemaphoreType.DMA((DISPATCH_SLOTS,)), pltpu.SemaphoreType.DMA((DISPATCH_SLOTS,)),
                            pltpu.SemaphoreType.DMA(())]),
        out_shape=jax.ShapeDtypeStruct((nblk * bm, D_MODEL), F32),
        compiler_params=_params("arbitrary"), name="moe_dispatch")(seg, dest, h2)

    xmap = lambda i, be, nu: (jnp.minimum(i, nu[0] - 1), 0)
    grid_spec = pltpu.PrefetchScalarGridSpec(
        num_scalar_prefetch=2, grid=(nblk,),
        in_specs=[pl.BlockSpec((bm, D_MODEL), xmap),
                  pl.BlockSpec((None, D_MODEL, D_EXPERT), lambda i, be, nu: (be[i], 0, 0)),
                  pl.BlockSpec((None, D_MODEL, D_EXPERT), lambda i, be, nu: (be[i], 0, 0)),
                  pl.BlockSpec((None, D_EXPERT, D_MODEL), lambda i, be, nu: (be[i], 0, 0))],
        out_specs=pl.BlockSpec((bm, D_MODEL), lambda i, be, nu: (i, 0)),
        scratch_shapes=[pltpu.VMEM((D_MODEL, D_EXPERT), BF16), pltpu.VMEM((D_MODEL, D_EXPERT), BF16),
                        pltpu.VMEM((D_EXPERT, D_MODEL), BF16)])
    yb = pl.pallas_call(
        _expert_kernel, grid_spec=grid_spec, out_shape=jax.ShapeDtypeStruct((nblk * bm, D_MODEL), F32),
        compiler_params=_params("arbitrary"), name="moe_experts")(blk_expert, n_used, xs, w_e1, w_e3, w_e2)

    tc = td
    return pl.pallas_call(
        functools.partial(_combine_kernel, tc=tc), grid=(n // tc,),
        in_specs=[pl.BlockSpec((None, 1, 2 * tc), lambda i: (i, 0, 0), memory_space=pltpu.SMEM),
                  pl.BlockSpec((None, 1, 2 * tc), lambda i: (jnp.minimum(i + 1, n // tc - 1), 0, 0),
                               memory_space=pltpu.SMEM),
                  pl.BlockSpec((tc, D_MODEL), lambda i: (i, 0)),
                  pl.BlockSpec((tc, 8), lambda i: (i, 0)),
                  pl.BlockSpec(memory_space=pl.ANY)],
        out_specs=pl.BlockSpec((tc, D_MODEL), lambda i: (i, 0)),
        out_shape=jax.ShapeDtypeStruct((n, D_MODEL), F32),
        scratch_shapes=[pltpu.VMEM((2, tc, D_MODEL), F32), pltpu.VMEM((2, tc, D_MODEL), F32),
                        pltpu.SemaphoreType.DMA((2,))],
        compiler_params=_params("arbitrary"), name="moe_combine")(dest, dest, x1, wgt, yb)


def _prep_weights(l, g_mix, w_in, b_f, g_qnorm, g_knorm, w_conv, a_log, dt_bias, g_onorm, b_gate,
                  w_br_a, w_br_b, w_out, g_ffn, w_grp, b_grp, w_exp, b_exp, w_e1, w_e3, w_e2):
    w = w_in[l]
    offs = [0]
    for s in (WA, WA, WA, H_A, QK_B, QK_B, V_B, H_B, H_B, V_B, D_MODEL, D_MODEL):
        offs.append(offs[-1] + s)
    col = lambda i, j=None: w[:, offs[i]:offs[(i if j is None else j) + 1]]
    n_small = H_A + 2 * H_B
    w_small = jnp.concatenate([col(3), col(7), col(8)], axis=1)
    zeros8 = jnp.zeros((H_A,), F32)
    bias = jnp.concatenate([b_f[l], zeros8, dt_bias[l]])
    alog = jnp.concatenate([zeros8, zeros8, a_log[l]])
    pad = lambda v, n: jnp.pad(v, (0, n - v.shape[0]))
    rows_t = 32
    return dict(
        g_mix=g_mix[l][None, :],
        wq=col(0).astype(BF16), wkt=col(1).T.astype(BF16), wvt=col(2).T.astype(BF16),
        gq=jnp.tile(g_qnorm[l], H_A)[None, :], gkt=jnp.tile(g_knorm[l], H_A)[:, None],
        ws=jnp.pad(w_small, ((0, 0), (0, LANES - n_small))).astype(BF16),
        wst=jnp.pad(w_small.T, ((0, rows_t - n_small), (0, 0))).astype(BF16),
        wu=col(4, 6).astype(BF16),
        bias=pad(bias, LANES)[None, :], alog=pad(alog, LANES)[None, :],
        biast=pad(bias, rows_t)[:, None], alogt=pad(alog, rows_t)[:, None],
        wz=col(9).astype(BF16), wg=col(10, 11).astype(BF16), bg=b_gate[l][None, :],
        w_conv=w_conv[l], g_onorm=g_onorm[l][None, :],
        w_br_a=w_br_a[l].astype(BF16), w_br_b=w_br_b[l].astype(BF16), w_out=w_out[l].astype(BF16),
        g_ffn=g_ffn[l][None, :],
        wr=jnp.pad(jnp.concatenate([w_grp[l], w_exp[l]], axis=1), ((0, 0), (0, LANES - N_GROUPS - N_EXPERTS))).astype(BF16),
        br=pad(jnp.concatenate([b_grp[l], b_exp[l]]), LANES)[None, :],
        w_e1=w_e1[l], w_e3=w_e3[l], w_e2=w_e2[l],
    )


def _layer(x, past_k, past_v, past_logf, s0, conv0, p):
    b, t, _ = x.shape
    past = past_k.shape[1]
    n = b * t
    xf = x.reshape(n, D_MODEL)
    tm = _pick_tile(n, 512)
    tm_in = _pick_tile(n, 1024)

    q16, kt32, kt16, vt32, vt16 = _inproj_fox(xf, p, b, t)
    small, smallt, u = _row_call(
        _inproj_gdn_kernel, n, tm_in, [xf],
        [p["g_mix"], p["ws"], p["wst"], p["wu"], p["bias"], p["alog"], p["biast"], p["alogt"]],
        [((n, LANES), F32, False), ((32, n), F32, True), ((n, CONV_DIM), F32, False)], "inproj_gdn")
    z, gates = _row_call(
        _inproj_gate_kernel, n, tm_in, [xf], [p["g_mix"], p["wz"], p["wg"], p["bg"]],
        [((n, V_B), F32, False), ((n, 2 * D_MODEL), F32, False)], "inproj_gate")

    logf = small[:, :H_A].reshape(b, t, H_A)
    past_lanes = jnp.pad(past_logf.astype(F32), ((0, 0), (0, 0), (0, LANES - H_A)))
    k_aug, q_aug = _forget_companions(jnp.concatenate([past_lanes, small.reshape(b, t, LANES)], axis=1))
    feature_major = lambda c: c.transpose(0, 2, 3, 1).reshape(b, WA, past).astype(BF16)
    kt_all = jnp.concatenate([feature_major(past_k), kt16], axis=2)
    vt_all = jnp.concatenate([feature_major(past_v), vt16], axis=2)
    y_a = _fox_attention(q16, q_aug[:, past:].reshape(n, 2 * LANES), kt_all, k_aug, vt_all, b, t, past)

    y_b, s_new, conv_new = _gdn(u, small, smallt, z, s0.reshape(b, H_B * DK_B, DV_B), conv0,
                                p["w_conv"], p["g_onorm"], b, t)

    x1, h2, logits = _row_call(
        _merge_kernel, n, tm, [xf, y_a, y_b, gates],
        [p["w_br_a"], p["w_br_b"], p["w_out"], p["g_ffn"], p["wr"], p["br"]],
        [((n, D_MODEL), F32, False), ((n, D_MODEL), F32, False), ((n, LANES), F32, False)], "merge_outproj")
    x2 = _hier_moe(x1, h2, logits, p["w_e1"], p["w_e3"], p["w_e2"])
    token_major = lambda c: c.reshape(b, H_A, DH_A, t).transpose(0, 3, 1, 2)
    return (x2.reshape(b, t, D_MODEL), token_major(kt32), token_major(vt32), logf,
            s_new.reshape(b, H_B, DK_B, DV_B), conv_new)


def kernel(x_prompt, x_sample, cache_fox_k, cache_fox_v, cache_fox_logf, state_gdn, state_gdn_conv, g_mix, w_in, b_f, g_qnorm, g_knorm, w_conv, a_log, dt_bias, g_onorm, b_gate, w_br_a, w_br_b, w_out, g_ffn, w_grp, b_grp, w_exp, b_exp, w_e1, w_e3, w_e2):
    depth = w_in.shape[0]
    bp = x_prompt.shape[0]
    yp, ys = x_prompt, x_sample
    outs_p, outs_s = [], []
    for l in range(depth):
        p = _prep_weights(l, g_mix, w_in, b_f, g_qnorm, g_knorm, w_conv, a_log, dt_bias, g_onorm, b_gate,
                          w_br_a, w_br_b, w_out, g_ffn, w_grp, b_grp, w_exp, b_exp, w_e1, w_e3, w_e2)
        ys, *rest_s = _layer(ys, cache_fox_k[l], cache_fox_v[l], cache_fox_logf[l], state_gdn[l], state_gdn_conv[l], p)
        yp, *rest_p = _layer(
            yp, jnp.zeros((bp, 0, H_A, DH_A), F32), jnp.zeros((bp, 0, H_A, DH_A), F32), jnp.zeros((bp, 0, H_A), F32),
            jnp.zeros((bp, H_B, DK_B, DV_B), F32), jnp.zeros((bp, CONV_W - 1, CONV_DIM), F32), p)
        outs_p.append(rest_p)
        outs_s.append(rest_s)
    stack = lambda outs, i: jnp.stack([o[i] for o in outs])
    return (yp, ys,
            stack(outs_p, 0), stack(outs_p, 1), stack(outs_p, 2), stack(outs_p, 3), stack(outs_p, 4),
            stack(outs_s, 0), stack(outs_s, 1), stack(outs_s, 2), stack(outs_s, 3), stack(outs_s, 4))
```

```python
import functools
import math

import jax
import jax.numpy as jnp
import numpy as np
from jax import lax
from jax.experimental import pallas as pl
from jax.experimental.pallas import tpu as pltpu

F32, BF16, I32, U32 = jnp.float32, jnp.bfloat16, jnp.int32, jnp.uint32

D_MODEL = 1024
H_A, DH_A = 8, 64
WA = H_A * DH_A
H_B, DK_B, DV_B = 8, 64, 128
QK_B, V_B = H_B * DK_B, H_B * DV_B
CONV_W = 4
CONV_DIM = 2 * QK_B + V_B
N_GROUPS, EXPERTS_PER_GROUP = 4, 8
N_EXPERTS = N_GROUPS * EXPERTS_PER_GROUP
D_EXPERT = 512
GDN_CHUNK = 64
GDN_CHUNKS_PER_STEP = 4
EPS = 1e-6
LOG2E = math.log2(math.e)
NEG = -1e30
LANES = 128
V7X_VMEM_LIMIT = 56 * 1024 * 1024
MOE_BLOCK = 512


def _params(*sem):
    return pltpu.CompilerParams(dimension_semantics=sem, vmem_limit_bytes=V7X_VMEM_LIMIT)


def _pick_tile(n, cap, mult=8):
    if n <= cap:
        return n
    for t in range(cap - cap % mult, 0, -mult):
        if n % t == 0:
            return t
    return n


def _dot(a, b):
    return jnp.dot(a.astype(BF16), b.astype(BF16), preferred_element_type=F32)


def _dot_t(a, b):
    return lax.dot_general(a.astype(BF16), b.astype(BF16), (((1,), (1,)), ((), ())), preferred_element_type=F32)


def _dot_tl(a, b):
    return lax.dot_general(a.astype(BF16), b.astype(BF16), (((0,), (0,)), ((), ())), preferred_element_type=F32)


def _split(x, n):
    parts = []
    for _ in range(n):
        p = x.astype(BF16)
        parts.append(p)
        x = x - p.astype(F32)
    return parts


def _dot_exact_lhs(ones, x, n):
    acc = None
    for p in _split(x, n):
        t = jnp.dot(ones, p, preferred_element_type=F32)
        acc = t if acc is None else acc + t
    return acc


def _dot_exact_rhs(x, ones, n):
    acc = None
    for p in _split(x, n):
        t = jnp.dot(p, ones, preferred_element_type=F32)
        acc = t if acc is None else acc + t
    return acc


def _iota(shape, dim):
    return lax.broadcasted_iota(I32, shape, dim)


def _softplus(x):
    return jnp.maximum(x, 0.0) + jnp.log1p(jnp.exp(-jnp.abs(x)))


def _silu(x):
    return x * jax.nn.sigmoid(x)


def _rms_rows(x, g):
    return x * lax.rsqrt(jnp.mean(x * x, axis=-1, keepdims=True) + EPS) * g


def _head_ones(width, head):
    shift = head.bit_length() - 1
    assert head == 1 << shift
    r = jnp.right_shift(_iota((width, width), 0), shift)
    c = jnp.right_shift(_iota((width, width), 1), shift)
    return (r == c).astype(BF16)


def _group_sumsq(y, head):
    ones = _head_ones(2 * LANES, head)
    sq = y * y
    parts = [_dot_exact_rhs(sq[:, c:c + 2 * LANES], ones, 2) for c in range(0, y.shape[1], 2 * LANES)]
    return parts[0] if len(parts) == 1 else jnp.concatenate(parts, axis=1)


def _inproj_fox_kernel(x_ref, g_ref, wq_ref, wkt_ref, wvt_ref, gq_ref, gkt_ref,
                       q_ref, kt32_ref, kt16_ref, vt32_ref, vt16_ref):
    h = _rms_rows(x_ref[...], g_ref[...]).astype(BF16)
    q = jnp.dot(h, wq_ref[...], preferred_element_type=F32)
    q = q * lax.rsqrt(_group_sumsq(q, DH_A) * (1.0 / DH_A) + EPS) * gq_ref[...]
    q_ref[...] = (q * (DH_A ** -0.5 * LOG2E)).astype(BF16)
    kt = lax.dot_general(wkt_ref[...], h, (((1,), (1,)), ((), ())), preferred_element_type=F32)
    ones = _head_ones(2 * LANES, DH_A)
    sq = kt * kt
    ms = jnp.concatenate([_dot_exact_lhs(ones, sq[r:r + 2 * LANES, :], 2) for r in range(0, WA, 2 * LANES)], axis=0)
    kt = kt * lax.rsqrt(ms * (1.0 / DH_A) + EPS) * gkt_ref[...]
    kt32_ref[...] = kt
    kt16_ref[...] = kt.astype(BF16)
    vt = lax.dot_general(wvt_ref[...], h, (((1,), (1,)), ((), ())), preferred_element_type=F32)
    vt32_ref[...] = vt
    vt16_ref[...] = vt.astype(BF16)


def _inproj_fox(xf, p, b, t):
    n = b * t
    tm = _pick_tile(t, 512, LANES)
    per = t // tm
    row = lambda i: (i, 0)
    col = lambda i: (i // per, 0, i % per)
    full = [p["g_mix"], p["wq"], p["wkt"], p["wvt"], p["gq"], p["gkt"]]
    return pl.pallas_call(
        _inproj_fox_kernel, grid=(n // tm,),
        in_specs=[pl.BlockSpec((tm, D_MODEL), row)] + [pl.BlockSpec(a.shape, lambda i: (0, 0)) for a in full],
        out_specs=[pl.BlockSpec((tm, WA), row)] + [pl.BlockSpec((None, WA, tm), col)] * 4,
        out_shape=[jax.ShapeDtypeStruct((n, WA), BF16), jax.ShapeDtypeStruct((b, WA, t), F32),
                   jax.ShapeDtypeStruct((b, WA, t), BF16), jax.ShapeDtypeStruct((b, WA, t), F32),
                   jax.ShapeDtypeStruct((b, WA, t), BF16)],
        compiler_params=_params("arbitrary"), name="inproj_fox")(xf, *full)


def _small_epilogue(y, bias, a_log, idx):
    yb = y + bias
    logf = -_softplus(-yb)
    beta = jax.nn.sigmoid(y)
    g = -jnp.exp(a_log) * _softplus(yb)
    return jnp.where(idx < H_A, logf, jnp.where(idx < H_A + H_B, beta, g))


def _inproj_gdn_kernel(x_ref, g_ref, ws_ref, wst_ref, wu_ref, bias_ref, alog_ref, biast_ref, alogt_ref,
                       small_ref, smallt_ref, u_ref):
    h = _rms_rows(x_ref[...], g_ref[...]).astype(BF16)
    ys = jnp.dot(h, ws_ref[...], preferred_element_type=F32)
    small_ref[...] = _small_epilogue(ys, bias_ref[...], alog_ref[...], _iota(ys.shape, 1))
    yt = lax.dot_general(wst_ref[...], h, (((1,), (1,)), ((), ())), preferred_element_type=F32)
    smallt_ref[...] = _small_epilogue(yt, biast_ref[...], alogt_ref[...], _iota(yt.shape, 0))
    for c in range(0, CONV_DIM, 512):
        u_ref[:, c:c + 512] = jnp.dot(h, wu_ref[:, c:c + 512], preferred_element_type=F32)


def _inproj_gate_kernel(x_ref, g_ref, wz_ref, wg_ref, bg_ref, zs_ref, gate_ref):
    h = _rms_rows(x_ref[...], g_ref[...]).astype(BF16)
    for c in range(0, V_B, 512):
        zs_ref[:, c:c + 512] = _silu(jnp.dot(h, wz_ref[:, c:c + 512], preferred_element_type=F32)).astype(BF16)
    for c in range(0, 2 * D_MODEL, 512):
        y = jnp.dot(h, wg_ref[:, c:c + 512], preferred_element_type=F32)
        gate_ref[:, c:c + 512] = jax.nn.sigmoid(y + bg_ref[:, c:c + 512]).astype(BF16)


def _row_call(kernel, n, tm, row_inputs, full_inputs, outs, name):
    in_specs = [pl.BlockSpec((tm, a.shape[1]), lambda i: (i, 0)) for a in row_inputs]
    in_specs += [pl.BlockSpec(a.shape, lambda i, nd=a.ndim: (0,) * nd) for a in full_inputs]
    out_specs, out_shapes = [], []
    for shape, dtype, transposed in outs:
        if transposed:
            out_specs.append(pl.BlockSpec((shape[0], tm), lambda i: (0, i)))
        else:
            out_specs.append(pl.BlockSpec((shape[0] // n * tm, shape[1]), lambda i: (i, 0)))
        out_shapes.append(jax.ShapeDtypeStruct(shape, dtype))
    return pl.pallas_call(kernel, grid=(n // tm,), in_specs=in_specs, out_specs=out_specs, out_shape=out_shapes,
                          compiler_params=_params("arbitrary"), name=name)(*row_inputs, *full_inputs)


AUG_PIECES = 3
AUG_GROUP = 16


def _companion_placement():
    n = AUG_PIECES
    place_k = np.zeros((n, LANES, LANES), np.float32)
    place_q = np.zeros((n, LANES, 2 * LANES), np.float32)
    ones_k = np.zeros((LANES, 1), np.float32)
    ones_q = np.zeros((1, 2 * LANES), np.float32)
    for hd in range(H_A):
        pair, odd = divmod(hd, 2)
        g = AUG_GROUP * pair
        for k in range(n):
            place_k[k, g + n * odd + k, hd] = -1.0
            place_q[k, hd, odd * LANES + g + 2 * n + k] = 1.0
            ones_k[g + 2 * n + k, 0] = 1.0
            ones_q[0, odd * LANES + g + n * odd + k] = 1.0
    return jnp.asarray(place_k, BF16), jnp.asarray(place_q, BF16), jnp.asarray(ones_k, F32), jnp.asarray(ones_q, F32)


def _cumsum_kernel(x_ref, pk_ref, pq_ref, ok_ref, oq_ref, kaug_ref, qaug_ref, carry):
    @pl.when(pl.program_id(1) == 0)
    def _():
        carry[...] = jnp.zeros_like(carry)

    x = x_ref[...]
    t = x.shape[0]
    sub = _pick_tile(t, 512)
    tril = (_iota((sub, sub), 0) >= _iota((sub, sub), 1)).astype(BF16)
    last, blocks = carry[...], []
    for r in range(0, t, sub):
        blocks.append(_dot_exact_lhs(tril, x[r:r + sub, :], 3) + last)
        last = blocks[-1][sub - 1:sub, :]
    carry[...] = last
    c = blocks[0] if len(blocks) == 1 else jnp.concatenate(blocks, axis=0)
    slab_k, slab_q = ok_ref[...], oq_ref[...]
    for k, piece in enumerate(_split(c * LOG2E, AUG_PIECES)):
        slab_k = slab_k + lax.dot_general(pk_ref[k], piece, (((1,), (1,)), ((), ())), preferred_element_type=F32)
        slab_q = slab_q + jnp.dot(piece, pq_ref[k], preferred_element_type=F32)
    kaug_ref[...] = slab_k.astype(BF16)
    qaug_ref[...] = slab_q.astype(BF16)


def _forget_companions(x):
    b, tt, c = x.shape
    tc = _pick_tile(tt, 512, LANES)
    consts = _companion_placement()
    return pl.pallas_call(
        _cumsum_kernel, grid=(b, tt // tc),
        in_specs=[pl.BlockSpec((None, tc, c), lambda i, j: (i, j, 0))]
                 + [pl.BlockSpec(a.shape, lambda i, j, nd=a.ndim: (0,) * nd) for a in consts],
        out_specs=[pl.BlockSpec((None, LANES, tc), lambda i, j: (i, 0, j)),
                   pl.BlockSpec((None, tc, 2 * LANES), lambda i, j: (i, j, 0))],
        out_shape=[jax.ShapeDtypeStruct((b, LANES, tt), BF16), jax.ShapeDtypeStruct((b, tt, 2 * LANES), BF16)],
        scratch_shapes=[pltpu.VMEM((1, c), F32)],
        compiler_params=_params("arbitrary", "arbitrary"), name="fox_cumsum")(x, *consts)


def _attn_kernel(qi_ref, kj_ref, last_ref, q_ref, qa_ref, kt_ref, kat_ref, vt_ref, o_ref, qq_sc, m_sc, l_sc, acc_sc,
                 *, tq, tk, past):
    n = pl.program_id(1)
    i, j = qi_ref[n], kj_ref[n]
    lo = _iota((1, LANES), 1) < DH_A
    halves = (lo, jnp.logical_not(lo))
    row_lo = _iota((LANES, 1), 0) < DH_A
    row_halves = (row_lo, jnp.logical_not(row_lo))

    @pl.when(j == 0)
    def _():
        m_sc[...] = jnp.full_like(m_sc, NEG)
        l_sc[...] = jnp.zeros_like(l_sc)
        acc_sc[...] = jnp.zeros_like(acc_sc)
        lane = _iota((1, LANES), 1)
        for hd in range(H_A):
            pair, odd = divmod(hd, 2)
            q2 = q_ref[:, pair * LANES:(pair + 1) * LANES]
            qq_sc[hd, :, :LANES] = jnp.where(halves[odd], q2, jnp.zeros_like(q2))
            comp = qa_ref[:, odd * LANES:(odd + 1) * LANES]
            own = jnp.logical_and(lane >= AUG_GROUP * pair, lane < AUG_GROUP * (pair + 1))
            qq_sc[hd, :, LANES:] = jnp.where(own, comp, jnp.zeros_like(comp))

    def step(masked):
        if masked:
            visible = j * tk + _iota((1, tk), 1) <= past + i * tq + _iota((tq, 1), 0)
        scores = []
        for p in range(H_A // 2):
            kk = jnp.concatenate([kt_ref[p * LANES:(p + 1) * LANES, :], kat_ref[...]], axis=0)
            scores += [jnp.dot(qq_sc[2 * p + half], kk, preferred_element_type=F32) for half in range(2)]
        probs, alphas = [], []
        for hd in range(H_A):
            s = jnp.where(visible, scores[hd], NEG) if masked else scores[hd]
            m_prev = m_sc[hd]
            m_new = jnp.maximum(m_prev, jnp.max(s, axis=-1, keepdims=True))
            alpha = jnp.exp2(m_prev - m_new)
            pr = jnp.exp2(s - m_new)
            l_sc[hd] = alpha * l_sc[hd] + jnp.sum(pr, axis=-1, keepdims=True)
            m_sc[hd] = m_new
            alphas.append(alpha)
            probs.append(pr.astype(BF16))
        for p in range(H_A // 2):
            v2 = vt_ref[p * LANES:(p + 1) * LANES, :]
            pvs = [_dot_t(probs[2 * p + half], jnp.where(row_halves[half], v2, jnp.zeros_like(v2))) for half in range(2)]
            acc_sc[p] = jnp.where(lo, alphas[2 * p], alphas[2 * p + 1]) * acc_sc[p] + pvs[0] + pvs[1]

    some_hidden = (j + 1) * tk - 1 > past + i * tq
    pl.when(some_hidden)(functools.partial(step, True))
    pl.when(jnp.logical_not(some_hidden))(functools.partial(step, False))

    @pl.when(last_ref[n] == 1)
    def _():
        for p in range(H_A // 2):
            l2 = jnp.where(lo, l_sc[2 * p], l_sc[2 * p + 1])
            o_ref[:, p * LANES:(p + 1) * LANES] = (acc_sc[p] / l2).astype(BF16)


def _fox_attention(q, q_aug, kt_all, kt_aug, vt_all, b, t, past):
    tt = past + t
    tq = _pick_tile(t, 512, 16)
    tk = _pick_tile(tt, 512, LANES)
    nq, nk = t // tq, tt // tk
    pairs = [(i, j) for i in range(nq) for j in range(nk) if j * tk <= past + (i + 1) * tq - 1]
    qi = jnp.array([p[0] for p in pairs], I32)
    kj = jnp.array([p[1] for p in pairs], I32)
    last = jnp.array([1 if (idx + 1 == len(pairs) or pairs[idx + 1][0] != p[0]) else 0
                      for idx, p in enumerate(pairs)], I32)
    grid_spec = pltpu.PrefetchScalarGridSpec(
        num_scalar_prefetch=3, grid=(b, len(pairs)),
        in_specs=[
            pl.BlockSpec((tq, WA), lambda bi, n, qi, kj, la: (bi * nq + qi[n], 0)),
            pl.BlockSpec((tq, 2 * LANES), lambda bi, n, qi, kj, la: (bi * nq + qi[n], 0)),
            pl.BlockSpec((None, WA, tk), lambda bi, n, qi, kj, la: (bi, 0, kj[n])),
            pl.BlockSpec((None, LANES, tk), lambda bi, n, qi, kj, la: (bi, 0, kj[n])),
            pl.BlockSpec((None, WA, tk), lambda bi, n, qi, kj, la: (bi, 0, kj[n])),
        ],
        out_specs=pl.BlockSpec((tq, WA), lambda bi, n, qi, kj, la: (bi * nq + qi[n], 0)),
        scratch_shapes=[pltpu.VMEM((H_A, tq, 2 * LANES), BF16), pltpu.VMEM((H_A, tq, 1), F32),
                        pltpu.VMEM((H_A, tq, 1), F32), pltpu.VMEM((H_A // 2, tq, LANES), F32)])
    return pl.pallas_call(
        functools.partial(_attn_kernel, tq=tq, tk=tk, past=past), grid_spec=grid_spec,
        out_shape=jax.ShapeDtypeStruct((b * t, WA), BF16),
        compiler_params=_params("arbitrary", "arbitrary"), name="fox_attention")(qi, kj, last, q, q_aug, kt_all, kt_aug, vt_all)


def _gdn_kernel(u_ref, small_ref, smallt_ref, z_ref, s0_ref, conv0_ref, wconv_ref, gon_ref,
                yb_ref, sout_ref, cout_ref, s_sc, ubuf, *, chunk, cps, nsteps):
    c = pl.program_id(1)
    L, C = chunk, cps
    R = L * C
    lg = L.bit_length() - 1
    keep = CONV_W - 1

    @pl.when(c == 0)
    def _():
        s_sc[...] = s0_ref[...]
        ubuf[8 - keep:8, :] = conv0_ref[...]

    ubuf[8:8 + R, :] = u_ref[...]
    w = wconv_ref[...]
    conv = ubuf[8 - keep:8 - keep + R, :] * w[0:1, :]
    for i in range(1, CONV_W):
        conv = conv + ubuf[8 - keep + i:8 - keep + i + R, :] * w[i:i + 1, :]
    act = _silu(conv)
    tail = ubuf[8 + R - keep:8 + R, :]
    ubuf[8 - keep:8, :] = tail

    @pl.when(c == nsteps - 1)
    def _():
        cout_ref[...] = tail

    sm = small_ref[...]
    smt = smallt_ref[...]
    rr, rc = _iota((R, R), 0), _iota((R, R), 1)
    same = jnp.right_shift(rr, lg) == jnp.right_shift(rc, lg)
    g_col = _dot_exact_lhs(jnp.logical_and(same, rr >= rc).astype(BF16), sm, 3)
    qn_all = act[:, :QK_B]
    qn_all = qn_all * lax.rsqrt(_group_sumsq(qn_all, DK_B) + EPS) * (DK_B ** -0.5)
    kn_all = act[:, QK_B:2 * QK_B]
    kn_all = kn_all * lax.rsqrt(_group_sumsq(kn_all, DK_B) + EPS)

    W2 = 2 * L
    lo64 = _iota((1, LANES), 1) < DK_B
    hi64 = jnp.logical_not(lo64)
    lane2 = _iota((1, W2), 1)
    lo_l, hi_l = lane2 < L, lane2 >= L
    ri, ci = _iota((L, W2), 0), jnp.bitwise_and(_iota((L, W2), 1), L - 1)
    incl, strict = ri >= ci, ri > ci
    eye = (ri == ci).astype(F32)
    n_sq = max(int(math.ceil(math.log2(L))) - 1, 0)
    sub128 = _iota((LANES, 1), 0) < DK_B

    def stack_heads(x, lo, hi):
        x = x.astype(BF16)
        zero = jnp.zeros_like(x)
        return jnp.concatenate([jnp.where(lo, x, zero), jnp.where(hi, x, zero)], axis=0)

    items = [(ch, p) for ch in range(C) for p in range(H_B // 2)]
    g_rows = []
    for ch in range(C):
        tt, jj = _iota((R, W2), 0), jnp.bitwise_and(_iota((R, W2), 1), L - 1)
        sel = jnp.logical_and(jnp.right_shift(tt, lg) == ch, jnp.bitwise_and(tt, L - 1) <= jj)
        g_rows.append(_dot_exact_rhs(smt, sel.astype(BF16), 3))

    st = {}
    for ch, p in items:
        rs = slice(ch * L, (ch + 1) * L)
        ha, hb = 2 * H_B + 2 * p, 2 * H_B + 2 * p + 1
        gca, gcb = g_col[rs, ha:ha + 1], g_col[rs, hb:hb + 1]
        gla, glb = gca[L - 1:L, :], gcb[L - 1:L, :]
        q2, k2 = qn_all[rs, p * LANES:(p + 1) * LANES], kn_all[rs, p * LANES:(p + 1) * LANES]
        beta2 = jnp.where(lo64, sm[rs, H_B + 2 * p:H_B + 2 * p + 1], sm[rs, H_B + 2 * p + 1:H_B + 2 * p + 2])
        eg2 = jnp.exp(jnp.where(lo64, gca, gcb))
        egl2 = jnp.exp(jnp.where(lo64, gla - gca, glb - gcb))
        kb2 = k2 * beta2
        gr = jnp.where(lo_l, g_rows[ch][ha:ha + 1, :], g_rows[ch][hb:hb + 1, :])
        decay = jnp.exp(jnp.where(incl, jnp.where(lo_l, gca, gcb) - gr, NEG))
        ybd = stack_heads(k2, lo64, hi64)
        va = act[rs, 2 * QK_B + 2 * p * DV_B:2 * QK_B + (2 * p + 1) * DV_B] * sm[rs, H_B + 2 * p:H_B + 2 * p + 1]
        vb = act[rs, 2 * QK_B + (2 * p + 1) * DV_B:2 * QK_B + (2 * p + 2) * DV_B] * sm[rs, H_B + 2 * p + 1:H_B + 2 * p + 2]
        st[ch, p] = dict(
            decay=decay, ybd=ybd, kb2=kb2, q2=q2,
            kbg_bd=stack_heads(kb2 * eg2, lo64, hi64), qd_bd=stack_heads(q2 * eg2, lo64, hi64),
            kd_bd=stack_heads(k2 * egl2, lo64, hi64), vst=jnp.concatenate([va, vb], axis=0),
            gl2=jnp.where(sub128, jnp.exp(gla), jnp.exp(glb)))
    for it in items:
        d = st[it]
        d["m"] = jnp.where(strict, _dot_t(d["kb2"], d["ybd"]) * d["decay"], 0.0)
        d["attn"] = jnp.where(incl, _dot_t(d["q2"], d["ybd"]) * d["decay"], 0.0)
    for it in items:
        d = st[it]
        d["t"] = eye - d["m"]
        d["mp"] = d["m"]
    for lvl in range(n_sq):
        for it in items:
            d = st[it]
            d["mp"] = _dot(d["mp"], stack_heads(d["mp"], lo_l, hi_l))
        for it in items:
            d = st[it]
            d["t"] = d["t"] + _dot(d["t"], stack_heads(d["mp"], lo_l, hi_l))
    for it in items:
        d = st[it]
        d["ust"] = _dot(stack_heads(d["t"], lo_l, hi_l), d["vst"])
        d["w2"] = _dot(d["t"], d["kbg_bd"])
    for ch in range(C):
        rs = slice(ch * L, (ch + 1) * L)
        s2s, ass = [], []
        for p in range(H_B // 2):
            d = st[ch, p]
            s2 = s_sc[p * LANES:(p + 1) * LANES, :]
            s2s.append(s2)
            ass.append(_dot(jnp.concatenate([stack_heads(d["w2"], lo64, hi64), d["qd_bd"]], axis=0), s2))
        for p in range(H_B // 2):
            d = st[ch, p]
            v_new = d["ust"] - ass[p][:W2, :]
            o_st = ass[p][W2:, :] + _dot(stack_heads(d["attn"], lo_l, hi_l), v_new)
            s_sc[p * LANES:(p + 1) * LANES, :] = s2s[p] * d["gl2"] + _dot_tl(d["kd_bd"], v_new)
            for half in range(2):
                hd = 2 * p + half
                o = o_st[half * L:(half + 1) * L, :]
                on = o * lax.rsqrt(jnp.mean(o * o, axis=-1, keepdims=True) + EPS) * gon_ref[...]
                zs = z_ref[rs, hd * DV_B:(hd + 1) * DV_B].astype(F32)
                yb_ref[rs, hd * DV_B:(hd + 1) * DV_B] = (on * zs).astype(BF16)

    @pl.when(c == nsteps - 1)
    def _():
        sout_ref[...] = s_sc[...]


def _gdn(u, small, smallt, z, s0, conv0, w_conv, g_onorm, b, t):
    clen = GDN_CHUNK if t % GDN_CHUNK == 0 else t
    assert clen & (clen - 1) == 0, clen
    cps = GDN_CHUNKS_PER_STEP if (t // clen) % GDN_CHUNKS_PER_STEP == 0 else 1
    chunk = clen * cps
    nc = t // chunk
    smallt3 = smallt.reshape(smallt.shape[0], b * nc, chunk).transpose(1, 0, 2)
    keep = CONV_W - 1
    return pl.pallas_call(
        functools.partial(_gdn_kernel, chunk=clen, cps=cps, nsteps=nc), grid=(b, nc),
        in_specs=[
            pl.BlockSpec((chunk, CONV_DIM), lambda i, j: (i * nc + j, 0)),
            pl.BlockSpec((chunk, LANES), lambda i, j: (i * nc + j, 0)),
            pl.BlockSpec((None, smallt.shape[0], chunk), lambda i, j: (i * nc + j, 0, 0)),
            pl.BlockSpec((chunk, V_B), lambda i, j: (i * nc + j, 0)),
            pl.BlockSpec((None, H_B * DK_B, DV_B), lambda i, j: (i, 0, 0)),
            pl.BlockSpec((None, keep, CONV_DIM), lambda i, j: (i, 0, 0)),
            pl.BlockSpec((CONV_W, CONV_DIM), lambda i, j: (0, 0)),
            pl.BlockSpec((1, DV_B), lambda i, j: (0, 0)),
        ],
        out_specs=[
            pl.BlockSpec((chunk, V_B), lambda i, j: (i * nc + j, 0)),
            pl.BlockSpec((None, H_B * DK_B, DV_B), lambda i, j: (i, 0, 0)),
            pl.BlockSpec((None, keep, CONV_DIM), lambda i, j: (i, 0, 0)),
        ],
        out_shape=[jax.ShapeDtypeStruct((b * t, V_B), BF16),
                   jax.ShapeDtypeStruct((b, H_B * DK_B, DV_B), F32),
                   jax.ShapeDtypeStruct((b, keep, CONV_DIM), F32)],
        scratch_shapes=[pltpu.VMEM((H_B * DK_B, DV_B), F32), pltpu.VMEM((8 + chunk, CONV_DIM), F32)],
        compiler_params=_params("arbitrary", "arbitrary"), name="gdn_chunked",
    )(u, small, smallt3, z, s0, conv0, w_conv, g_onorm)


def _merge_kernel(x_ref, ya_ref, yb_ref, gate_ref, wa_ref, wb_ref, wo_ref, gf_ref, wr_ref, br_ref,
                  x1_ref, h2_ref, logit_ref):
    br_a = jnp.dot(ya_ref[...], wa_ref[...], preferred_element_type=F32)
    br_b = jnp.dot(yb_ref[...], wb_ref[...], preferred_element_type=F32)
    merged = gate_ref[:, :D_MODEL] * br_a + gate_ref[:, D_MODEL:] * br_b
    x1 = x_ref[...] + _dot(merged, wo_ref[...])
    x1_ref[...] = x1
    h2 = _rms_rows(x1, gf_ref[...])
    h2_ref[...] = h2
    logit_ref[...] = _dot(h2, wr_ref[...]) + br_ref[...]


def _route_kernel(logit_ref, meta_ref, wgt_ref, cnt_ref, run, start, *, block):
    sweep, step = pl.program_id(0), pl.program_id(1)

    @pl.when(jnp.logical_and(sweep == 0, step == 0))
    def _():
        run[...] = jnp.zeros_like(run)
        start[...] = jnp.zeros_like(start)

    @pl.when(jnp.logical_and(sweep == 1, step == 0))
    def _():
        counts = run[...]
        cnt_ref[...] = counts
        padded = jnp.floor((counts + (block - 1)) * (1.0 / block)) * block
        before_e = (_iota((LANES, LANES), 0) < _iota((LANES, LANES), 1)).astype(BF16)
        start[...] = _dot_exact_rhs(jnp.broadcast_to(padded, (8, LANES)), before_e, 3)[0:1, :]
        run[...] = jnp.zeros_like(run)

    x = logit_ref[...]
    tm = x.shape[0]
    lane = _iota((1, LANES), 1)
    lanef = lane.astype(F32)
    big = float(LANES)

    def first_argmax(vals, valid):
        mx = jnp.max(vals, axis=-1, keepdims=True)
        idx = jnp.min(jnp.where(jnp.logical_and(vals == mx, valid), lanef, big), axis=-1, keepdims=True)
        return mx, idx.astype(I32)

    gmask = lane < N_GROUPS
    gmax, gsel = first_argmax(jnp.where(gmask, x, NEG), gmask)
    p_g = 1.0 / jnp.sum(jnp.where(gmask, jnp.exp(x - gmax), 0.0), axis=-1, keepdims=True)
    first = N_GROUPS + EXPERTS_PER_GROUP * gsel
    emask = jnp.logical_and(lane >= first, lane < first + EXPERTS_PER_GROUP)
    ev = jnp.where(emask, x, NEG)
    m1, i1 = first_argmax(ev, emask)
    emask2 = jnp.logical_and(emask, lane != i1)
    m2, i2 = first_argmax(jnp.where(emask2, x, NEG), emask2)
    e21 = jnp.exp(m2 - m1)
    w1 = p_g / (1.0 + e21)
    w2 = p_g * e21 / (1.0 + e21)
    e1, e2 = i1 - N_GROUPS, i2 - N_GROUPS
    oh1, oh2 = lane == e1, lane == e2
    onehot = jnp.logical_or(oh1, oh2).astype(BF16)
    strict = (_iota((tm, tm), 0) > _iota((tm, tm), 1)).astype(BF16)
    slot = jnp.dot(strict, onehot, preferred_element_type=F32) + run[...] + start[...]
    run[...] = run[...] + jnp.sum(onehot.astype(F32), axis=0, keepdims=True)

    @pl.when(sweep == 1)
    def _():
        d1 = jnp.sum(jnp.where(oh1, slot, 0.0), axis=-1, keepdims=True).astype(I32)
        d2 = jnp.sum(jnp.where(oh2, slot, 0.0), axis=-1, keepdims=True).astype(I32)
        l8 = _iota((1, 8), 1)
        meta_ref[...] = jnp.where(l8 == 0, e1, jnp.where(l8 == 1, e2, jnp.where(l8 == 2, d1, jnp.where(l8 == 3, d2, 0))))
        wgt_ref[...] = jnp.where(l8 == 0, w1, jnp.where(l8 == 1, w2, 0.0))


def _row_copy(src, dst, sem, s, d):
    return pltpu.make_async_copy(src.at[pl.ds(s, 1)], dst.at[pl.ds(d, 1)], sem)


DMA_UNROLL = 8


DISPATCH_SLOTS = 3


def _dispatch_kernel(seg_ref, dest_ref, h_ref, xs_ref, buf, sem_in, sem_out, sem_pad, *, td, nsteps, pad_bits):
    step = pl.program_id(0)
    slot = lax.rem(step, DISPATCH_SLOTS)

    def tile_copy(tile, s):
        return pltpu.make_async_copy(h_ref.at[pl.ds(tile * td, td)], buf.at[s], sem_in.at[s])

    def drain(s):
        def body(t, carry):
            _row_copy(buf.at[s], xs_ref, sem_out.at[s], 0, 0).wait()
            _row_copy(buf.at[s], xs_ref, sem_out.at[s], 0, 0).wait()
            return carry
        lax.fori_loop(0, td, body, 0, unroll=DMA_UNROLL)

    @pl.when(step == 0)
    def _():
        tile_copy(0, 0).start()

    @pl.when(step >= 2)
    def _():
        drain(lax.rem(step + 1, DISPATCH_SLOTS))

    @pl.when(step + 1 < nsteps)
    def _():
        tile_copy(step + 1, lax.rem(step + 1, DISPATCH_SLOTS)).start()

    tile_copy(step, slot).wait()

    def issue(t, carry):
        _row_copy(buf.at[slot], xs_ref, sem_out.at[slot], t, dest_ref[0, t]).start(priority=0)
        _row_copy(buf.at[slot], xs_ref, sem_out.at[slot], t, dest_ref[0, td + t]).start(priority=1)
        return carry

    lax.fori_loop(0, td, issue, 0, unroll=DMA_UNROLL)

    @pl.when(step == nsteps - 1)
    def _():
        if nsteps >= 2:
            drain(lax.rem(step + 2, DISPATCH_SLOTS))
        drain(slot)
        buf[0] = jnp.zeros((td, D_MODEL), F32)
        runs = [1 << k for k in reversed(range(3, pad_bits))]

        def run_copy(rows, at):
            return pltpu.make_async_copy(buf.at[0, pl.ds(0, rows)], xs_ref.at[pl.ds(pl.multiple_of(at, 8), rows)], sem_pad)

        def fill(wait):
            def body(e, carry):
                first, end = seg_ref[e], seg_ref[N_EXPERTS + e]
                at = jnp.bitwise_and(first + 7, -8)
                for r in range(7):
                    @pl.when(first + r < at)
                    def _(r=r):
                        c = _row_copy(buf.at[0], xs_ref, sem_pad, 0, 0 if wait else first + r)
                        c.wait() if wait else c.start()
                gap = end - at
                for rows in runs:
                    bit = jnp.bitwise_and(gap, rows) != 0

                    @pl.when(bit)
                    def _(rows=rows, at=at):
                        run_copy(rows, 0).wait() if wait else run_copy(rows, at).start()

                    at = at + jnp.where(bit, rows, 0)
                return carry
            lax.fori_loop(0, N_EXPERTS, body, 0)

        fill(False)
        fill(True)
        tail = lambda c: pltpu.make_async_copy(buf.at[0], xs_ref.at[pl.ds(c * td, td)], sem_pad)
        first, stop = lax.div(seg_ref[2 * N_EXPERTS - 1], td), xs_ref.shape[0] // td
        lax.fori_loop(first, stop, lambda c, x: (tail(c).start(), x)[1], 0)
        lax.fori_loop(first, stop, lambda c, x: (tail(0).wait(), x)[1], 0)


def _expert_kernel(be_ref, nu_ref, x_ref, w1_ref, w3_ref, w2_ref, y_ref, w1_sc, w3_sc, w2_sc):
    blk = pl.program_id(0)
    used = blk < nu_ref[0]
    new_expert = jnp.logical_or(blk == 0, be_ref[blk] != be_ref[jnp.maximum(blk - 1, 0)])

    @pl.when(jnp.logical_and(used, new_expert))
    def _():
        w1_sc[...] = w1_ref[...].astype(BF16)
        w3_sc[...] = w3_ref[...].astype(BF16)
        w2_sc[...] = w2_ref[...].astype(BF16)

    @pl.when(used)
    def _():
        x = x_ref[...].astype(BF16)
        a = jnp.dot(x, w1_sc[...], preferred_element_type=F32)
        g = jnp.dot(x, w3_sc[...], preferred_element_type=F32)
        y_ref[...] = _dot(_silu(a) * g, w2_sc[...])

    @pl.when(jnp.logical_not(used))
    def _():
        y_ref[...] = jnp.zeros_like(y_ref)


def _combine_kernel(dest_ref, next_ref, x1_ref, wgt_ref, yb_ref, o_ref, g0, g1, sems, *, tc):
    step, nsteps = pl.program_id(0), pl.num_programs(0)
    slot = step % 2

    def issue(idx_ref, s):
        def body(t, carry):
            _row_copy(yb_ref, g0.at[s], sems.at[s], idx_ref[0, t], t).start(priority=0)
            _row_copy(yb_ref, g1.at[s], sems.at[s], idx_ref[0, tc + t], t).start(priority=1)
            return carry
        lax.fori_loop(0, tc, body, 0, unroll=DMA_UNROLL)

    @pl.when(step == 0)
    def _():
        issue(dest_ref, 0)

    @pl.when(step + 1 < nsteps)
    def _():
        issue(next_ref, 1 - slot)

    def drain(t, carry):
        _row_copy(yb_ref, g0.at[slot], sems.at[slot], 0, 0).wait()
        _row_copy(yb_ref, g1.at[slot], sems.at[slot], 0, 0).wait()
        return carry

    lax.fori_loop(0, tc, drain, 0, unroll=DMA_UNROLL)
    wg = wgt_ref[...]
    o_ref[...] = x1_ref[...] + (g0[slot] * wg[:, 0:1] + g1[slot] * wg[:, 1:2])


def _hier_moe(x1, h2, logits, w_e1, w_e3, w_e2):
    n = x1.shape[0]
    tr = _pick_tile(n, 512)
    bm = MOE_BLOCK if 2 * n >= 4 * MOE_BLOCK * N_EXPERTS else MOE_BLOCK // 4
    assert bm & (bm - 1) == 0
    meta, wgt, cnt = pl.pallas_call(
        functools.partial(_route_kernel, block=bm), grid=(2, n // tr),
        in_specs=[pl.BlockSpec((tr, LANES), lambda s, i: (i, 0))],
        out_specs=[pl.BlockSpec((tr, 8), lambda s, i: (i * s, 0)), pl.BlockSpec((tr, 8), lambda s, i: (i * s, 0)),
                   pl.BlockSpec((1, LANES), lambda s, i: (0, 0))],
        out_shape=[jax.ShapeDtypeStruct((n, 8), I32), jax.ShapeDtypeStruct((n, 8), F32),
                   jax.ShapeDtypeStruct((1, LANES), F32)],
        scratch_shapes=[pltpu.VMEM((1, LANES), F32), pltpu.VMEM((1, LANES), F32)],
        compiler_params=_params("arbitrary", "arbitrary"), name="moe_route")(logits)

    counts = cnt[0, :N_EXPERTS].astype(I32)
    padded = ((counts + bm - 1) // bm) * bm
    pad_end = jnp.cumsum(padded)
    nblk = -(-(2 * n) // bm) + N_EXPERTS
    n_used = (pad_end[-1] // bm).astype(I32).reshape(1)
    blk_first = jnp.arange(nblk, dtype=I32) * bm
    blk_expert = jnp.minimum(jnp.sum((pad_end[None, :] <= blk_first[:, None]).astype(I32), axis=1), N_EXPERTS - 1)
    seg = jnp.concatenate([pad_end - padded + counts, pad_end])

    td = _pick_tile(n, min(256, bm))
    assert bm % td == 0
    dest = meta[:, 2:4].reshape(n // td, td, 2).transpose(0, 2, 1).reshape(n // td, 1, 2 * td)
    xs = pl.pallas_call(
        functools.partial(_dispatch_kernel, td=td, nsteps=n // td, pad_bits=bm.bit_length() - 1),
        grid_spec=pltpu.PrefetchScalarGridSpec(
            num_scalar_prefetch=1, grid=(n // td,),
            in_specs=[pl.BlockSpec((None, 1, 2 * td), lambda i, sg: (i, 0, 0), memory_space=pltpu.SMEM),
                      pl.BlockSpec(memory_space=pl.ANY)],
            out_specs=pl.BlockSpec(memory_space=pl.ANY),
            scratch_shapes=[pltpu.VMEM((DISPATCH_SLOTS, td, D_MODEL), F32),
                            pltpu.SemaphoreType.DMA((DISPATCH_SLOTS,)), pltpu.SemaphoreType.DMA((DISPATCH_SLOTS,)),
                            pltpu.SemaphoreType.DMA(())]),
        out_shape=jax.ShapeDtypeStruct((nblk * bm, D_MODEL), F32),
        compiler_params=_params("arbitrary"), name="moe_dispatch")(seg, dest, h2)

    xmap = lambda i, be, nu: (jnp.minimum(i, nu[0] - 1), 0)
    grid_spec = pltpu.PrefetchScalarGridSpec(
        num_scalar_prefetch=2, grid=(nblk,),
        in_specs=[pl.BlockSpec((bm, D_MODEL), xmap),
                  pl.BlockSpec((None, D_MODEL, D_EXPERT), lambda i, be, nu: (be[i], 0, 0)),
                  pl.BlockSpec((None, D_MODEL, D_EXPERT), lambda i, be, nu: (be[i], 0, 0)),
                  pl.BlockSpec((None, D_EXPERT, D_MODEL), lambda i, be, nu: (be[i], 0, 0))],
        out_specs=pl.BlockSpec((bm, D_MODEL), lambda i, be, nu: (i, 0)),
        scratch_shapes=[pltpu.VMEM((D_MODEL, D_EXPERT), BF16), pltpu.VMEM((D_MODEL, D_EXPERT), BF16),
                        pltpu.VMEM((D_EXPERT, D_MODEL), BF16)])
    yb = pl.pallas_call(
        _expert_kernel, grid_spec=grid_spec, out_shape=jax.ShapeDtypeStruct((nblk * bm, D_MODEL), F32),
        compiler_params=_params("arbitrary"), name="moe_experts")(blk_expert, n_used, xs, w_e1, w_e3, w_e2)

    tc = td
    return pl.pallas_call(
        functools.partial(_combine_kernel, tc=tc), grid=(n // tc,),
        in_specs=[pl.BlockSpec((None, 1, 2 * tc), lambda i: (i, 0, 0), memory_space=pltpu.SMEM),
                  pl.BlockSpec((None, 1, 2 * tc), lambda i: (jnp.minimum(i + 1, n // tc - 1), 0, 0),
                               memory_space=pltpu.SMEM),
                  pl.BlockSpec((tc, D_MODEL), lambda i: (i, 0)),
                  pl.BlockSpec((tc, 8), lambda i: (i, 0)),
                  pl.BlockSpec(memory_space=pl.ANY)],
        out_specs=pl.BlockSpec((tc, D_MODEL), lambda i: (i, 0)),
        out_shape=jax.ShapeDtypeStruct((n, D_MODEL), F32),
        scratch_shapes=[pltpu.VMEM((2, tc, D_MODEL), F32), pltpu.VMEM((2, tc, D_MODEL), F32),
                        pltpu.SemaphoreType.DMA((2,))],
        compiler_params=_params("arbitrary"), name="moe_combine")(dest, dest, x1, wgt, yb)


def _prep_weights(l, g_mix, w_in, b_f, g_qnorm, g_knorm, w_conv, a_log, dt_bias, g_onorm, b_gate,
                  w_br_a, w_br_b, w_out, g_ffn, w_grp, b_grp, w_exp, b_exp, w_e1, w_e3, w_e2):
    w = w_in[l]
    offs = [0]
    for s in (WA, WA, WA, H_A, QK_B, QK_B, V_B, H_B, H_B, V_B, D_MODEL, D_MODEL):
        offs.append(offs[-1] + s)
    col = lambda i, j=None: w[:, offs[i]:offs[(i if j is None else j) + 1]]
    n_small = H_A + 2 * H_B
    w_small = jnp.concatenate([col(3), col(7), col(8)], axis=1)
    zeros8 = jnp.zeros((H_A,), F32)
    bias = jnp.concatenate([b_f[l], zeros8, dt_bias[l]])
    alog = jnp.concatenate([zeros8, zeros8, a_log[l]])
    pad = lambda v, n: jnp.pad(v, (0, n - v.shape[0]))
    rows_t = 32
    return dict(
        g_mix=g_mix[l][None, :],
        wq=col(0).astype(BF16), wkt=col(1).T.astype(BF16), wvt=col(2).T.astype(BF16),
        gq=jnp.tile(g_qnorm[l], H_A)[None, :], gkt=jnp.tile(g_knorm[l], H_A)[:, None],
        ws=jnp.pad(w_small, ((0, 0), (0, LANES - n_small))).astype(BF16),
        wst=jnp.pad(w_small.T, ((0, rows_t - n_small), (0, 0))).astype(BF16),
        wu=col(4, 6).astype(BF16),
        bias=pad(bias, LANES)[None, :], alog=pad(alog, LANES)[None, :],
        biast=pad(bias, rows_t)[:, None], alogt=pad(alog, rows_t)[:, None],
        wz=col(9).astype(BF16), wg=col(10, 11).astype(BF16), bg=b_gate[l][None, :],
        w_conv=w_conv[l], g_onorm=g_onorm[l][None, :],
        w_br_a=w_br_a[l].astype(BF16), w_br_b=w_br_b[l].astype(BF16), w_out=w_out[l].astype(BF16),
        g_ffn=g_ffn[l][None, :],
        wr=jnp.pad(jnp.concatenate([w_grp[l], w_exp[l]], axis=1), ((0, 0), (0, LANES - N_GROUPS - N_EXPERTS))).astype(BF16),
        br=pad(jnp.concatenate([b_grp[l], b_exp[l]]), LANES)[None, :],
        w_e1=w_e1[l], w_e3=w_e3[l], w_e2=w_e2[l],
    )


def _layer(x, past_k, past_v, past_logf, s0, conv0, p):
    b, t, _ = x.shape
    past = past_k.shape[1]
    n = b * t
    xf = x.reshape(n, D_MODEL)
    tm = tm_in = _pick_tile(n, 1024)

    q16, kt32, kt16, vt32, vt16 = _inproj_fox(xf, p, b, t)
    small, smallt, u = _row_call(
        _inproj_gdn_kernel, n, tm_in, [xf],
        [p["g_mix"], p["ws"], p["wst"], p["wu"], p["bias"], p["alog"], p["biast"], p["alogt"]],
        [((n, LANES), F32, False), ((32, n), F32, True), ((n, CONV_DIM), F32, False)], "inproj_gdn")
    z, gates = _row_call(
        _inproj_gate_kernel, n, tm_in, [xf], [p["g_mix"], p["wz"], p["wg"], p["bg"]],
        [((n, V_B), BF16, False), ((n, 2 * D_MODEL), BF16, False)], "inproj_gate")

    logf = small[:, :H_A].reshape(b, t, H_A)
    past_lanes = jnp.pad(past_logf.astype(F32), ((0, 0), (0, 0), (0, LANES - H_A)))
    k_aug, q_aug = _forget_companions(jnp.concatenate([past_lanes, small.reshape(b, t, LANES)], axis=1))
    feature_major = lambda c: c.transpose(0, 2, 3, 1).reshape(b, WA, past).astype(BF16)
    kt_all = jnp.concatenate([feature_major(past_k), kt16], axis=2)
    vt_all = jnp.concatenate([feature_major(past_v), vt16], axis=2)
    y_a = _fox_attention(q16, q_aug[:, past:].reshape(n, 2 * LANES), kt_all, k_aug, vt_all, b, t, past)

    y_b, s_new, conv_new = _gdn(u, small, smallt, z, s0.reshape(b, H_B * DK_B, DV_B), conv0,
                                p["w_conv"], p["g_onorm"], b, t)

    x1, h2, logits = _row_call(
        _merge_kernel, n, tm, [xf, y_a, y_b, gates],
        [p["w_br_a"], p["w_br_b"], p["w_out"], p["g_ffn"], p["wr"], p["br"]],
        [((n, D_MODEL), F32, False), ((n, D_MODEL), F32, False), ((n, LANES), F32, False)], "merge_outproj")
    x2 = _hier_moe(x1, h2, logits, p["w_e1"], p["w_e3"], p["w_e2"])
    token_major = lambda c: c.reshape(b, H_A, DH_A, t).transpose(0, 3, 1, 2)
    return (x2.reshape(b, t, D_MODEL), token_major(kt32), token_major(vt32), logf,
            s_new.reshape(b, H_B, DK_B, DV_B), conv_new)


def kernel(x_prompt, x_sample, cache_fox_k, cache_fox_v, cache_fox_logf, state_gdn, state_gdn_conv, g_mix, w_in, b_f, g_qnorm, g_knorm, w_conv, a_log, dt_bias, g_onorm, b_gate, w_br_a, w_br_b, w_out, g_ffn, w_grp, b_grp, w_exp, b_exp, w_e1, w_e3, w_e2):
    depth = w_in.shape[0]
    bp = x_prompt.shape[0]
    yp, ys = x_prompt, x_sample
    outs_p, outs_s = [], []
    for l in range(depth):
        p = _prep_weights(l, g_mix, w_in, b_f, g_qnorm, g_knorm, w_conv, a_log, dt_bias, g_onorm, b_gate,
                          w_br_a, w_br_b, w_out, g_ffn, w_grp, b_grp, w_exp, b_exp, w_e1, w_e3, w_e2)
        ys, *rest_s = _layer(ys, cache_fox_k[l], cache_fox_v[l], cache_fox_logf[l], state_gdn[l], state_gdn_conv[l], p)
        yp, *rest_p = _layer(
            yp, jnp.zeros((bp, 0, H_A, DH_A), F32), jnp.zeros((bp, 0, H_A, DH_A), F32), jnp.zeros((bp, 0, H_A), F32),
            jnp.zeros((bp, H_B, DK_B, DV_B), F32), jnp.zeros((bp, CONV_W - 1, CONV_DIM), F32), p)
        outs_p.append(rest_p)
        outs_s.append(rest_s)
    stack = lambda outs, i: jnp.stack([o[i] for o in outs])
    return (yp, ys,
            stack(outs_p, 0), stack(outs_p, 1), stack(outs_p, 2), stack(outs_p, 3), stack(outs_p, 4),
            stack(outs_s, 0), stack(outs_s, 1), stack(outs_s, 2), stack(outs_s, 3), stack(outs_s, 4))
```

```python
import functools
import math

import jax
import jax.numpy as jnp
import numpy as np
from jax import lax
from jax.experimental import pallas as pl
from jax.experimental.pallas import tpu as pltpu

F32, BF16, I32, U32 = jnp.float32, jnp.bfloat16, jnp.int32, jnp.uint32

D_MODEL = 1024
H_A, DH_A = 8, 64
WA = H_A * DH_A
H_B, DK_B, DV_B = 8, 64, 128
QK_B, V_B = H_B * DK_B, H_B * DV_B
CONV_W = 4
CONV_DIM = 2 * QK_B + V_B
N_GROUPS, EXPERTS_PER_GROUP = 4, 8
N_EXPERTS = N_GROUPS * EXPERTS_PER_GROUP
D_EXPERT = 512
GDN_CHUNK = 64
GDN_CHUNKS_PER_STEP = 4
EPS = 1e-6
LOG2E = math.log2(math.e)
NEG = -1e30
LANES = 128
V7X_VMEM_LIMIT = 56 * 1024 * 1024
MOE_BLOCK = 512


def _params(*sem):
    return pltpu.CompilerParams(dimension_semantics=sem, vmem_limit_bytes=V7X_VMEM_LIMIT)


def _pick_tile(n, cap, mult=8):
    if n <= cap:
        return n
    for t in range(cap - cap % mult, 0, -mult):
        if n % t == 0:
            return t
    return n


def _dot(a, b):
    return jnp.dot(a.astype(BF16), b.astype(BF16), preferred_element_type=F32)


def _dot_t(a, b):
    return lax.dot_general(a.astype(BF16), b.astype(BF16), (((1,), (1,)), ((), ())), preferred_element_type=F32)


def _dot_tl(a, b):
    return lax.dot_general(a.astype(BF16), b.astype(BF16), (((0,), (0,)), ((), ())), preferred_element_type=F32)


def _split(x, n):
    parts = []
    for _ in range(n):
        p = x.astype(BF16)
        parts.append(p)
        x = x - p.astype(F32)
    return parts


def _dot_exact_lhs(ones, x, n):
    acc = None
    for p in _split(x, n):
        t = jnp.dot(ones, p, preferred_element_type=F32)
        acc = t if acc is None else acc + t
    return acc


def _dot_exact_rhs(x, ones, n):
    acc = None
    for p in _split(x, n):
        t = jnp.dot(p, ones, preferred_element_type=F32)
        acc = t if acc is None else acc + t
    return acc


def _iota(shape, dim):
    return lax.broadcasted_iota(I32, shape, dim)


def _softplus(x):
    return jnp.maximum(x, 0.0) + jnp.log1p(jnp.exp(-jnp.abs(x)))


def _silu(x):
    return x * jax.nn.sigmoid(x)


def _rms_rows(x, g):
    return x * lax.rsqrt(jnp.mean(x * x, axis=-1, keepdims=True) + EPS) * g


def _head_ones(width, head):
    shift = head.bit_length() - 1
    assert head == 1 << shift
    r = jnp.right_shift(_iota((width, width), 0), shift)
    c = jnp.right_shift(_iota((width, width), 1), shift)
    return (r == c).astype(BF16)


def _group_sumsq(y, head):
    ones = _head_ones(2 * LANES, head)
    sq = y * y
    parts = [_dot_exact_rhs(sq[:, c:c + 2 * LANES], ones, 2) for c in range(0, y.shape[1], 2 * LANES)]
    return parts[0] if len(parts) == 1 else jnp.concatenate(parts, axis=1)


def _inproj_fox_kernel(x_ref, g_ref, wq_ref, wkt_ref, wvt_ref, gq_ref, gkt_ref,
                       q_ref, kt32_ref, kt16_ref, vt32_ref, vt16_ref):
    h = _rms_rows(x_ref[...], g_ref[...]).astype(BF16)
    q = jnp.dot(h, wq_ref[...], preferred_element_type=F32)
    q = q * lax.rsqrt(_group_sumsq(q, DH_A) * (1.0 / DH_A) + EPS) * gq_ref[...]
    q_ref[...] = (q * (DH_A ** -0.5 * LOG2E)).astype(BF16)
    kt = lax.dot_general(wkt_ref[...], h, (((1,), (1,)), ((), ())), preferred_element_type=F32)
    ones = _head_ones(2 * LANES, DH_A)
    sq = kt * kt
    ms = jnp.concatenate([_dot_exact_lhs(ones, sq[r:r + 2 * LANES, :], 2) for r in range(0, WA, 2 * LANES)], axis=0)
    kt = kt * lax.rsqrt(ms * (1.0 / DH_A) + EPS) * gkt_ref[...]
    kt32_ref[...] = kt
    kt16_ref[...] = kt.astype(BF16)
    vt = lax.dot_general(wvt_ref[...], h, (((1,), (1,)), ((), ())), preferred_element_type=F32)
    vt32_ref[...] = vt
    vt16_ref[...] = vt.astype(BF16)


def _inproj_fox(xf, p, b, t):
    n = b * t
    tm = _pick_tile(t, 512, LANES)
    per = t // tm
    row = lambda i: (i, 0)
    col = lambda i: (i // per, 0, i % per)
    full = [p["g_mix"], p["wq"], p["wkt"], p["wvt"], p["gq"], p["gkt"]]
    return pl.pallas_call(
        _inproj_fox_kernel, grid=(n // tm,),
        in_specs=[pl.BlockSpec((tm, D_MODEL), row)] + [pl.BlockSpec(a.shape, lambda i: (0, 0)) for a in full],
        out_specs=[pl.BlockSpec((tm, WA), row)] + [pl.BlockSpec((None, WA, tm), col)] * 4,
        out_shape=[jax.ShapeDtypeStruct((n, WA), BF16), jax.ShapeDtypeStruct((b, WA, t), F32),
                   jax.ShapeDtypeStruct((b, WA, t), BF16), jax.ShapeDtypeStruct((b, WA, t), F32),
                   jax.ShapeDtypeStruct((b, WA, t), BF16)],
        compiler_params=_params("arbitrary"), name="inproj_fox")(xf, *full)


def _small_epilogue(y, bias, a_log, idx):
    yb = y + bias
    logf = -_softplus(-yb)
    beta = jax.nn.sigmoid(y)
    g = -jnp.exp(a_log) * _softplus(yb)
    return jnp.where(idx < H_A, logf, jnp.where(idx < H_A + H_B, beta, g))


def _inproj_gdn_kernel(x_ref, g_ref, ws_ref, wst_ref, wu_ref, bias_ref, alog_ref, biast_ref, alogt_ref,
                       small_ref, smallt_ref, u_ref):
    h = _rms_rows(x_ref[...], g_ref[...]).astype(BF16)
    ys = jnp.dot(h, ws_ref[...], preferred_element_type=F32)
    small_ref[...] = _small_epilogue(ys, bias_ref[...], alog_ref[...], _iota(ys.shape, 1))
    yt = lax.dot_general(wst_ref[...], h, (((1,), (1,)), ((), ())), preferred_element_type=F32)
    smallt_ref[...] = _small_epilogue(yt, biast_ref[...], alogt_ref[...], _iota(yt.shape, 0))
    for c in range(0, CONV_DIM, 512):
        u_ref[:, c:c + 512] = jnp.dot(h, wu_ref[:, c:c + 512], preferred_element_type=F32)


def _inproj_gate_kernel(x_ref, g_ref, wz_ref, wg_ref, bg_ref, zs_ref, gate_ref):
    h = _rms_rows(x_ref[...], g_ref[...]).astype(BF16)
    for c in range(0, V_B, 512):
        zs_ref[:, c:c + 512] = _silu(jnp.dot(h, wz_ref[:, c:c + 512], preferred_element_type=F32)).astype(BF16)
    for c in range(0, 2 * D_MODEL, 512):
        y = jnp.dot(h, wg_ref[:, c:c + 512], preferred_element_type=F32)
        gate_ref[:, c:c + 512] = jax.nn.sigmoid(y + bg_ref[:, c:c + 512]).astype(BF16)


def _row_call(kernel, n, tm, row_inputs, full_inputs, outs, name):
    in_specs = [pl.BlockSpec((tm, a.shape[1]), lambda i: (i, 0)) for a in row_inputs]
    in_specs += [pl.BlockSpec(a.shape, lambda i, nd=a.ndim: (0,) * nd) for a in full_inputs]
    out_specs, out_shapes = [], []
    for shape, dtype, transposed in outs:
        if transposed:
            out_specs.append(pl.BlockSpec((shape[0], tm), lambda i: (0, i)))
        else:
            out_specs.append(pl.BlockSpec((shape[0] // n * tm, shape[1]), lambda i: (i, 0)))
        out_shapes.append(jax.ShapeDtypeStruct(shape, dtype))
    return pl.pallas_call(kernel, grid=(n // tm,), in_specs=in_specs, out_specs=out_specs, out_shape=out_shapes,
                          compiler_params=_params("arbitrary"), name=name)(*row_inputs, *full_inputs)


AUG_PIECES = 3
AUG_GROUP = 16


def _companion_placement():
    n = AUG_PIECES
    place_k = np.zeros((n, LANES, LANES), np.float32)
    place_q = np.zeros((n, LANES, 2 * LANES), np.float32)
    ones_k = np.zeros((LANES, 1), np.float32)
    ones_q = np.zeros((1, 2 * LANES), np.float32)
    for hd in range(H_A):
        pair, odd = divmod(hd, 2)
        g = AUG_GROUP * pair
        for k in range(n):
            place_k[k, g + n * odd + k, hd] = -1.0
            place_q[k, hd, odd * LANES + g + 2 * n + k] = 1.0
            ones_k[g + 2 * n + k, 0] = 1.0
            ones_q[0, odd * LANES + g + n * odd + k] = 1.0
    return jnp.asarray(place_k, BF16), jnp.asarray(place_q, BF16), jnp.asarray(ones_k, F32), jnp.asarray(ones_q, F32)


def _cumsum_kernel(x_ref, pk_ref, pq_ref, ok_ref, oq_ref, kaug_ref, qaug_ref, carry):
    @pl.when(pl.program_id(1) == 0)
    def _():
        carry[...] = jnp.zeros_like(carry)

    x = x_ref[...]
    t = x.shape[0]
    sub = _pick_tile(t, 512)
    tril = (_iota((sub, sub), 0) >= _iota((sub, sub), 1)).astype(BF16)
    last, blocks = carry[...], []
    for r in range(0, t, sub):
        blocks.append(_dot_exact_lhs(tril, x[r:r + sub, :], 3) + last)
        last = blocks[-1][sub - 1:sub, :]
    carry[...] = last
    c = blocks[0] if len(blocks) == 1 else jnp.concatenate(blocks, axis=0)
    slab_k, slab_q = ok_ref[...], oq_ref[...]
    for k, piece in enumerate(_split(c * LOG2E, AUG_PIECES)):
        slab_k = slab_k + lax.dot_general(pk_ref[k], piece, (((1,), (1,)), ((), ())), preferred_element_type=F32)
        slab_q = slab_q + jnp.dot(piece, pq_ref[k], preferred_element_type=F32)
    kaug_ref[...] = slab_k.astype(BF16)
    qaug_ref[...] = slab_q.astype(BF16)


def _forget_companions(x):
    b, tt, c = x.shape
    tc = _pick_tile(tt, 512, LANES)
    consts = _companion_placement()
    return pl.pallas_call(
        _cumsum_kernel, grid=(b, tt // tc),
        in_specs=[pl.BlockSpec((None, tc, c), lambda i, j: (i, j, 0))]
                 + [pl.BlockSpec(a.shape, lambda i, j, nd=a.ndim: (0,) * nd) for a in consts],
        out_specs=[pl.BlockSpec((None, LANES, tc), lambda i, j: (i, 0, j)),
                   pl.BlockSpec((None, tc, 2 * LANES), lambda i, j: (i, j, 0))],
        out_shape=[jax.ShapeDtypeStruct((b, LANES, tt), BF16), jax.ShapeDtypeStruct((b, tt, 2 * LANES), BF16)],
        scratch_shapes=[pltpu.VMEM((1, c), F32)],
        compiler_params=_params("arbitrary", "arbitrary"), name="fox_cumsum")(x, *consts)


def _attn_kernel(qi_ref, kj_ref, last_ref, q_ref, qa_ref, kt_ref, kat_ref, vt_ref, o_ref, qq_sc, m_sc, l_sc, acc_sc,
                 *, tq, tk, past):
    n = pl.program_id(1)
    i, j = qi_ref[n], kj_ref[n]
    lo = _iota((1, LANES), 1) < DH_A
    halves = (lo, jnp.logical_not(lo))
    row_lo = _iota((LANES, 1), 0) < DH_A
    row_halves = (row_lo, jnp.logical_not(row_lo))

    @pl.when(j == 0)
    def _():
        m_sc[...] = jnp.full_like(m_sc, NEG)
        l_sc[...] = jnp.zeros_like(l_sc)
        acc_sc[...] = jnp.zeros_like(acc_sc)
        lane = _iota((1, LANES), 1)
        for hd in range(H_A):
            pair, odd = divmod(hd, 2)
            q2 = q_ref[:, pair * LANES:(pair + 1) * LANES]
            qq_sc[hd, :, :LANES] = jnp.where(halves[odd], q2, jnp.zeros_like(q2))
            comp = qa_ref[:, odd * LANES:(odd + 1) * LANES]
            own = jnp.logical_and(lane >= AUG_GROUP * pair, lane < AUG_GROUP * (pair + 1))
            qq_sc[hd, :, LANES:] = jnp.where(own, comp, jnp.zeros_like(comp))

    def step(masked):
        if masked:
            visible = j * tk + _iota((1, tk), 1) <= past + i * tq + _iota((tq, 1), 0)
        scores = []
        for p in range(H_A // 2):
            kk = jnp.concatenate([kt_ref[p * LANES:(p + 1) * LANES, :], kat_ref[...]], axis=0)
            scores += [jnp.dot(qq_sc[2 * p + half], kk, preferred_element_type=F32) for half in range(2)]
        probs, alphas = [], []
        for hd in range(H_A):
            s = jnp.where(visible, scores[hd], NEG) if masked else scores[hd]
            m_prev = m_sc[hd]
            m_new = jnp.maximum(m_prev, jnp.max(s, axis=-1, keepdims=True))
            alpha = jnp.exp2(m_prev - m_new)
            pr = jnp.exp2(s - m_new)
            l_sc[hd] = alpha * l_sc[hd] + jnp.sum(pr, axis=-1, keepdims=True)
            m_sc[hd] = m_new
            alphas.append(alpha)
            probs.append(pr.astype(BF16))
        for p in range(H_A // 2):
            v2 = vt_ref[p * LANES:(p + 1) * LANES, :]
            pvs = [_dot_t(probs[2 * p + half], jnp.where(row_halves[half], v2, jnp.zeros_like(v2))) for half in range(2)]
            acc_sc[p] = jnp.where(lo, alphas[2 * p], alphas[2 * p + 1]) * acc_sc[p] + pvs[0] + pvs[1]

    some_hidden = (j + 1) * tk - 1 > past + i * tq
    pl.when(some_hidden)(functools.partial(step, True))
    pl.when(jnp.logical_not(some_hidden))(functools.partial(step, False))

    @pl.when(last_ref[n] == 1)
    def _():
        for p in range(H_A // 2):
            l2 = jnp.where(lo, l_sc[2 * p], l_sc[2 * p + 1])
            o_ref[:, p * LANES:(p + 1) * LANES] = (acc_sc[p] / l2).astype(BF16)


def _fox_attention(q, q_aug, kt_all, kt_aug, vt_all, b, t, past):
    tt = past + t
    tq = _pick_tile(t, 512, 16)
    tk = _pick_tile(tt, 512, LANES)
    nq, nk = t // tq, tt // tk
    pairs = [(i, j) for i in range(nq) for j in range(nk) if j * tk <= past + (i + 1) * tq - 1]
    qi = jnp.array([p[0] for p in pairs], I32)
    kj = jnp.array([p[1] for p in pairs], I32)
    last = jnp.array([1 if (idx + 1 == len(pairs) or pairs[idx + 1][0] != p[0]) else 0
                      for idx, p in enumerate(pairs)], I32)
    grid_spec = pltpu.PrefetchScalarGridSpec(
        num_scalar_prefetch=3, grid=(b, len(pairs)),
        in_specs=[
            pl.BlockSpec((tq, WA), lambda bi, n, qi, kj, la: (bi * nq + qi[n], 0)),
            pl.BlockSpec((tq, 2 * LANES), lambda bi, n, qi, kj, la: (bi * nq + qi[n], 0)),
            pl.BlockSpec((None, WA, tk), lambda bi, n, qi, kj, la: (bi, 0, kj[n])),
            pl.BlockSpec((None, LANES, tk), lambda bi, n, qi, kj, la: (bi, 0, kj[n])),
            pl.BlockSpec((None, WA, tk), lambda bi, n, qi, kj, la: (bi, 0, kj[n])),
        ],
        out_specs=pl.BlockSpec((tq, WA), lambda bi, n, qi, kj, la: (bi * nq + qi[n], 0)),
        scratch_shapes=[pltpu.VMEM((H_A, tq, 2 * LANES), BF16), pltpu.VMEM((H_A, tq, 1), F32),
                        pltpu.VMEM((H_A, tq, 1), F32), pltpu.VMEM((H_A // 2, tq, LANES), F32)])
    return pl.pallas_call(
        functools.partial(_attn_kernel, tq=tq, tk=tk, past=past), grid_spec=grid_spec,
        out_shape=jax.ShapeDtypeStruct((b * t, WA), BF16),
        compiler_params=_params("arbitrary", "arbitrary"), name="fox_attention")(qi, kj, last, q, q_aug, kt_all, kt_aug, vt_all)


def _gdn_kernel(u_ref, small_ref, smallt_ref, z_ref, s0_ref, conv0_ref, wconv_ref, gon_ref,
                yb_ref, sout_ref, cout_ref, s_sc, ubuf, *, chunk, cps, nsteps):
    c = pl.program_id(1)
    L, C = chunk, cps
    R = L * C
    lg = L.bit_length() - 1
    keep = CONV_W - 1

    @pl.when(c == 0)
    def _():
        s_sc[...] = s0_ref[...]
        ubuf[0:8 - keep, :] = jnp.zeros((8 - keep, CONV_DIM), F32)
        ubuf[8 - keep:8, :] = conv0_ref[...]

    ubuf[8:8 + R, :] = u_ref[...]
    w = wconv_ref[...]
    rows = ubuf[...]
    conv = rows * w[0:1, :]
    for i in range(1, CONV_W):
        conv = pltpu.roll(conv, 1, axis=0) + rows * w[i:i + 1, :]
    act = _silu(conv[8:, :])
    tail = ubuf[8 + R - keep:8 + R, :]
    ubuf[8 - keep:8, :] = tail

    @pl.when(c == nsteps - 1)
    def _():
        cout_ref[...] = tail

    sm = small_ref[...]
    smt = smallt_ref[...]
    rr, rc = _iota((R, R), 0), _iota((R, R), 1)
    same = jnp.right_shift(rr, lg) == jnp.right_shift(rc, lg)
    g_col = _dot_exact_lhs(jnp.logical_and(same, rr >= rc).astype(BF16), sm, 3)
    qn_all = act[:, :QK_B]
    qn_all = qn_all * lax.rsqrt(_group_sumsq(qn_all, DK_B) + EPS) * (DK_B ** -0.5)
    kn_all = act[:, QK_B:2 * QK_B]
    kn_all = kn_all * lax.rsqrt(_group_sumsq(kn_all, DK_B) + EPS)

    W2 = 2 * L
    lo64 = _iota((1, LANES), 1) < DK_B
    hi64 = jnp.logical_not(lo64)
    lane2 = _iota((1, W2), 1)
    lo_l, hi_l = lane2 < L, lane2 >= L
    ri, ci = _iota((L, W2), 0), jnp.bitwise_and(_iota((L, W2), 1), L - 1)
    incl, strict = ri >= ci, ri > ci
    eye = (ri == ci).astype(F32)
    n_sq = max(int(math.ceil(math.log2(L))) - 1, 0)
    sub128 = _iota((LANES, 1), 0) < DK_B

    def stack_heads(x, lo, hi):
        x = x.astype(BF16)
        zero = jnp.zeros_like(x)
        return jnp.concatenate([jnp.where(lo, x, zero), jnp.where(hi, x, zero)], axis=0)

    items = [(ch, p) for ch in range(C) for p in range(H_B // 2)]
    g_rows = []
    for ch in range(C):
        tt, jj = _iota((R, W2), 0), jnp.bitwise_and(_iota((R, W2), 1), L - 1)
        sel = jnp.logical_and(jnp.right_shift(tt, lg) == ch, jnp.bitwise_and(tt, L - 1) <= jj)
        g_rows.append(_dot_exact_rhs(smt, sel.astype(BF16), 3))

    st = {}
    for ch, p in items:
        rs = slice(ch * L, (ch + 1) * L)
        ha, hb = 2 * H_B + 2 * p, 2 * H_B + 2 * p + 1
        gca, gcb = g_col[rs, ha:ha + 1], g_col[rs, hb:hb + 1]
        gla, glb = gca[L - 1:L, :], gcb[L - 1:L, :]
        q2, k2 = qn_all[rs, p * LANES:(p + 1) * LANES], kn_all[rs, p * LANES:(p + 1) * LANES]
        beta2 = jnp.where(lo64, sm[rs, H_B + 2 * p:H_B + 2 * p + 1], sm[rs, H_B + 2 * p + 1:H_B + 2 * p + 2])
        eg2 = jnp.exp(jnp.where(lo64, gca, gcb))
        egl2 = jnp.exp(jnp.where(lo64, gla - gca, glb - gcb))
        kb2 = k2 * beta2
        gr = jnp.where(lo_l, g_rows[ch][ha:ha + 1, :], g_rows[ch][hb:hb + 1, :])
        decay = jnp.exp(jnp.where(incl, jnp.where(lo_l, gca, gcb) - gr, NEG))
        ybd = stack_heads(k2, lo64, hi64)
        va = act[rs, 2 * QK_B + 2 * p * DV_B:2 * QK_B + (2 * p + 1) * DV_B] * sm[rs, H_B + 2 * p:H_B + 2 * p + 1]
        vb = act[rs, 2 * QK_B + (2 * p + 1) * DV_B:2 * QK_B + (2 * p + 2) * DV_B] * sm[rs, H_B + 2 * p + 1:H_B + 2 * p + 2]
        st[ch, p] = dict(
            decay=decay, ybd=ybd, kb2=kb2, q2=q2,
            kbg_bd=stack_heads(kb2 * eg2, lo64, hi64), qd_bd=stack_heads(q2 * eg2, lo64, hi64),
            kd_bd=stack_heads(k2 * egl2, lo64, hi64), vst=jnp.concatenate([va, vb], axis=0),
            gl2=jnp.where(sub128, jnp.exp(gla), jnp.exp(glb)))
    for it in items:
        d = st[it]
        d["m"] = jnp.where(strict, _dot_t(d["kb2"], d["ybd"]) * d["decay"], 0.0)
        d["attn"] = jnp.where(incl, _dot_t(d["q2"], d["ybd"]) * d["decay"], 0.0)
    for it in items:
        d = st[it]
        d["t"] = eye - d["m"]
        d["mp"] = d["m"]
    for lvl in range(n_sq):
        for it in items:
            d = st[it]
            d["mp"] = _dot(d["mp"], stack_heads(d["mp"], lo_l, hi_l))
        for it in items:
            d = st[it]
            d["t"] = d["t"] + _dot(d["t"], stack_heads(d["mp"], lo_l, hi_l))
    for it in items:
        d = st[it]
        d["ust"] = _dot(stack_heads(d["t"], lo_l, hi_l), d["vst"])
        d["w2"] = _dot(d["t"], d["kbg_bd"])
    for ch in range(C):
        rs = slice(ch * L, (ch + 1) * L)
        s2s, ass = [], []
        for p in range(H_B // 2):
            d = st[ch, p]
            s2 = s_sc[p * LANES:(p + 1) * LANES, :]
            s2s.append(s2)
            ass.append(_dot(jnp.concatenate([stack_heads(d["w2"], lo64, hi64), d["qd_bd"]], axis=0), s2))
        for p in range(H_B // 2):
            d = st[ch, p]
            v_new = d["ust"] - ass[p][:W2, :]
            o_st = ass[p][W2:, :] + _dot(stack_heads(d["attn"], lo_l, hi_l), v_new)
            s_sc[p * LANES:(p + 1) * LANES, :] = s2s[p] * d["gl2"] + _dot_tl(d["kd_bd"], v_new)
            for half in range(2):
                hd = 2 * p + half
                o = o_st[half * L:(half + 1) * L, :]
                on = o * lax.rsqrt(jnp.mean(o * o, axis=-1, keepdims=True) + EPS) * gon_ref[...]
                zs = z_ref[rs, hd * DV_B:(hd + 1) * DV_B].astype(F32)
                yb_ref[rs, hd * DV_B:(hd + 1) * DV_B] = (on * zs).astype(BF16)

    @pl.when(c == nsteps - 1)
    def _():
        sout_ref[...] = s_sc[...]


def _gdn(u, small, smallt, z, s0, conv0, w_conv, g_onorm, b, t):
    clen = GDN_CHUNK if t % GDN_CHUNK == 0 else t
    assert clen & (clen - 1) == 0, clen
    cps = GDN_CHUNKS_PER_STEP if (t // clen) % GDN_CHUNKS_PER_STEP == 0 else 1
    chunk = clen * cps
    nc = t // chunk
    smallt3 = smallt.reshape(smallt.shape[0], b * nc, chunk).transpose(1, 0, 2)
    keep = CONV_W - 1
    return pl.pallas_call(
        functools.partial(_gdn_kernel, chunk=clen, cps=cps, nsteps=nc), grid=(b, nc),
        in_specs=[
            pl.BlockSpec((chunk, CONV_DIM), lambda i, j: (i * nc + j, 0)),
            pl.BlockSpec((chunk, LANES), lambda i, j: (i * nc + j, 0)),
            pl.BlockSpec((None, smallt.shape[0], chunk), lambda i, j: (i * nc + j, 0, 0)),
            pl.BlockSpec((chunk, V_B), lambda i, j: (i * nc + j, 0)),
            pl.BlockSpec((None, H_B * DK_B, DV_B), lambda i, j: (i, 0, 0)),
            pl.BlockSpec((None, keep, CONV_DIM), lambda i, j: (i, 0, 0)),
            pl.BlockSpec((CONV_W, CONV_DIM), lambda i, j: (0, 0)),
            pl.BlockSpec((1, DV_B), lambda i, j: (0, 0)),
        ],
        out_specs=[
            pl.BlockSpec((chunk, V_B), lambda i, j: (i * nc + j, 0)),
            pl.BlockSpec((None, H_B * DK_B, DV_B), lambda i, j: (i, 0, 0)),
            pl.BlockSpec((None, keep, CONV_DIM), lambda i, j: (i, 0, 0)),
        ],
        out_shape=[jax.ShapeDtypeStruct((b * t, V_B), BF16),
                   jax.ShapeDtypeStruct((b, H_B * DK_B, DV_B), F32),
                   jax.ShapeDtypeStruct((b, keep, CONV_DIM), F32)],
        scratch_shapes=[pltpu.VMEM((H_B * DK_B, DV_B), F32), pltpu.VMEM((8 + chunk, CONV_DIM), F32)],
        compiler_params=_params("arbitrary", "arbitrary"), name="gdn_chunked",
    )(u, small, smallt3, z, s0, conv0, w_conv, g_onorm)


def _merge_kernel(x_ref, ya_ref, yb_ref, gate_ref, wa_ref, wb_ref, wo_ref, gf_ref, wr_ref, br_ref,
                  x1_ref, h2_ref, logit_ref):
    br_a = jnp.dot(ya_ref[...], wa_ref[...], preferred_element_type=F32)
    br_b = jnp.dot(yb_ref[...], wb_ref[...], preferred_element_type=F32)
    merged = gate_ref[:, :D_MODEL] * br_a + gate_ref[:, D_MODEL:] * br_b
    x1 = x_ref[...] + _dot(merged, wo_ref[...])
    x1_ref[...] = x1
    h2 = _rms_rows(x1, gf_ref[...])
    h2_ref[...] = h2
    logit_ref[...] = _dot(h2, wr_ref[...]) + br_ref[...]


def _route_kernel(logit_ref, meta_ref, wgt_ref, cnt_ref, run, start, *, block):
    sweep, step = pl.program_id(0), pl.program_id(1)

    @pl.when(jnp.logical_and(sweep == 0, step == 0))
    def _():
        run[...] = jnp.zeros_like(run)
        start[...] = jnp.zeros_like(start)

    @pl.when(jnp.logical_and(sweep == 1, step == 0))
    def _():
        counts = run[...]
        cnt_ref[...] = counts
        padded = jnp.floor((counts + (block - 1)) * (1.0 / block)) * block
        before_e = (_iota((LANES, LANES), 0) < _iota((LANES, LANES), 1)).astype(BF16)
        start[...] = _dot_exact_rhs(jnp.broadcast_to(padded, (8, LANES)), before_e, 3)[0:1, :]
        run[...] = jnp.zeros_like(run)

    x = logit_ref[...]
    tm = x.shape[0]
    lane = _iota((1, LANES), 1)
    lanef = lane.astype(F32)
    big = float(LANES)

    def first_argmax(vals, valid):
        mx = jnp.max(vals, axis=-1, keepdims=True)
        idx = jnp.min(jnp.where(jnp.logical_and(vals == mx, valid), lanef, big), axis=-1, keepdims=True)
        return mx, idx.astype(I32)

    gmask = lane < N_GROUPS
    gmax, gsel = first_argmax(jnp.where(gmask, x, NEG), gmask)
    p_g = 1.0 / jnp.sum(jnp.where(gmask, jnp.exp(x - gmax), 0.0), axis=-1, keepdims=True)
    first = N_GROUPS + EXPERTS_PER_GROUP * gsel
    emask = jnp.logical_and(lane >= first, lane < first + EXPERTS_PER_GROUP)
    ev = jnp.where(emask, x, NEG)
    m1, i1 = first_argmax(ev, emask)
    emask2 = jnp.logical_and(emask, lane != i1)
    m2, i2 = first_argmax(jnp.where(emask2, x, NEG), emask2)
    e21 = jnp.exp(m2 - m1)
    w1 = p_g / (1.0 + e21)
    w2 = p_g * e21 / (1.0 + e21)
    e1, e2 = i1 - N_GROUPS, i2 - N_GROUPS
    oh1, oh2 = lane == e1, lane == e2
    onehot = jnp.logical_or(oh1, oh2).astype(BF16)
    strict = (_iota((tm, tm), 0) > _iota((tm, tm), 1)).astype(BF16)
    slot = jnp.dot(strict, onehot, preferred_element_type=F32) + run[...] + start[...]
    run[...] = run[...] + jnp.sum(onehot.astype(F32), axis=0, keepdims=True)

    @pl.when(sweep == 1)
    def _():
        d1 = jnp.sum(jnp.where(oh1, slot, 0.0), axis=-1, keepdims=True).astype(I32)
        d2 = jnp.sum(jnp.where(oh2, slot, 0.0), axis=-1, keepdims=True).astype(I32)
        l8 = _iota((1, 8), 1)
        meta_ref[...] = jnp.where(l8 == 0, e1, jnp.where(l8 == 1, e2, jnp.where(l8 == 2, d1, jnp.where(l8 == 3, d2, 0))))
        wgt_ref[...] = jnp.where(l8 == 0, w1, jnp.where(l8 == 1, w2, 0.0))


def _row_copy(src, dst, sem, s, d):
    return pltpu.make_async_copy(src.at[pl.ds(s, 1)], dst.at[pl.ds(d, 1)], sem)


DMA_UNROLL = 8


DISPATCH_SLOTS = 3


def _dispatch_kernel(seg_ref, dest_ref, h_ref, xs_ref, buf, sem_in, sem_out, sem_pad, *, td, nsteps, pad_bits):
    step = pl.program_id(0)
    slot = lax.rem(step, DISPATCH_SLOTS)

    def tile_copy(tile, s):
        return pltpu.make_async_copy(h_ref.at[pl.ds(tile * td, td)], buf.at[s], sem_in.at[s])

    def drain(s):
        def body(t, carry):
            _row_copy(buf.at[s], xs_ref, sem_out.at[s], 0, 0).wait()
            _row_copy(buf.at[s], xs_ref, sem_out.at[s], 0, 0).wait()
            return carry
        lax.fori_loop(0, td, body, 0, unroll=DMA_UNROLL)

    @pl.when(step == 0)
    def _():
        tile_copy(0, 0).start()

    @pl.when(step >= 2)
    def _():
        drain(lax.rem(step + 1, DISPATCH_SLOTS))

    @pl.when(step + 1 < nsteps)
    def _():
        tile_copy(step + 1, lax.rem(step + 1, DISPATCH_SLOTS)).start()

    tile_copy(step, slot).wait()

    def issue(t, carry):
        _row_copy(buf.at[slot], xs_ref, sem_out.at[slot], t, dest_ref[0, t]).start(priority=0)
        _row_copy(buf.at[slot], xs_ref, sem_out.at[slot], t, dest_ref[0, td + t]).start(priority=1)
        return carry

    lax.fori_loop(0, td, issue, 0, unroll=DMA_UNROLL)

    @pl.when(step == nsteps - 1)
    def _():
        if nsteps >= 2:
            drain(lax.rem(step + 2, DISPATCH_SLOTS))
        drain(slot)
        buf[0] = jnp.zeros((td, D_MODEL), F32)
        runs = [1 << k for k in reversed(range(3, pad_bits))]

        def run_copy(rows, at):
            return pltpu.make_async_copy(buf.at[0, pl.ds(0, rows)], xs_ref.at[pl.ds(pl.multiple_of(at, 8), rows)], sem_pad)

        def fill(wait):
            def body(e, carry):
                first, end = seg_ref[e], seg_ref[N_EXPERTS + e]
                at = jnp.bitwise_and(first + 7, -8)
                for r in range(7):
                    @pl.when(first + r < at)
                    def _(r=r):
                        c = _row_copy(buf.at[0], xs_ref, sem_pad, 0, 0 if wait else first + r)
                        c.wait() if wait else c.start()
                gap = end - at
                for rows in runs:
                    bit = jnp.bitwise_and(gap, rows) != 0

                    @pl.when(bit)
                    def _(rows=rows, at=at):
                        run_copy(rows, 0).wait() if wait else run_copy(rows, at).start()

                    at = at + jnp.where(bit, rows, 0)
                return carry
            lax.fori_loop(0, N_EXPERTS, body, 0)

        fill(False)
        fill(True)
        tail = lambda c: pltpu.make_async_copy(buf.at[0], xs_ref.at[pl.ds(c * td, td)], sem_pad)
        first, stop = lax.div(seg_ref[2 * N_EXPERTS - 1], td), xs_ref.shape[0] // td
        lax.fori_loop(first, stop, lambda c, x: (tail(c).start(), x)[1], 0)
        lax.fori_loop(first, stop, lambda c, x: (tail(0).wait(), x)[1], 0)


def _expert_kernel(be_ref, nu_ref, x_ref, w1_ref, w3_ref, w2_ref, y_ref, w1_sc, w3_sc, w2_sc):
    blk = pl.program_id(0)
    used = blk < nu_ref[0]
    new_expert = jnp.logical_or(blk == 0, be_ref[blk] != be_ref[jnp.maximum(blk - 1, 0)])

    @pl.when(jnp.logical_and(used, new_expert))
    def _():
        w1_sc[...] = w1_ref[...].astype(BF16)
        w3_sc[...] = w3_ref[...].astype(BF16)
        w2_sc[...] = w2_ref[...].astype(BF16)

    @pl.when(used)
    def _():
        x = x_ref[...].astype(BF16)
        a = jnp.dot(x, w1_sc[...], preferred_element_type=F32)
        g = jnp.dot(x, w3_sc[...], preferred_element_type=F32)
        y_ref[...] = _dot(_silu(a) * g, w2_sc[...])

    @pl.when(jnp.logical_not(used))
    def _():
        y_ref[...] = jnp.zeros_like(y_ref)


def _combine_kernel(dest_ref, next_ref, x1_ref, wgt_ref, yb_ref, o_ref, g0, g1, sems, *, tc):
    step, nsteps = pl.program_id(0), pl.num_programs(0)
    slot = step % 2

    def issue(idx_ref, s):
        def body(t, carry):
            _row_copy(yb_ref, g0.at[s], sems.at[s], idx_ref[0, t], t).start(priority=0)
            _row_copy(yb_ref, g1.at[s], sems.at[s], idx_ref[0, tc + t], t).start(priority=1)
            return carry
        lax.fori_loop(0, tc, body, 0, unroll=DMA_UNROLL)

    @pl.when(step == 0)
    def _():
        issue(dest_ref, 0)

    @pl.when(step + 1 < nsteps)
    def _():
        issue(next_ref, 1 - slot)

    def drain(t, carry):
        _row_copy(yb_ref, g0.at[slot], sems.at[slot], 0, 0).wait()
        _row_copy(yb_ref, g1.at[slot], sems.at[slot], 0, 0).wait()
        return carry

    lax.fori_loop(0, tc, drain, 0, unroll=DMA_UNROLL)
    wg = wgt_ref[...]
    o_ref[...] = x1_ref[...] + (g0[slot] * wg[:, 0:1] + g1[slot] * wg[:, 1:2])


def _hier_moe(x1, h2, logits, w_e1, w_e3, w_e2):
    n = x1.shape[0]
    tr = _pick_tile(n, 512)
    bm = MOE_BLOCK if 2 * n >= 4 * MOE_BLOCK * N_EXPERTS else MOE_BLOCK // 4
    assert bm & (bm - 1) == 0
    meta, wgt, cnt = pl.pallas_call(
        functools.partial(_route_kernel, block=bm), grid=(2, n // tr),
        in_specs=[pl.BlockSpec((tr, LANES), lambda s, i: (i, 0))],
        out_specs=[pl.BlockSpec((tr, 8), lambda s, i: (i * s, 0)), pl.BlockSpec((tr, 8), lambda s, i: (i * s, 0)),
                   pl.BlockSpec((1, LANES), lambda s, i: (0, 0))],
        out_shape=[jax.ShapeDtypeStruct((n, 8), I32), jax.ShapeDtypeStruct((n, 8), F32),
                   jax.ShapeDtypeStruct((1, LANES), F32)],
        scratch_shapes=[pltpu.VMEM((1, LANES), F32), pltpu.VMEM((1, LANES), F32)],
        compiler_params=_params("arbitrary", "arbitrary"), name="moe_route")(logits)

    counts = cnt[0, :N_EXPERTS].astype(I32)
    padded = ((counts + bm - 1) // bm) * bm
    pad_end = jnp.cumsum(padded)
    nblk = -(-(2 * n) // bm) + N_EXPERTS
    n_used = (pad_end[-1] // bm).astype(I32).reshape(1)
    blk_first = jnp.arange(nblk, dtype=I32) * bm
    blk_expert = jnp.minimum(jnp.sum((pad_end[None, :] <= blk_first[:, None]).astype(I32), axis=1), N_EXPERTS - 1)
    seg = jnp.concatenate([pad_end - padded + counts, pad_end])

    td = _pick_tile(n, min(256, bm))
    assert bm % td == 0
    dest = meta[:, 2:4].reshape(n // td, td, 2).transpose(0, 2, 1).reshape(n // td, 1, 2 * td)
    xs = pl.pallas_call(
        functools.partial(_dispatch_kernel, td=td, nsteps=n // td, pad_bits=bm.bit_length() - 1),
        grid_spec=pltpu.PrefetchScalarGridSpec(
            num_scalar_prefetch=1, grid=(n // td,),
            in_specs=[pl.BlockSpec((None, 1, 2 * td), lambda i, sg: (i, 0, 0), memory_space=pltpu.SMEM),
                      pl.BlockSpec(memory_space=pl.ANY)],
            out_specs=pl.BlockSpec(memory_space=pl.ANY),
            scratch_shapes=[pltpu.VMEM((DISPATCH_SLOTS, td, D_MODEL), F32),
                            pltpu.SemaphoreType.DMA((DISPATCH_SLOTS,)), pltpu.SemaphoreType.DMA((DISPATCH_SLOTS,)),
                            pltpu.SemaphoreType.DMA(())]),
        out_shape=jax.ShapeDtypeStruct((nblk * bm, D_MODEL), F32),
        compiler_params=_params("arbitrary"), name="moe_dispatch")(seg, dest, h2)

    xmap = lambda i, be, nu: (jnp.minimum(i, nu[0] - 1), 0)
    grid_spec = pltpu.PrefetchScalarGridSpec(
        num_scalar_prefetch=2, grid=(nblk,),
        in_specs=[pl.BlockSpec((bm, D_MODEL), xmap),
                  pl.BlockSpec((None, D_MODEL, D_EXPERT), lambda i, be, nu: (be[i], 0, 0)),
                  pl.BlockSpec((None, D_MODEL, D_EXPERT), lambda i, be, nu: (be[i], 0, 0)),
                  pl.BlockSpec((None, D_EXPERT, D_MODEL), lambda i, be, nu: (be[i], 0, 0))],
        out_specs=pl.BlockSpec((bm, D_MODEL), lambda i, be, nu: (i, 0)),
        scratch_shapes=[pltpu.VMEM((D_MODEL, D_EXPERT), BF16), pltpu.VMEM((D_MODEL, D_EXPERT), BF16),
                        pltpu.VMEM((D_EXPERT, D_MODEL), BF16)])
    yb = pl.pallas_call(
        _expert_kernel, grid_spec=grid_spec, out_shape=jax.ShapeDtypeStruct((nblk * bm, D_MODEL), F32),
        compiler_params=_params("arbitrary"), name="moe_experts")(blk_expert, n_used, xs, w_e1, w_e3, w_e2)

    tc = td
    return pl.pallas_call(
        functools.partial(_combine_kernel, tc=tc), grid=(n // tc,),
        in_specs=[pl.BlockSpec((None, 1, 2 * tc), lambda i: (i, 0, 0), memory_space=pltpu.SMEM),
                  pl.BlockSpec((None, 1, 2 * tc), lambda i: (jnp.minimum(i + 1, n // tc - 1), 0, 0),
                               memory_space=pltpu.SMEM),
                  pl.BlockSpec((tc, D_MODEL), lambda i: (i, 0)),
                  pl.BlockSpec((tc, 8), lambda i: (i, 0)),
                  pl.BlockSpec(memory_space=pl.ANY)],
        out_specs=pl.BlockSpec((tc, D_MODEL), lambda i: (i, 0)),
        out_shape=jax.ShapeDtypeStruct((n, D_MODEL), F32),
        scratch_shapes=[pltpu.VMEM((2, tc, D_MODEL), F32), pltpu.VMEM((2, tc, D_MODEL), F32),
                        pltpu.SemaphoreType.DMA((2,))],
        compiler_params=_params("arbitrary"), name="moe_combine")(dest, dest, x1, wgt, yb)


def _prep_weights(l, g_mix, w_in, b_f, g_qnorm, g_knorm, w_conv, a_log, dt_bias, g_onorm, b_gate,
                  w_br_a, w_br_b, w_out, g_ffn, w_grp, b_grp, w_exp, b_exp, w_e1, w_e3, w_e2):
    w = w_in[l]
    offs = [0]
    for s in (WA, WA, WA, H_A, QK_B, QK_B, V_B, H_B, H_B, V_B, D_MODEL, D_MODEL):
        offs.append(offs[-1] + s)
    col = lambda i, j=None: w[:, offs[i]:offs[(i if j is None else j) + 1]]
    n_small = H_A + 2 * H_B
    w_small = jnp.concatenate([col(3), col(7), col(8)], axis=1)
    zeros8 = jnp.zeros((H_A,), F32)
    bias = jnp.concatenate([b_f[l], zeros8, dt_bias[l]])
    alog = jnp.concatenate([zeros8, zeros8, a_log[l]])
    pad = lambda v, n: jnp.pad(v, (0, n - v.shape[0]))
    rows_t = 32
    return dict(
        g_mix=g_mix[l][None, :],
        wq=col(0).astype(BF16), wkt=col(1).T.astype(BF16), wvt=col(2).T.astype(BF16),
        gq=jnp.tile(g_qnorm[l], H_A)[None, :], gkt=jnp.tile(g_knorm[l], H_A)[:, None],
        ws=jnp.pad(w_small, ((0, 0), (0, LANES - n_small))).astype(BF16),
        wst=jnp.pad(w_small.T, ((0, rows_t - n_small), (0, 0))).astype(BF16),
        wu=col(4, 6).astype(BF16),
        bias=pad(bias, LANES)[None, :], alog=pad(alog, LANES)[None, :],
        biast=pad(bias, rows_t)[:, None], alogt=pad(alog, rows_t)[:, None],
        wz=col(9).astype(BF16), wg=col(10, 11).astype(BF16), bg=b_gate[l][None, :],
        w_conv=w_conv[l], g_onorm=g_onorm[l][None, :],
        w_br_a=w_br_a[l].astype(BF16), w_br_b=w_br_b[l].astype(BF16), w_out=w_out[l].astype(BF16),
        g_ffn=g_ffn[l][None, :],
        wr=jnp.pad(jnp.concatenate([w_grp[l], w_exp[l]], axis=1), ((0, 0), (0, LANES - N_GROUPS - N_EXPERTS))).astype(BF16),
        br=pad(jnp.concatenate([b_grp[l], b_exp[l]]), LANES)[None, :],
        w_e1=w_e1[l], w_e3=w_e3[l], w_e2=w_e2[l],
    )


def _layer(x, past_k, past_v, past_logf, s0, conv0, p):
    b, t, _ = x.shape
    past = past_k.shape[1]
    n = b * t
    xf = x.reshape(n, D_MODEL)
    tm = tm_in = _pick_tile(n, 1024)

    q16, kt32, kt16, vt32, vt16 = _inproj_fox(xf, p, b, t)
    small, smallt, u = _row_call(
        _inproj_gdn_kernel, n, tm_in, [xf],
        [p["g_mix"], p["ws"], p["wst"], p["wu"], p["bias"], p["alog"], p["biast"], p["alogt"]],
        [((n, LANES), F32, False), ((32, n), F32, True), ((n, CONV_DIM), F32, False)], "inproj_gdn")
    z, gates = _row_call(
        _inproj_gate_kernel, n, tm_in, [xf], [p["g_mix"], p["wz"], p["wg"], p["bg"]],
        [((n, V_B), BF16, False), ((n, 2 * D_MODEL), BF16, False)], "inproj_gate")

    logf = small[:, :H_A].reshape(b, t, H_A)
    past_lanes = jnp.pad(past_logf.astype(F32), ((0, 0), (0, 0), (0, LANES - H_A)))
    k_aug, q_aug = _forget_companions(jnp.concatenate([past_lanes, small.reshape(b, t, LANES)], axis=1))
    feature_major = lambda c: c.transpose(0, 2, 3, 1).reshape(b, WA, past).astype(BF16)
    kt_all = jnp.concatenate([feature_major(past_k), kt16], axis=2)
    vt_all = jnp.concatenate([feature_major(past_v), vt16], axis=2)
    y_a = _fox_attention(q16, q_aug[:, past:].reshape(n, 2 * LANES), kt_all, k_aug, vt_all, b, t, past)

    y_b, s_new, conv_new = _gdn(u, small, smallt, z, s0.reshape(b, H_B * DK_B, DV_B), conv0,
                                p["w_conv"], p["g_onorm"], b, t)

    x1, h2, logits = _row_call(
        _merge_kernel, n, tm, [xf, y_a, y_b, gates],
        [p["w_br_a"], p["w_br_b"], p["w_out"], p["g_ffn"], p["wr"], p["br"]],
        [((n, D_MODEL), F32, False), ((n, D_MODEL), F32, False), ((n, LANES), F32, False)], "merge_outproj")
    x2 = _hier_moe(x1, h2, logits, p["w_e1"], p["w_e3"], p["w_e2"])
    token_major = lambda c: c.reshape(b, H_A, DH_A, t).transpose(0, 3, 1, 2)
    return (x2.reshape(b, t, D_MODEL), token_major(kt32), token_major(vt32), logf,
            s_new.reshape(b, H_B, DK_B, DV_B), conv_new)


def kernel(x_prompt, x_sample, cache_fox_k, cache_fox_v, cache_fox_logf, state_gdn, state_gdn_conv, g_mix, w_in, b_f, g_qnorm, g_knorm, w_conv, a_log, dt_bias, g_onorm, b_gate, w_br_a, w_br_b, w_out, g_ffn, w_grp, b_grp, w_exp, b_exp, w_e1, w_e3, w_e2):
    depth = w_in.shape[0]
    bp = x_prompt.shape[0]
    yp, ys = x_prompt, x_sample
    outs_p, outs_s = [], []
    for l in range(depth):
        p = _prep_weights(l, g_mix, w_in, b_f, g_qnorm, g_knorm, w_conv, a_log, dt_bias, g_onorm, b_gate,
                          w_br_a, w_br_b, w_out, g_ffn, w_grp, b_grp, w_exp, b_exp, w_e1, w_e3, w_e2)
        ys, *rest_s = _layer(ys, cache_fox_k[l], cache_fox_v[l], cache_fox_logf[l], state_gdn[l], state_gdn_conv[l], p)
        yp, *rest_p = _layer(
            yp, jnp.zeros((bp, 0, H_A, DH_A), F32), jnp.zeros((bp, 0, H_A, DH_A), F32), jnp.zeros((bp, 0, H_A), F32),
            jnp.zeros((bp, H_B, DK_B, DV_B), F32), jnp.zeros((bp, CONV_W - 1, CONV_DIM), F32), p)
        outs_p.append(rest_p)
        outs_s.append(rest_s)
    stack = lambda outs, i: jnp.stack([o[i] for o in outs])
    return (yp, ys,
            stack(outs_p, 0), stack(outs_p, 1), stack(outs_p, 2), stack(outs_p, 3), stack(outs_p, 4),
            stack(outs_s, 0), stack(outs_s, 1), stack(outs_s, 2), stack(outs_s, 3), stack(outs_s, 4))
```
